```python
import math
import jax, jax.numpy as jnp
from jax import lax
import numpy as np

D_MODEL = 1024
BATCH = 2
SEQ = 16384
DEPTH = 4

N_EVEN = (DEPTH + 1) // 2
N_ODD = DEPTH // 2
D_FF = 1408
N_FFN = 2
EPS = 1e-6
Q_BLOCK = 128

DIFF_HEADS = 4
DIFF_DH = 64
DIFF_DV = 2 * DIFF_DH
GLA_HEADS = 4
GLA_DK = 64
GLA_DV = 128
GLA_RANK = 16
GLA_TAU = 16.0
GLA_CHUNK = 64
SB_HEADS = 8
SB_DH = D_MODEL // SB_HEADS

HYB_SPLITS = (DIFF_HEADS * 2 * DIFF_DH, DIFF_HEADS * 2 * DIFF_DH, DIFF_HEADS * DIFF_DV,
              GLA_HEADS * GLA_DK, GLA_HEADS * GLA_DK, GLA_HEADS * GLA_DV, GLA_HEADS * GLA_DV, GLA_RANK)
HYB_IN = sum(HYB_SPLITS)
HYB_OUT = DIFF_HEADS * DIFF_DV + GLA_HEADS * GLA_DV

kernel_name = "hybrid_diffattn_gla_stickbreak_macaron"


def rmsnorm(x, g):
    xf = x.astype(jnp.float32)
    y = xf * lax.rsqrt(jnp.mean(xf * xf, axis=-1, keepdims=True) + EPS)
    return (y * g.astype(jnp.float32)).astype(x.dtype)


def swiglu(x, wg, wu, wd):
    return (jax.nn.silu(x @ wg) * (x @ wu)) @ wd


def to_blocks(t, size):
    b, s = t.shape[:2]
    t = t.reshape((b, s // size, size) + t.shape[2:])
    return jnp.moveaxis(t, 1, 0)


def from_blocks(t):
    t = jnp.moveaxis(t, 0, 1)
    return t.reshape((t.shape[0], t.shape[1] * t.shape[2]) + t.shape[3:])


def strict_upper(n, dtype):
    r = jnp.arange(n)
    return (r[:, None] > r[None, :]).astype(dtype)


def diff_attention(q, k, v, lam_params, subln_g, lambda_init):
    b, s = q.shape[:2]
    f32 = jnp.float32
    lp = lam_params.astype(f32)
    lam = jnp.exp(jnp.sum(lp[0] * lp[1])) - jnp.exp(jnp.sum(lp[2] * lp[3])) + lambda_init
    qf = q.astype(f32) * (DIFF_DH ** -0.5)
    kf = k.astype(f32)
    vf = v.astype(f32)
    outs = []
    for i in range(s // Q_BLOCK):
        start, end = i * Q_BLOCK, (i + 1) * Q_BLOCK
        sc = jnp.einsum('bqhd,bkhd->bhqk', qf[:, start:end], kf[:, :end])
        mask = jnp.arange(end)[None, :] <= (start + jnp.arange(Q_BLOCK))[:, None]
        p = jax.nn.softmax(jnp.where(mask, sc, -jnp.inf), axis=-1)
        p = p.reshape(b, DIFF_HEADS, 2, Q_BLOCK, end)
        p = p[:, :, 0] - lam * p[:, :, 1]
        outs.append(jnp.einsum('bhqk,bkhv->bqhv', p, vf[:, :end]))
    o = jnp.concatenate(outs, axis=1)
    o = rmsnorm(o, subln_g) * (1.0 - lambda_init)
    return o.reshape(b, s, DIFF_HEADS * DIFF_DV).astype(q.dtype)


def gla_chunked(q, k, v, log_a):
    b = q.shape[0]
    tri = jnp.tril(jnp.ones((GLA_CHUNK, GLA_CHUNK), dtype=bool))

    def prep(t):
        return jnp.moveaxis(to_blocks(t, GLA_CHUNK), 3, 2)

    def step(state, inp):
        qc, kc, vc, lac = inp
        cum = jnp.cumsum(lac, axis=2)
        o_inter = jnp.einsum('bhtk,bhkv->bhtv', qc * jnp.exp(cum), state)
        rel = cum[:, :, :, None, :] - cum[:, :, None, :, :]
        decay = jnp.exp(jnp.where(tri[:, :, None], rel, -jnp.inf))
        attn = jnp.einsum('bhtk,bhsk,bhtsk->bhts', qc, kc, decay)
        o = o_inter + jnp.einsum('bhts,bhsv->bhtv', attn, vc)
        last = cum[:, :, -1, :]
        k_dec = kc * jnp.exp(last[:, :, None, :] - cum)
        state = state * jnp.exp(last)[..., None] + jnp.einsum('bhsk,bhsv->bhkv', k_dec, vc)
        return state, o

    state0 = jnp.zeros((b, GLA_HEADS, GLA_DK, GLA_DV), jnp.float32)
    _, o = lax.scan(step, state0, (prep(q), prep(k), prep(v), prep(log_a)))
    return from_blocks(jnp.moveaxis(o, 2, 3))


def hybrid_mixer(h, w_in, w_out, lam_params, subln_g, w_a2, b_a, norm_g, lambda_init):
    b, s, _ = h.shape
    proj = h @ w_in
    idx = [int(i) for i in np.cumsum(HYB_SPLITS)[:-1]]
    dq, dk, dv, gq, gk, gv, gg, ga = jnp.split(proj, idx, axis=-1)
    a_out = diff_attention(dq.reshape(b, s, 2 * DIFF_HEADS, DIFF_DH),
                           dk.reshape(b, s, 2 * DIFF_HEADS, DIFF_DH),
                           dv.reshape(b, s, DIFF_HEADS, DIFF_DV),
                           lam_params, subln_g, lambda_init)
    f32 = jnp.float32
    log_a = jax.nn.log_sigmoid((ga @ w_a2 + b_a).astype(f32)) / GLA_TAU
    o_b = gla_chunked(gq.reshape(b, s, GLA_HEADS, GLA_DK).astype(f32) * (GLA_DK ** -0.5),
                      gk.reshape(b, s, GLA_HEADS, GLA_DK).astype(f32),
                      gv.reshape(b, s, GLA_HEADS, GLA_DV).astype(f32),
                      log_a.reshape(b, s, GLA_HEADS, GLA_DK))
    o_b = rmsnorm(o_b, norm_g) * jax.nn.silu(gg.reshape(b, s, GLA_HEADS, GLA_DV).astype(f32))
    b_out = o_b.reshape(b, s, GLA_HEADS * GLA_DV).astype(h.dtype)
    return jnp.concatenate([a_out, b_out], axis=-1) @ w_out


def stick_breaking(h, w_qkv, w_out):
    b, s, _ = h.shape
    f32 = jnp.float32
    q, k, v = jnp.split(h @ w_qkv, 3, axis=-1)
    qf = q.reshape(b, s, SB_HEADS, SB_DH).astype(f32) * (SB_DH ** -0.5)
    kf = k.reshape(b, s, SB_HEADS, SB_DH).astype(f32)
    vf = v.reshape(b, s, SB_HEADS, SB_DH).astype(f32)
    within_tri = strict_upper(Q_BLOCK, f32)
    outs = []
    for i in range(s // Q_BLOCK):
        start, end = i * Q_BLOCK, (i + 1) * Q_BLOCK
        nk = end // Q_BLOCK
        z = jnp.einsum('bqhd,bkhd->bhqk', qf[:, start:end], kf[:, :end])
        mask = jnp.arange(end)[None, :] < (start + jnp.arange(Q_BLOCK))[:, None]
        log_keep = jnp.where(mask, -jax.nn.softplus(z), 0.0)
        lk = log_keep.reshape(b, SB_HEADS, Q_BLOCK, nk, Q_BLOCK)
        within = jnp.einsum('bhqnj,js->bhqns', lk, within_tri)
        later = jnp.einsum('bhqm,mn->bhqn', jnp.sum(lk, axis=-1), strict_upper(nk, f32))
        suffix = (within + later[..., None]).reshape(b, SB_HEADS, Q_BLOCK, end)
        att = jnp.where(mask, jnp.exp(z + log_keep + suffix), 0.0)
        outs.append(jnp.einsum('bhqk,bkhd->bqhd', att, vf[:, :end]))
    o = jnp.concatenate(outs, axis=1)
    return o.reshape(b, s, D_MODEL).astype(h.dtype) @ w_out


def setup_inputs(seed: int = 0) -> dict:
    key = jax.random.key(seed)
    ks = jax.random.split(key, 20)
    f32 = jnp.float32

    def nrm(k, shape, fan_in):
        return jax.random.normal(k, shape, f32) * (fan_in ** -0.5)

    def gain(k, shape):
        return 1.0 + 0.02 * jax.random.normal(k, shape, f32)

    return {
        "x": jax.random.normal(ks[0], (BATCH, SEQ, D_MODEL), f32),
        "ffn_pre_g": gain(ks[1], (DEPTH, N_FFN, D_MODEL)),
        "ffn_post_g": gain(ks[2], (DEPTH, N_FFN, D_MODEL)),
        "ffn_w_gate": nrm(ks[3], (DEPTH, N_FFN, D_MODEL, D_FF), D_MODEL),
        "ffn_w_up": nrm(ks[4], (DEPTH, N_FFN, D_MODEL, D_FF), D_MODEL),
        "ffn_w_down": nrm(ks[5], (DEPTH, N_FFN, D_FF, D_MODEL), D_FF),
        "mix_pre_g": gain(ks[6], (DEPTH, D_MODEL)),
        "mix_post_g": gain(ks[7], (DEPTH, D_MODEL)),
        "hyb_w_in": nrm(ks[8], (N_EVEN, D_MODEL, HYB_IN), D_MODEL),
        "hyb_w_out": nrm(ks[9], (N_EVEN, HYB_OUT, D_MODEL), HYB_OUT),
        "diff_lambda": 0.1 * jax.random.normal(ks[10], (N_EVEN, 4, DIFF_DH), f32),
        "diff_subln_g": gain(ks[11], (N_EVEN, DIFF_DV)),
        "gla_w_a2": nrm(ks[12], (N_EVEN, GLA_RANK, GLA_HEADS * GLA_DK), GLA_RANK),
        "gla_b_a": 0.1 * jax.random.normal(ks[13], (N_EVEN, GLA_HEADS * GLA_DK), f32),
        "gla_norm_g": gain(ks[14], (N_EVEN, GLA_DV)),
        "sb_w_qkv": nrm(ks[15], (N_ODD, D_MODEL, 3 * D_MODEL), D_MODEL),
        "sb_w_out": nrm(ks[16], (N_ODD, D_MODEL, D_MODEL), D_MODEL),
    }


def reference(x, ffn_pre_g, ffn_post_g, ffn_w_gate, ffn_w_up, ffn_w_down, mix_pre_g, mix_post_g,
              hyb_w_in, hyb_w_out, diff_lambda, diff_subln_g, gla_w_a2, gla_b_a, gla_norm_g,
              sb_w_qkv, sb_w_out):
    h = x
    for layer in range(DEPTH):
        f = swiglu(rmsnorm(h, ffn_pre_g[layer, 0]), ffn_w_gate[layer, 0], ffn_w_up[layer, 0], ffn_w_down[layer, 0])
        h = h + 0.5 * rmsnorm(f, ffn_post_g[layer, 0])
        m = rmsnorm(h, mix_pre_g[layer])
        if layer % 2 == 0:
            e = layer // 2
            lambda_init = 0.8 - 0.6 * math.exp(-0.3 * layer)
            m = hybrid_mixer(m, hyb_w_in[e], hyb_w_out[e], diff_lambda[e], diff_subln_g[e],
                             gla_w_a2[e], gla_b_a[e], gla_norm_g[e], lambda_init)
        else:
            o = layer // 2
            m = stick_breaking(m, sb_w_qkv[o], sb_w_out[o])
        h = h + rmsnorm(m, mix_post_g[layer])
        f = swiglu(rmsnorm(h, ffn_pre_g[layer, 1]), ffn_w_gate[layer, 1], ffn_w_up[layer, 1], ffn_w_down[layer, 1])
        h = h + 0.5 * rmsnorm(f, ffn_post_g[layer, 1])
    return h
```

```python
import functools
import math

import jax
import jax.numpy as jnp
from jax import lax
from jax.experimental import pallas as pl
from jax.experimental.pallas import tpu as pltpu

F32 = jnp.float32
BF16 = jnp.bfloat16

EPS = 1e-6
DIFF_HEADS = 4
DIFF_DH = 64
DIFF_DV = 2 * DIFF_DH
GLA_HEADS = 4
GLA_DK = 64
GLA_DV = 128
GLA_RANK = 16
GLA_TAU = 16.0
GLA_CHUNK = 64
GLA_SUB = 16
SB_HEADS = 8

LANES = 128
VMEM_LIMIT = 56 * 1024 * 1024

TOKEN_TILE = 512
ATTN_TILE = 256
GLA_TILE = 512


def _cparams(sem):
    return pltpu.CompilerParams(dimension_semantics=sem, vmem_limit_bytes=VMEM_LIMIT)


def _rms(x, g):
    return x * lax.rsqrt(jnp.mean(x * x, axis=-1, keepdims=True) + EPS) * g


def _dot(a, b):
    return jnp.dot(a, b, preferred_element_type=F32)


def _dot_nt(a, b):
    return lax.dot_general(a, b, (((1,), (1,)), ((), ())), preferred_element_type=F32)


def _dot_tn(a, b):
    return lax.dot_general(a, b, (((0,), (0,)), ((), ())), preferred_element_type=F32)


def _split3(x):
    hi = x.astype(BF16)
    r1 = x - hi.astype(F32)
    mid = r1.astype(BF16)
    lo = (r1 - mid.astype(F32)).astype(BF16)
    return hi, mid, lo


def _softplus_parts(z):
    return jnp.maximum(z, 0.0), jnp.log(1.0 + jnp.exp(-jnp.abs(z)))


def _ffn_kernel(h_ref, gpre_ref, wg_ref, wu_ref, wd_ref, gpost_ref, o_ref):
    h = h_ref[...]
    xn = _rms(h, gpre_ref[...]).astype(BF16)
    gate = _dot(xn, wg_ref[...])
    up = _dot(xn, wu_ref[...])
    act = (gate * jax.nn.sigmoid(gate) * up).astype(BF16)
    f = _dot(act, wd_ref[...])
    o_ref[...] = h + 0.5 * _rms(f, gpost_ref[...])


def _ffn(h, g_pre, wg, wu, wd, g_post):
    n, d = h.shape
    dff = wg.shape[1]
    tm = min(TOKEN_TILE, n)
    row = lambda i: (i, 0)
    fixed = lambda i: (0, 0)
    return pl.pallas_call(
        _ffn_kernel,
        grid=(n // tm,),
        in_specs=[pl.BlockSpec((tm, d), row), pl.BlockSpec((1, d), fixed),
                  pl.BlockSpec((d, dff), fixed), pl.BlockSpec((d, dff), fixed),
                  pl.BlockSpec((dff, d), fixed), pl.BlockSpec((1, d), fixed)],
        out_specs=pl.BlockSpec((tm, d), row),
        out_shape=jax.ShapeDtypeStruct((n, d), F32),
        compiler_params=_cparams(("parallel",)),
        name="ffn",
    )(h, g_pre.reshape(1, d), wg.astype(BF16), wu.astype(BF16), wd.astype(BF16), g_post.reshape(1, d))


def _proj_even_kernel(h_ref, g_ref, w_ref, wga_ref, wa2_ref, ba_ref,
                      dq_ref, dk_ref, dv_ref, gq_ref, gk_ref, la_ref, gv_ref, gg_ref):
    m = _rms(h_ref[...], g_ref[...]).astype(BF16)
    wd = dq_ref.shape[1]

    def seg(i):
        return _dot(m, w_ref[:, i * wd:(i + 1) * wd])

    dq_ref[...] = (seg(0) * DIFF_DH ** -0.5).astype(BF16)
    dk_ref[...] = seg(1).astype(BF16)
    dv_ref[...] = seg(2).astype(BF16)
    gq_ref[...] = seg(3) * GLA_DK ** -0.5
    gk_ref[...] = seg(4)
    gv_ref[...] = seg(5)
    gg_ref[...] = seg(6)
    ga = _dot(m, wga_ref[...])
    ga_hi, ga_lo, _ = _split3(ga)
    w_hi, w_lo, _ = _split3(wa2_ref[...])
    x = _dot(ga_hi, w_hi) + _dot(ga_lo, w_hi) + _dot(ga_hi, w_lo) + ba_ref[...]
    relu_neg, log_term = _softplus_parts(-x)
    la_ref[...] = -(relu_neg + log_term) * (1.0 / GLA_TAU)


def _pad_heads(w, heads, dh):
    lead = w.shape[:-1]
    w = w.reshape(lead + (heads, dh))
    w = jnp.pad(w, [(0, 0)] * len(lead) + [(0, 0), (0, LANES - dh)])
    return w.reshape(lead + (heads * LANES,))


def _proj_even(h, g, w_in, w_a2, b_a):
    n, d = h.shape
    tm = min(TOKEN_TILE, n)
    wd = DIFF_HEADS * DIFF_DV
    c = 0
    parts = []
    for width in (wd, wd, wd, GLA_HEADS * GLA_DK, GLA_HEADS * GLA_DK, GLA_HEADS * GLA_DV, GLA_HEADS * GLA_DV):
        parts.append(w_in[:, c:c + width])
        c += width
    parts[3] = _pad_heads(parts[3], GLA_HEADS, GLA_DK)
    parts[4] = _pad_heads(parts[4], GLA_HEADS, GLA_DK)
    w_main = jnp.concatenate(parts, axis=1).astype(BF16)
    w_ga = jnp.pad(w_in[:, c:c + GLA_RANK], ((0, 0), (0, LANES - GLA_RANK))).astype(BF16)
    wa2 = jnp.pad(_pad_heads(w_a2, GLA_HEADS, GLA_DK), ((0, LANES - GLA_RANK), (0, 0)))
    ba = _pad_heads(b_a, GLA_HEADS, GLA_DK).reshape(1, wd)
    row = lambda i: (i, 0)
    fixed = lambda i: (0, 0)
    out_bf = jax.ShapeDtypeStruct((n, wd), BF16)
    out_f = jax.ShapeDtypeStruct((n, wd), F32)
    return pl.pallas_call(
        _proj_even_kernel,
        grid=(n // tm,),
        in_specs=[pl.BlockSpec((tm, d), row), pl.BlockSpec((1, d), fixed),
                  pl.BlockSpec(w_main.shape, fixed), pl.BlockSpec(w_ga.shape, fixed),
                  pl.BlockSpec(wa2.shape, fixed), pl.BlockSpec((1, wd), fixed)],
        out_specs=[pl.BlockSpec((tm, wd), row)] * 8,
        out_shape=[out_bf, out_bf, out_bf, out_f, out_f, out_f, out_f, out_f],
        compiler_params=_cparams(("parallel",)),
        name="proj_even",
    )(h, g.reshape(1, d), w_main, w_ga, wa2, ba)


def _proj_odd_kernel(h_ref, g_ref, w_ref, q_ref, k_ref, v_ref):
    m = _rms(h_ref[...], g_ref[...]).astype(BF16)
    d = q_ref.shape[1]
    dh = d // SB_HEADS
    q_ref[...] = (_dot(m, w_ref[:, 0:d]) * dh ** -0.5).astype(BF16)
    k_ref[...] = _dot(m, w_ref[:, d:2 * d]).astype(BF16)
    v_ref[...] = _dot(m, w_ref[:, 2 * d:3 * d]).astype(BF16)


def _proj_odd(h, g, w_qkv):
    n, d = h.shape
    tm = min(TOKEN_TILE, n)
    row = lambda i: (i, 0)
    fixed = lambda i: (0, 0)
    out = jax.ShapeDtypeStruct((n, d), BF16)
    return pl.pallas_call(
        _proj_odd_kernel,
        grid=(n // tm,),
        in_specs=[pl.BlockSpec((tm, d), row), pl.BlockSpec((1, d), fixed), pl.BlockSpec(w_qkv.shape, fixed)],
        out_specs=[pl.BlockSpec((tm, d), row)] * 3,
        out_shape=[out, out, out],
        compiler_params=_cparams(("parallel",)),
        name="proj_odd",
    )(h, g.reshape(1, d), w_qkv.astype(BF16))


def _out_kernel(h_ref, a_ref, b_ref, w_ref, g_ref, o_ref):
    half = a_ref.shape[1]
    y = _dot(a_ref[...], w_ref[0:half, :]) + _dot(b_ref[...], w_ref[half:, :])
    o_ref[...] = h_ref[...] + _rms(y, g_ref[...])


def _out_proj(h, a, a_col, b, b_col, w_out, g):
    n, d = h.shape
    tm = min(TOKEN_TILE, n)
    half = w_out.shape[0] // 2
    row = lambda i: (i, 0)
    fixed = lambda i: (0, 0)
    return pl.pallas_call(
        _out_kernel,
        grid=(n // tm,),
        in_specs=[pl.BlockSpec((tm, d), row),
                  pl.BlockSpec((tm, half), lambda i: (i, a_col)),
                  pl.BlockSpec((tm, half), lambda i: (i, b_col)),
                  pl.BlockSpec(w_out.shape, fixed), pl.BlockSpec((1, d), fixed)],
        out_specs=pl.BlockSpec((tm, d), row),
        out_shape=jax.ShapeDtypeStruct((n, d), F32),
        compiler_params=_cparams(("parallel",)),
        name="out_proj",
    )(h, a, b, w_out.astype(BF16), g.reshape(1, d))


def _diff_kernel(lam_ref, g_ref, q_ref, k_ref, v_ref, o_ref, m_ref, l_ref, acc_ref, *, lambda_init):
    i = pl.program_id(2)
    tq = q_ref.shape[0]
    tk = tq
    q = q_ref[...]
    lane = lax.broadcasted_iota(jnp.int32, q.shape, 1)
    zero = jnp.zeros_like(q)
    q_maps = (jnp.where(lane < DIFF_DH, q, zero), jnp.where(lane >= DIFF_DH, q, zero))
    m_ref[...] = jnp.full(m_ref.shape, -jnp.inf, F32)
    l_ref[...] = jnp.zeros(l_ref.shape, F32)
    acc_ref[...] = jnp.zeros(acc_ref.shape, F32)
    causal = (lax.broadcasted_iota(jnp.int32, (tq, tk), 1) <= lax.broadcasted_iota(jnp.int32, (tq, tk), 0))

    def block(j, masked):
        rows = pl.ds(pl.multiple_of(j * tk, tk), tk)
        kb = k_ref[rows, :]
        vb = v_ref[rows, :]
        for mi in range(2):
            s = _dot_nt(q_maps[mi], kb)
            if masked:
                s = jnp.where(causal, s, -jnp.inf)
            m_prev = m_ref[mi]
            m_new = jnp.maximum(m_prev, jnp.max(s, axis=-1, keepdims=True))
            alpha = jnp.exp(m_prev - m_new)
            p = jnp.exp(s - m_new)
            l_ref[mi] = alpha * l_ref[mi] + jnp.sum(p, axis=-1, keepdims=True)
            acc_ref[mi] = alpha * acc_ref[mi] + _dot(p.astype(BF16), vb)
            m_ref[mi] = m_new

    def body(j, carry):
        block(j, False)
        return carry

    lax.fori_loop(0, i, body, 0)
    block(i, True)

    lp = lam_ref[...]
    lam = (jnp.exp(jnp.sum(lp[0:1] * lp[1:2], axis=-1, keepdims=True))
           - jnp.exp(jnp.sum(lp[2:3] * lp[3:4], axis=-1, keepdims=True)) + lambda_init)
    o = acc_ref[0] / l_ref[0] - lam * (acc_ref[1] / l_ref[1])
    o_ref[...] = (_rms(o, g_ref[...]) * (1.0 - lambda_init)).astype(o_ref.dtype)


def _diff_attention(dq, dk, dv, lam_params, subln_g, lambda_init, batch, seq):
    n, wd = dq.shape
    tq = min(ATTN_TILE, seq)
    nq = seq // tq
    qspec = pl.BlockSpec((tq, LANES), lambda b, h, i: (b * nq + i, h))
    kvspec = pl.BlockSpec((seq, LANES), lambda b, h, i: (b, h))
    fixed = lambda b, h, i: (0, 0)
    return pl.pallas_call(
        functools.partial(_diff_kernel, lambda_init=lambda_init),
        grid=(batch, DIFF_HEADS, nq),
        in_specs=[pl.BlockSpec(lam_params.shape, fixed), pl.BlockSpec((1, DIFF_DV), fixed), qspec, kvspec, kvspec],
        out_specs=qspec,
        out_shape=jax.ShapeDtypeStruct((n, wd), BF16),
        scratch_shapes=[pltpu.VMEM((2, tq, 1), F32), pltpu.VMEM((2, tq, 1), F32), pltpu.VMEM((2, tq, DIFF_DV), F32)],
        compiler_params=_cparams(("parallel", "parallel", "arbitrary")),
        name="diff_attn",
    )(lam_params, subln_g.reshape(1, DIFF_DV), dq, dk, dv)


def _gla_kernel(q_ref, k_ref, la_ref, v_ref, gg_ref, g_ref, o_ref, state_ref):
    c = GLA_CHUNK
    sub = GLA_SUB
    nsub = c // sub

    @pl.when(pl.program_id(2) == 0)
    def _():
        state_ref[...] = jnp.zeros(state_ref.shape, F32)

    r_i = lax.broadcasted_iota(jnp.int32, (c, c), 0)
    c_i = lax.broadcasted_iota(jnp.int32, (c, c), 1)
    tril = (c_i <= r_i).astype(BF16)
    off_mask = c_i < (r_i // sub) * sub
    diag_mask = c_i <= r_i
    lane3 = lax.broadcasted_iota(jnp.int32, (nsub, sub, c), 2)
    blk3 = lax.broadcasted_iota(jnp.int32, (nsub, sub, c), 0)
    g = g_ref[...]

    def chunk(ci, carry):
        rows = pl.ds(pl.multiple_of(ci * c, c), c)
        q = q_ref[rows, :]
        k = k_ref[rows, :]
        v = v_ref[rows, :].astype(BF16)
        la_hi, la_mid, la_lo = _split3(la_ref[rows, :])
        cum = _dot(tril, la_hi) + _dot(tril, la_mid) + _dot(tril, la_lo)
        last = cum[c - 1:c, :]
        state = state_ref[...]
        o = _dot_nt((q * jnp.exp(cum)).astype(BF16), state.astype(BF16))

        pieces = [jnp.zeros((sub, c), F32)]
        for blk in range(1, nsub):
            lo = blk * sub
            ref = cum[lo:lo + 1, :]
            qs = q[lo:lo + sub, :] * jnp.exp(cum[lo:lo + sub, :] - ref)
            ks = k * jnp.exp(jnp.minimum(ref - cum, 0.0))
            pieces.append(_dot_nt(qs.astype(BF16), ks.astype(BF16)))
        a_off = jnp.concatenate(pieces, axis=0)

        q3 = q.reshape(nsub, sub, LANES)
        k3 = k.reshape(nsub, sub, LANES)
        c3 = cum.reshape(nsub, sub, LANES)
        a3 = jnp.zeros((nsub, sub, c), F32)
        for j in range(sub):
            dec = jnp.exp(jnp.minimum(c3 - c3[:, j:j + 1, :], 0.0))
            col = jnp.sum(q3 * k3[:, j:j + 1, :] * dec, axis=-1, keepdims=True)
            a3 = jnp.where(lane3 == blk3 * sub + j, col, a3)
        a_diag = a3.reshape(c, c)

        attn = jnp.where(off_mask, a_off, jnp.where(diag_mask, a_diag, 0.0))
        o = o + _dot(attn.astype(BF16), v)

        k_dec = k * jnp.exp(last - cum)
        state_ref[...] = state * jnp.exp(last) + _dot_tn(v, k_dec.astype(BF16))

        gate = gg_ref[rows, :]
        o_ref[rows, :] = (_rms(o, g) * (gate * jax.nn.sigmoid(gate))).astype(o_ref.dtype)
        return carry

    lax.fori_loop(0, q_ref.shape[0] // c, chunk, 0)


def _gla(gq, gk, la, gv, gg, norm_g, batch, seq):
    n, wd = gv.shape
    t = min(GLA_TILE, seq)
    nt = seq // t
    spec = pl.BlockSpec((t, LANES), lambda b, h, i: (b * nt + i, h))
    fixed = lambda b, h, i: (0, 0)
    return pl.pallas_call(
        _gla_kernel,
        grid=(batch, GLA_HEADS, nt),
        in_specs=[spec, spec, spec, spec, spec, pl.BlockSpec((1, GLA_DV), fixed)],
        out_specs=spec,
        out_shape=jax.ShapeDtypeStruct((n, wd), BF16),
        scratch_shapes=[pltpu.VMEM((GLA_DV, LANES), F32)],
        compiler_params=_cparams(("parallel", "parallel", "arbitrary")),
        name="gla",
    )(gq, gk, la, gv, gg, norm_g.reshape(1, GLA_DV))


def _sb_kernel(q_ref, k_ref, v_ref, o_ref, acc_ref, later_ref):
    i = pl.program_id(2)
    tq = q_ref.shape[0]
    tk = tq
    q = q_ref[...]
    r_i = lax.broadcasted_iota(jnp.int32, (tk, tk), 0)
    c_i = lax.broadcasted_iota(jnp.int32, (tk, tk), 1)
    upper = (r_i > c_i).astype(BF16)
    strict = c_i < r_i

    def logs(j):
        rows = pl.ds(pl.multiple_of(j * tk, tk), tk)
        z = _dot_nt(q, k_ref[rows, :])
        relu, log_term = _softplus_parts(z)
        log_keep = -(relu + log_term)
        log_beta = jnp.minimum(z, 0.0) - log_term
        return rows, log_keep, log_beta

    rows, log_keep, log_beta = logs(i)
    log_keep = jnp.where(strict, log_keep, 0.0)
    within = _dot(log_keep.astype(BF16), upper)
    att = jnp.where(strict, jnp.exp(log_beta + within), 0.0)
    acc_ref[...] = _dot(att.astype(BF16), v_ref[rows, :])
    later_ref[...] = jnp.sum(log_keep, axis=-1, keepdims=True)

    def body(step, carry):
        rows, log_keep, log_beta = logs(i - 1 - step)
        within = _dot(log_keep.astype(BF16), upper)
        later = later_ref[...]
        att = jnp.exp(log_beta + within + later)
        acc_ref[...] += _dot(att.astype(BF16), v_ref[rows, :])
        later_ref[...] = later + jnp.sum(log_keep, axis=-1, keepdims=True)
        return carry

    lax.fori_loop(0, i, body, 0)
    o_ref[...] = acc_ref[...].astype(o_ref.dtype)


def _stick_breaking(q, k, v, batch, seq):
    n, d = q.shape
    tq = min(ATTN_TILE, seq)
    nq = seq // tq
    qspec = pl.BlockSpec((tq, LANES), lambda b, h, i: (b * nq + i, h))
    kvspec = pl.BlockSpec((seq, LANES), lambda b, h, i: (b, h))
    return pl.pallas_call(
        _sb_kernel,
        grid=(batch, SB_HEADS, nq),
        in_specs=[qspec, kvspec, kvspec],
        out_specs=qspec,
        out_shape=jax.ShapeDtypeStruct((n, d), BF16),
        scratch_shapes=[pltpu.VMEM((tq, LANES), F32), pltpu.VMEM((tq, 1), F32)],
        compiler_params=_cparams(("parallel", "parallel", "arbitrary")),
        name="stick_breaking",
    )(q, k, v)


def kernel(x, ffn_pre_g, ffn_post_g, ffn_w_gate, ffn_w_up, ffn_w_down, mix_pre_g, mix_post_g, hyb_w_in, hyb_w_out, diff_lambda, diff_subln_g, gla_w_a2, gla_b_a, gla_norm_g, sb_w_qkv, sb_w_out):
    batch, seq, d = x.shape
    depth = ffn_pre_g.shape[0]
    assert d // SB_HEADS == LANES and DIFF_DV == LANES and GLA_DV == LANES
    h = x.reshape(batch * seq, d)

    def ffn(h, layer, j):
        return _ffn(h, ffn_pre_g[layer, j], ffn_w_gate[layer, j], ffn_w_up[layer, j], ffn_w_down[layer, j],
                    ffn_post_g[layer, j])

    for layer in range(depth):
        h = ffn(h, layer, 0)
        if layer % 2 == 0:
            e = layer // 2
            lambda_init = 0.8 - 0.6 * math.exp(-0.3 * layer)
            dq, dk, dv, gq, gk, la, gv, gg = _proj_even(h, mix_pre_g[layer], hyb_w_in[e], gla_w_a2[e], gla_b_a[e])
            a_out = _diff_attention(dq, dk, dv, diff_lambda[e], diff_subln_g[e], lambda_init, batch, seq)
            b_out = _gla(gq, gk, la, gv, gg, gla_norm_g[e], batch, seq)
            h = _out_proj(h, a_out, 0, b_out, 0, hyb_w_out[e], mix_post_g[layer])
        else:
            o = layer // 2
            q, k, v = _proj_odd(h, mix_pre_g[layer], sb_w_qkv[o])
            att = _stick_breaking(q, k, v, batch, seq)
            h = _out_proj(h, att, 0, att, 1, sb_w_out[o], mix_post_g[layer])
        h = ffn(h, layer, 1)
    return h.reshape(batch, seq, d)
```

```python
import functools
import math

import jax
import jax.numpy as jnp
from jax import lax
from jax.experimental import pallas as pl
from jax.experimental.pallas import tpu as pltpu

F32 = jnp.float32
BF16 = jnp.bfloat16

EPS = 1e-6
DIFF_HEADS = 4
DIFF_DH = 64
DIFF_DV = 2 * DIFF_DH
GLA_HEADS = 4
GLA_DK = 64
GLA_DV = 128
GLA_RANK = 16
GLA_TAU = 16.0
GLA_CHUNK = 64
GLA_SUB = 16
SB_HEADS = 8

LANES = 128
VMEM_LIMIT = 56 * 1024 * 1024

TOKEN_TILE = 512
ATTN_TILE = 256
ATTN_QSUB = 4
SB_KV_UNROLL = 2
LOG2E = math.log2(math.e)
GLA_TILE = 512


def _cparams(sem):
    return pltpu.CompilerParams(dimension_semantics=sem, vmem_limit_bytes=VMEM_LIMIT)


def _rms(x, g):
    return x * lax.rsqrt(jnp.mean(x * x, axis=-1, keepdims=True) + EPS) * g


def _dot(a, b):
    return jnp.dot(a, b, preferred_element_type=F32)


def _dot_nt(a, b):
    return lax.dot_general(a, b, (((1,), (1,)), ((), ())), preferred_element_type=F32)


def _dot_tn(a, b):
    return lax.dot_general(a, b, (((0,), (0,)), ((), ())), preferred_element_type=F32)


def _split3(x):
    hi = x.astype(BF16)
    r1 = x - hi.astype(F32)
    mid = r1.astype(BF16)
    lo = (r1 - mid.astype(F32)).astype(BF16)
    return hi, mid, lo


def _softplus_parts(z):
    return jnp.maximum(z, 0.0), jnp.log(1.0 + jnp.exp(-jnp.abs(z)))


def _ffn_kernel(h_ref, gpre_ref, wg_ref, wu_ref, wd_ref, gpost_ref, o_ref):
    h = h_ref[...]
    xn = _rms(h, gpre_ref[...]).astype(BF16)
    gate = _dot(xn, wg_ref[...])
    up = _dot(xn, wu_ref[...])
    act = (gate * jax.nn.sigmoid(gate) * up).astype(BF16)
    f = _dot(act, wd_ref[...])
    o_ref[...] = h + 0.5 * _rms(f, gpost_ref[...])


def _ffn(h, g_pre, wg, wu, wd, g_post):
    n, d = h.shape
    dff = wg.shape[1]
    tm = min(TOKEN_TILE, n)
    row = lambda i: (i, 0)
    fixed = lambda i: (0, 0)
    return pl.pallas_call(
        _ffn_kernel,
        grid=(n // tm,),
        in_specs=[pl.BlockSpec((tm, d), row), pl.BlockSpec((1, d), fixed),
                  pl.BlockSpec((d, dff), fixed), pl.BlockSpec((d, dff), fixed),
                  pl.BlockSpec((dff, d), fixed), pl.BlockSpec((1, d), fixed)],
        out_specs=pl.BlockSpec((tm, d), row),
        out_shape=jax.ShapeDtypeStruct((n, d), F32),
        compiler_params=_cparams(("parallel",)),
        name="ffn",
    )(h, g_pre.reshape(1, d), wg.astype(BF16), wu.astype(BF16), wd.astype(BF16), g_post.reshape(1, d))


def _store_key_blocks(vt_ref, vt):
    tk = vt_ref.shape[2]
    for c in range(vt_ref.shape[0]):
        vt_ref[c] = vt[:, c * tk:(c + 1) * tk].astype(vt_ref.dtype)


def _proj_even_kernel(h_ref, g_ref, w_ref, wvt_ref, wga_ref, wa2_ref, ba_ref,
                      dq_ref, dk_ref, dvt_ref, gq_ref, gk_ref, la_ref, gv_ref, gg_ref):
    m = _rms(h_ref[...], g_ref[...]).astype(BF16)
    wd = dq_ref.shape[1]

    def seg(i):
        return _dot(m, w_ref[:, i * wd:(i + 1) * wd])

    dq_ref[...] = (seg(0) * (DIFF_DH ** -0.5 * LOG2E)).astype(BF16)
    dk_ref[...] = seg(1).astype(BF16)
    _store_key_blocks(dvt_ref, _dot_nt(wvt_ref[...], m))
    gq_ref[...] = seg(2) * GLA_DK ** -0.5
    gk_ref[...] = seg(3)
    gv_ref[...] = seg(4)
    gg_ref[...] = seg(5)
    ga = _dot(m, wga_ref[...])
    ga_hi, ga_lo, _ = _split3(ga)
    w_hi, w_lo, _ = _split3(wa2_ref[...])
    x = _dot(ga_hi, w_hi) + _dot(ga_lo, w_hi) + _dot(ga_hi, w_lo) + ba_ref[...]
    relu_neg, log_term = _softplus_parts(-x)
    la_ref[...] = -(relu_neg + log_term) * (1.0 / GLA_TAU)


def _pad_heads(w, heads, dh):
    lead = w.shape[:-1]
    w = w.reshape(lead + (heads, dh))
    w = jnp.pad(w, [(0, 0)] * len(lead) + [(0, 0), (0, LANES - dh)])
    return w.reshape(lead + (heads * LANES,))


def _proj_even(h, g, w_in, w_a2, b_a):
    n, d = h.shape
    tm = min(TOKEN_TILE, n)
    wd = DIFF_HEADS * DIFF_DV
    c = 0
    parts = []
    for width in (wd, wd, wd, GLA_HEADS * GLA_DK, GLA_HEADS * GLA_DK, GLA_HEADS * GLA_DV, GLA_HEADS * GLA_DV):
        parts.append(w_in[:, c:c + width])
        c += width
    parts[3] = _pad_heads(parts[3], GLA_HEADS, GLA_DK)
    parts[4] = _pad_heads(parts[4], GLA_HEADS, GLA_DK)
    w_vt = parts.pop(2).T.astype(BF16)
    w_main = jnp.concatenate(parts, axis=1).astype(BF16)
    tk = min(ATTN_TILE, tm)
    w_ga = jnp.pad(w_in[:, c:c + GLA_RANK], ((0, 0), (0, LANES - GLA_RANK))).astype(BF16)
    wa2 = jnp.pad(_pad_heads(w_a2, GLA_HEADS, GLA_DK), ((0, LANES - GLA_RANK), (0, 0)))
    ba = _pad_heads(b_a, GLA_HEADS, GLA_DK).reshape(1, wd)
    row = lambda i: (i, 0)
    fixed = lambda i: (0, 0)
    out_bf = jax.ShapeDtypeStruct((n, wd), BF16)
    out_f = jax.ShapeDtypeStruct((n, wd), F32)
    out_vt = jax.ShapeDtypeStruct((n // tk, wd, tk), BF16)
    rowspec = pl.BlockSpec((tm, wd), row)
    vtspec = pl.BlockSpec((tm // tk, wd, tk), lambda i: (i, 0, 0))
    return pl.pallas_call(
        _proj_even_kernel,
        grid=(n // tm,),
        in_specs=[pl.BlockSpec((tm, d), row), pl.BlockSpec((1, d), fixed),
                  pl.BlockSpec(w_main.shape, fixed), pl.BlockSpec(w_vt.shape, fixed), pl.BlockSpec(w_ga.shape, fixed),
                  pl.BlockSpec(wa2.shape, fixed), pl.BlockSpec((1, wd), fixed)],
        out_specs=[rowspec, rowspec, vtspec] + [rowspec] * 5,
        out_shape=[out_bf, out_bf, out_vt, out_f, out_f, out_f, out_f, out_f],
        compiler_params=_cparams(("parallel",)),
        name="proj_even",
    )(h, g.reshape(1, d), w_main, w_vt, w_ga, wa2, ba)


def _proj_odd_kernel(h_ref, g_ref, w_ref, wvt_ref, q_ref, k_ref, vt_ref):
    m = _rms(h_ref[...], g_ref[...]).astype(BF16)
    d = q_ref.shape[1]
    dh = d // SB_HEADS
    q_ref[...] = (_dot(m, w_ref[:, 0:d]) * (dh ** -0.5 * LOG2E)).astype(BF16)
    k_ref[...] = _dot(m, w_ref[:, d:2 * d]).astype(BF16)
    _store_key_blocks(vt_ref, _dot_nt(wvt_ref[...], m))


def _proj_odd(h, g, w_qkv):
    n, d = h.shape
    tm = min(TOKEN_TILE, n)
    tk = min(ATTN_TILE, tm)
    row = lambda i: (i, 0)
    fixed = lambda i: (0, 0)
    out = jax.ShapeDtypeStruct((n, d), BF16)
    w_qk = w_qkv[:, :2 * d].astype(BF16)
    w_vt = w_qkv[:, 2 * d:].T.astype(BF16)
    return pl.pallas_call(
        _proj_odd_kernel,
        grid=(n // tm,),
        in_specs=[pl.BlockSpec((tm, d), row), pl.BlockSpec((1, d), fixed),
                  pl.BlockSpec(w_qk.shape, fixed), pl.BlockSpec(w_vt.shape, fixed)],
        out_specs=[pl.BlockSpec((tm, d), row), pl.BlockSpec((tm, d), row),
                   pl.BlockSpec((tm // tk, d, tk), lambda i: (i, 0, 0))],
        out_shape=[out, out, jax.ShapeDtypeStruct((n // tk, d, tk), BF16)],
        compiler_params=_cparams(("parallel",)),
        name="proj_odd",
    )(h, g.reshape(1, d), w_qk, w_vt)


def _out_kernel(h_ref, a_ref, b_ref, w_ref, g_ref, o_ref):
    half = a_ref.shape[1]
    y = _dot(a_ref[...], w_ref[0:half, :]) + _dot(b_ref[...], w_ref[half:, :])
    o_ref[...] = h_ref[...] + _rms(y, g_ref[...])


def _out_proj(h, a, a_col, b, b_col, w_out, g):
    n, d = h.shape
    tm = min(TOKEN_TILE, n)
    half = w_out.shape[0] // 2
    row = lambda i: (i, 0)
    fixed = lambda i: (0, 0)
    return pl.pallas_call(
        _out_kernel,
        grid=(n // tm,),
        in_specs=[pl.BlockSpec((tm, d), row),
                  pl.BlockSpec((tm, half), lambda i: (i, a_col)),
                  pl.BlockSpec((tm, half), lambda i: (i, b_col)),
                  pl.BlockSpec(w_out.shape, fixed), pl.BlockSpec((1, d), fixed)],
        out_specs=pl.BlockSpec((tm, d), row),
        out_shape=jax.ShapeDtypeStruct((n, d), F32),
        compiler_params=_cparams(("parallel",)),
        name="out_proj",
    )(h, a, b, w_out.astype(BF16), g.reshape(1, d))


def _diff_kernel(lam_ref, g_ref, q_ref, k_ref, vt_ref, o_ref, m_ref, l_ref, acc_ref, *, lambda_init):
    i = pl.program_id(2)
    tk = vt_ref.shape[2]
    nsub = q_ref.shape[0] // tk
    base = i * nsub
    lane = lax.broadcasted_iota(jnp.int32, (tk, LANES), 1)
    causal = (lax.broadcasted_iota(jnp.int32, (tk, tk), 0) <= lax.broadcasted_iota(jnp.int32, (tk, tk), 1))

    def q_map(a, mi):
        q = q_ref[a * tk:(a + 1) * tk, :]
        keep = (lane < DIFF_DH) if mi == 0 else (lane >= DIFF_DH)
        return jnp.where(keep, q, jnp.zeros_like(q))

    def run_tiles(tiles):
        scores = [_dot_nt(kb, q_map(a, mi)) for a, mi, kb, _, _, _ in tiles]
        pending = []
        for (a, mi, _, vtb, first, masked), s in zip(tiles, scores):
            if masked:
                s = jnp.where(causal, s, -jnp.inf)
            m_blk = jnp.max(s, axis=0, keepdims=True)
            if first:
                alpha = None
                p = jnp.exp2(s - m_blk)
                m_ref[a, mi] = m_blk
                l_ref[a, mi] = jnp.sum(p, axis=0, keepdims=True)
            else:
                m_prev = m_ref[a, mi]
                m_new = jnp.maximum(m_prev, m_blk)
                alpha = jnp.exp2(m_prev - m_new)
                p = jnp.exp2(s - m_new)
                l_ref[a, mi] = alpha * l_ref[a, mi] + jnp.sum(p, axis=0, keepdims=True)
                m_ref[a, mi] = m_new
            pending.append((a, mi, alpha, _dot(vtb, p.astype(BF16))))
        for a, mi, alpha, pv in pending:
            acc_ref[a, mi] = pv if alpha is None else alpha * acc_ref[a, mi] + pv

    def key_block(j):
        return k_ref[pl.ds(pl.multiple_of(j * tk, tk), tk), :], vt_ref[j]

    band = []
    for c in range(nsub):
        kb, vtb = key_block(base + c)
        band += [(a, mi, kb, vtb, c == 0, a == c) for a in range(c, nsub) for mi in range(2)]
    run_tiles(band)

    def body(j, carry):
        kb, vtb = key_block(j)
        run_tiles([(a, mi, kb, vtb, False, False) for a in range(nsub) for mi in range(2)])
        return carry

    lax.fori_loop(0, base, body, 0)

    lp = lam_ref[...]
    lam = (jnp.exp(jnp.sum(lp[0:1] * lp[1:2], axis=-1, keepdims=True))
           - jnp.exp(jnp.sum(lp[2:3] * lp[3:4], axis=-1, keepdims=True)) + lambda_init)
    for a in range(nsub):
        o = acc_ref[a, 0] / l_ref[a, 0] - lam * (acc_ref[a, 1] / l_ref[a, 1])
        o = o * lax.rsqrt(jnp.mean(o * o, axis=0, keepdims=True) + EPS) * (1.0 - lambda_init)
        o_ref[a * tk:(a + 1) * tk, :] = (o.T * g_ref[...]).astype(o_ref.dtype)


def _diff_attention(dq, dk, dvt, lam_params, subln_g, lambda_init, batch, seq):
    n, wd = dq.shape
    tk = dvt.shape[2]
    tq = min(ATTN_QSUB * tk, seq)
    nq = seq // tq
    nsub = tq // tk
    qspec = pl.BlockSpec((tq, LANES), lambda b, h, i: (b * nq + i, h))
    kspec = pl.BlockSpec((seq, LANES), lambda b, h, i: (b, h))
    vtspec = pl.BlockSpec((seq // tk, LANES, tk), lambda b, h, i: (b, h, 0))
    fixed = lambda b, h, i: (0, 0)
    return pl.pallas_call(
        functools.partial(_diff_kernel, lambda_init=lambda_init),
        grid=(batch, DIFF_HEADS, nq),
        in_specs=[pl.BlockSpec(lam_params.shape, fixed), pl.BlockSpec((1, DIFF_DV), fixed), qspec, kspec, vtspec],
        out_specs=qspec,
        out_shape=jax.ShapeDtypeStruct((n, wd), BF16),
        scratch_shapes=[pltpu.VMEM((nsub, 2, 1, tk), F32), pltpu.VMEM((nsub, 2, 1, tk), F32),
                        pltpu.VMEM((nsub, 2, DIFF_DV, tk), F32)],
        compiler_params=_cparams(("parallel", "parallel", "arbitrary")),
        name="diff_attn",
    )(lam_params, subln_g.reshape(1, DIFF_DV), dq, dk, dvt)


def _gla_kernel(q_ref, k_ref, la_ref, v_ref, gg_ref, g_ref, o_ref, state_ref):
    c = GLA_CHUNK
    sub = GLA_SUB
    nsub = c // sub

    @pl.when(pl.program_id(2) == 0)
    def _():
        state_ref[...] = jnp.zeros(state_ref.shape, F32)

    r_i = lax.broadcasted_iota(jnp.int32, (c, c), 0)
    c_i = lax.broadcasted_iota(jnp.int32, (c, c), 1)
    tril = (c_i <= r_i).astype(BF16)
    off_mask = c_i < (r_i // sub) * sub
    diag_mask = c_i <= r_i
    lane3 = lax.broadcasted_iota(jnp.int32, (nsub, sub, c), 2)
    blk3 = lax.broadcasted_iota(jnp.int32, (nsub, sub, c), 0)
    g = g_ref[...]

    def chunk(ci, carry):
        rows = pl.ds(pl.multiple_of(ci * c, c), c)
        q = q_ref[rows, :]
        k = k_ref[rows, :]
        v = v_ref[rows, :].astype(BF16)
        la_hi, la_mid, la_lo = _split3(la_ref[rows, :])
        cum = _dot(tril, la_hi) + _dot(tril, la_mid) + _dot(tril, la_lo)
        last = cum[c - 1:c, :]
        state = state_ref[...]
        o = _dot_nt((q * jnp.exp(cum)).astype(BF16), state.astype(BF16))

        pieces = [jnp.zeros((sub, c), F32)]
        for blk in range(1, nsub):
            lo = blk * sub
            ref = cum[lo:lo + 1, :]
            qs = q[lo:lo + sub, :] * jnp.exp(cum[lo:lo + sub, :] - ref)
            ks = k * jnp.exp(jnp.minimum(ref - cum, 0.0))
            pieces.append(_dot_nt(qs.astype(BF16), ks.astype(BF16)))
        a_off = jnp.concatenate(pieces, axis=0)

        q3 = q.reshape(nsub, sub, LANES)
        k3 = k.reshape(nsub, sub, LANES)
        c3 = cum.reshape(nsub, sub, LANES)
        a3 = jnp.zeros((nsub, sub, c), F32)
        for j in range(sub):
            dec = jnp.exp(jnp.minimum(c3 - c3[:, j:j + 1, :], 0.0))
            col = jnp.sum(q3 * k3[:, j:j + 1, :] * dec, axis=-1, keepdims=True)
            a3 = jnp.where(lane3 == blk3 * sub + j, col, a3)
        a_diag = a3.reshape(c, c)

        attn = jnp.where(off_mask, a_off, jnp.where(diag_mask, a_diag, 0.0))
        o = o + _dot(attn.astype(BF16), v)

        k_dec = k * jnp.exp(last - cum)
        state_ref[...] = state * jnp.exp(last) + _dot_tn(v, k_dec.astype(BF16))

        gate = gg_ref[rows, :]
        o_ref[rows, :] = (_rms(o, g) * (gate * jax.nn.sigmoid(gate))).astype(o_ref.dtype)
        return carry

    lax.fori_loop(0, q_ref.shape[0] // c, chunk, 0)


def _gla(gq, gk, la, gv, gg, norm_g, batch, seq):
    n, wd = gv.shape
    t = min(GLA_TILE, seq)
    nt = seq // t
    spec = pl.BlockSpec((t, LANES), lambda b, h, i: (b * nt + i, h))
    fixed = lambda b, h, i: (0, 0)
    return pl.pallas_call(
        _gla_kernel,
        grid=(batch, GLA_HEADS, nt),
        in_specs=[spec, spec, spec, spec, spec, pl.BlockSpec((1, GLA_DV), fixed)],
        out_specs=spec,
        out_shape=jax.ShapeDtypeStruct((n, wd), BF16),
        scratch_shapes=[pltpu.VMEM((GLA_DV, LANES), F32)],
        compiler_params=_cparams(("parallel", "parallel", "arbitrary")),
        name="gla",
    )(gq, gk, la, gv, gg, norm_g.reshape(1, GLA_DV))


SUM_ROWS = 8


def _sb_kernel(q_ref, k_ref, vt_ref, o_ref, acc_ref):
    i = pl.program_id(2)
    tk = vt_ref.shape[2]
    nsub = q_ref.shape[0] // tk
    base = i * nsub
    r_s = lax.broadcasted_iota(jnp.int32, (tk + SUM_ROWS, tk), 0)
    c_s = lax.broadcasted_iota(jnp.int32, (tk + SUM_ROWS, tk), 1)
    suffix_mat = ((c_s > r_s) | (r_s >= tk)).astype(BF16)
    strict = (lax.broadcasted_iota(jnp.int32, (tk, tk), 0) < lax.broadcasted_iota(jnp.int32, (tk, tk), 1))

    def run_tiles(tiles, later):
        later = list(later)
        logits = [_dot_nt(kb, q_ref[a * tk:(a + 1) * tk, :]) for a, kb, _, _ in tiles]
        mids = []
        for (a, _, _, masked), z in zip(tiles, logits):
            log_term = jnp.log2(1.0 + jnp.exp2(-jnp.abs(z)))
            log_beta = jnp.minimum(z, 0.0) - log_term
            log_keep = log_beta - z
            if masked:
                log_keep = jnp.where(strict, log_keep, 0.0)
            mids.append((log_beta, _dot(suffix_mat, log_keep.astype(BF16))))
        pending = []
        for (a, _, vtb, masked), (log_beta, sums) in zip(tiles, mids):
            arg = log_beta + sums[:tk]
            if later[a] is not None:
                arg = arg + later[a]
            att = jnp.exp2(arg)
            if masked:
                att = jnp.where(strict, att, 0.0)
            pending.append((a, later[a] is None, _dot(vtb, att.astype(BF16))))
            later[a] = sums[tk:tk + 1] if later[a] is None else later[a] + sums[tk:tk + 1]
        for a, first, contrib in pending:
            if first:
                acc_ref[a] = contrib
            else:
                acc_ref[a] += contrib
        return tuple(later)

    def key_block(j):
        return k_ref[pl.ds(pl.multiple_of(j * tk, tk), tk), :], vt_ref[j]

    band = []
    for c in range(nsub - 1, -1, -1):
        kb, vtb = key_block(base + c)
        band += [(a, kb, vtb, a == c) for a in range(c, nsub)]
    later = run_tiles(band, [None] * nsub)

    kv_unroll = SB_KV_UNROLL if nsub % SB_KV_UNROLL == 0 else 1

    def body(step, later):
        tiles = []
        for u in range(kv_unroll):
            kb, vtb = key_block(base - 1 - (step * kv_unroll + u))
            tiles += [(a, kb, vtb, False) for a in range(nsub)]
        return run_tiles(tiles, later)

    lax.fori_loop(0, base // kv_unroll, body, later)
    for a in range(nsub):
        o_ref[a * tk:(a + 1) * tk, :] = acc_ref[a].T.astype(o_ref.dtype)


def _stick_breaking(q, k, vt, batch, seq):
    n, d = q.shape
    tk = vt.shape[2]
    tq = min(ATTN_QSUB * tk, seq)
    nq = seq // tq
    qspec = pl.BlockSpec((tq, LANES), lambda b, h, i: (b * nq + i, h))
    kspec = pl.BlockSpec((seq, LANES), lambda b, h, i: (b, h))
    vtspec = pl.BlockSpec((seq // tk, LANES, tk), lambda b, h, i: (b, h, 0))
    return pl.pallas_call(
        _sb_kernel,
        grid=(batch, SB_HEADS, nq),
        in_specs=[qspec, kspec, vtspec],
        out_specs=qspec,
        out_shape=jax.ShapeDtypeStruct((n, d), BF16),
        scratch_shapes=[pltpu.VMEM((tq // tk, LANES, tk), F32)],
        compiler_params=_cparams(("parallel", "parallel", "arbitrary")),
        name="stick_breaking",
    )(q, k, vt)


def kernel(x, ffn_pre_g, ffn_post_g, ffn_w_gate, ffn_w_up, ffn_w_down, mix_pre_g, mix_post_g, hyb_w_in, hyb_w_out, diff_lambda, diff_subln_g, gla_w_a2, gla_b_a, gla_norm_g, sb_w_qkv, sb_w_out):
    batch, seq, d = x.shape
    depth = ffn_pre_g.shape[0]
    assert d // SB_HEADS == LANES and DIFF_DV == LANES and GLA_DV == LANES
    h = x.reshape(batch * seq, d)

    def ffn(h, layer, j):
        return _ffn(h, ffn_pre_g[layer, j], ffn_w_gate[layer, j], ffn_w_up[layer, j], ffn_w_down[layer, j],
                    ffn_post_g[layer, j])

    for layer in range(depth):
        h = ffn(h, layer, 0)
        if layer % 2 == 0:
            e = layer // 2
            lambda_init = 0.8 - 0.6 * math.exp(-0.3 * layer)
            dq, dk, dv, gq, gk, la, gv, gg = _proj_even(h, mix_pre_g[layer], hyb_w_in[e], gla_w_a2[e], gla_b_a[e])
            a_out = _diff_attention(dq, dk, dv, diff_lambda[e], diff_subln_g[e], lambda_init, batch, seq)
            b_out = _gla(gq, gk, la, gv, gg, gla_norm_g[e], batch, seq)
            h = _out_proj(h, a_out, 0, b_out, 0, hyb_w_out[e], mix_post_g[layer])
        else:
            o = layer // 2
            q, k, v = _proj_odd(h, mix_pre_g[layer], sb_w_qkv[o])
            att = _stick_breaking(q, k, v, batch, seq)
            h = _out_proj(h, att, 0, att, 1, sb_w_out[o], mix_post_g[layer])
        h = ffn(h, layer, 1)
    return h.reshape(batch, seq, d)
```

```python
import functools
import math

import jax
import jax.numpy as jnp
from jax import lax
from jax.experimental import pallas as pl
from jax.experimental.pallas import tpu as pltpu

F32 = jnp.float32
BF16 = jnp.bfloat16

EPS = 1e-6
DIFF_HEADS = 4
DIFF_DH = 64
DIFF_DV = 2 * DIFF_DH
GLA_HEADS = 4
GLA_DK = 64
GLA_DV = 128
GLA_RANK = 16
GLA_TAU = 16.0
GLA_CHUNK = 64
GLA_SUB = 16
SB_HEADS = 8

LANES = 128
VMEM_LIMIT = 56 * 1024 * 1024

TOKEN_TILE = 512
ATTN_TILE = 256
ATTN_QSUB = 4
DIFF_KV_UNROLL = 2
ONES_ROWS = 16
LOG2_FLUSH = -150.0
LOG2E = math.log2(math.e)
GLA_TILE = 512


def _cparams(sem):
    return pltpu.CompilerParams(dimension_semantics=sem, vmem_limit_bytes=VMEM_LIMIT)


def _rms(x, g):
    return x * lax.rsqrt(jnp.mean(x * x, axis=-1, keepdims=True) + EPS) * g


def _dot(a, b):
    return jnp.dot(a, b, preferred_element_type=F32)


def _dot_nt(a, b):
    return lax.dot_general(a, b, (((1,), (1,)), ((), ())), preferred_element_type=F32)


def _dot_tn(a, b):
    return lax.dot_general(a, b, (((0,), (0,)), ((), ())), preferred_element_type=F32)


def _split3(x):
    hi = x.astype(BF16)
    r1 = x - hi.astype(F32)
    mid = r1.astype(BF16)
    lo = (r1 - mid.astype(F32)).astype(BF16)
    return hi, mid, lo


def _softplus_parts(z):
    return jnp.maximum(z, 0.0), jnp.log(1.0 + jnp.exp(-jnp.abs(z)))


def _ffn_kernel(h_ref, gpre_ref, wg_ref, wu_ref, wd_ref, gpost_ref, o_ref):
    h = h_ref[...]
    xn = _rms(h, gpre_ref[...]).astype(BF16)
    gate = _dot(xn, wg_ref[...])
    up = _dot(xn, wu_ref[...])
    act = (gate * jax.nn.sigmoid(gate) * up).astype(BF16)
    f = _dot(act, wd_ref[...])
    o_ref[...] = h + 0.5 * _rms(f, gpost_ref[...])


def _ffn(h, g_pre, wg, wu, wd, g_post):
    n, d = h.shape
    dff = wg.shape[1]
    tm = min(TOKEN_TILE, n)
    row = lambda i: (i, 0)
    fixed = lambda i: (0, 0)
    return pl.pallas_call(
        _ffn_kernel,
        grid=(n // tm,),
        in_specs=[pl.BlockSpec((tm, d), row), pl.BlockSpec((1, d), fixed),
                  pl.BlockSpec((d, dff), fixed), pl.BlockSpec((d, dff), fixed),
                  pl.BlockSpec((dff, d), fixed), pl.BlockSpec((1, d), fixed)],
        out_specs=pl.BlockSpec((tm, d), row),
        out_shape=jax.ShapeDtypeStruct((n, d), F32),
        compiler_params=_cparams(("parallel",)),
        name="ffn",
    )(h, g_pre.reshape(1, d), wg.astype(BF16), wu.astype(BF16), wd.astype(BF16), g_post.reshape(1, d))


def _store_key_blocks(vt_ref, vt):
    tk = vt_ref.shape[2]
    for c in range(vt_ref.shape[0]):
        vt_ref[c] = vt[:, c * tk:(c + 1) * tk].astype(vt_ref.dtype)


def _proj_even_kernel(h_ref, g_ref, w_ref, wvt_ref, ones_ref, wga_ref, wa2_ref, ba_ref,
                      dq_ref, dk_ref, dvt_ref, gq_ref, gk_ref, la_ref, gv_ref, gg_ref):
    m = _rms(h_ref[...], g_ref[...]).astype(BF16)
    wd = dq_ref.shape[1]

    def seg(i):
        return _dot(m, w_ref[:, i * wd:(i + 1) * wd])

    dq_ref[...] = (seg(0) * (DIFF_DH ** -0.5 * LOG2E)).astype(BF16)
    dk_ref[...] = seg(1).astype(BF16)
    _store_key_blocks(dvt_ref, _dot_nt(wvt_ref[...], m) + ones_ref[...])
    gq_ref[...] = seg(2) * GLA_DK ** -0.5
    gk_ref[...] = seg(3)
    gv_ref[...] = seg(4)
    gg_ref[...] = seg(5)
    ga = _dot(m, wga_ref[...])
    ga_hi, ga_lo, _ = _split3(ga)
    w_hi, w_lo, _ = _split3(wa2_ref[...])
    x = _dot(ga_hi, w_hi) + _dot(ga_lo, w_hi) + _dot(ga_hi, w_lo) + ba_ref[...]
    relu_neg, log_term = _softplus_parts(-x)
    la_ref[...] = -(relu_neg + log_term) * (1.0 / GLA_TAU)


def _pad_heads(w, heads, dh):
    lead = w.shape[:-1]
    w = w.reshape(lead + (heads, dh))
    w = jnp.pad(w, [(0, 0)] * len(lead) + [(0, 0), (0, LANES - dh)])
    return w.reshape(lead + (heads * LANES,))


def _proj_even(h, g, w_in, w_a2, b_a):
    n, d = h.shape
    tm = min(TOKEN_TILE, n)
    wd = DIFF_HEADS * DIFF_DV
    c = 0
    parts = []
    for width in (wd, wd, wd, GLA_HEADS * GLA_DK, GLA_HEADS * GLA_DK, GLA_HEADS * GLA_DV, GLA_HEADS * GLA_DV):
        parts.append(w_in[:, c:c + width])
        c += width
    parts[3] = _pad_heads(parts[3], GLA_HEADS, GLA_DK)
    parts[4] = _pad_heads(parts[4], GLA_HEADS, GLA_DK)
    w_vt = parts.pop(2).T.reshape(DIFF_HEADS, DIFF_DV, d)
    w_vt = jnp.pad(w_vt, ((0, 0), (0, ONES_ROWS), (0, 0))).reshape(-1, d).astype(BF16)
    ones_col = jnp.pad(jnp.zeros((DIFF_HEADS, DIFF_DV, 1), F32), ((0, 0), (0, ONES_ROWS), (0, 0)),
                       constant_values=1.0).reshape(-1, 1)
    vt_rows = w_vt.shape[0]
    w_main = jnp.concatenate(parts, axis=1).astype(BF16)
    tk = min(ATTN_TILE, tm)
    w_ga = jnp.pad(w_in[:, c:c + GLA_RANK], ((0, 0), (0, LANES - GLA_RANK))).astype(BF16)
    wa2 = jnp.pad(_pad_heads(w_a2, GLA_HEADS, GLA_DK), ((0, LANES - GLA_RANK), (0, 0)))
    ba = _pad_heads(b_a, GLA_HEADS, GLA_DK).reshape(1, wd)
    row = lambda i: (i, 0)
    fixed = lambda i: (0, 0)
    out_bf = jax.ShapeDtypeStruct((n, wd), BF16)
    out_f = jax.ShapeDtypeStruct((n, wd), F32)
    out_vt = jax.ShapeDtypeStruct((n // tk, vt_rows, tk), BF16)
    rowspec = pl.BlockSpec((tm, wd), row)
    vtspec = pl.BlockSpec((tm // tk, vt_rows, tk), lambda i: (i, 0, 0))
    return pl.pallas_call(
        _proj_even_kernel,
        grid=(n // tm,),
        in_specs=[pl.BlockSpec((tm, d), row), pl.BlockSpec((1, d), fixed),
                  pl.BlockSpec(w_main.shape, fixed), pl.BlockSpec(w_vt.shape, fixed),
                  pl.BlockSpec(ones_col.shape, fixed), pl.BlockSpec(w_ga.shape, fixed),
                  pl.BlockSpec(wa2.shape, fixed), pl.BlockSpec((1, wd), fixed)],
        out_specs=[rowspec, rowspec, vtspec] + [rowspec] * 5,
        out_shape=[out_bf, out_bf, out_vt, out_f, out_f, out_f, out_f, out_f],
        compiler_params=_cparams(("parallel",)),
        name="proj_even",
    )(h, g.reshape(1, d), w_main, w_vt, ones_col, w_ga, wa2, ba)


def _proj_odd_kernel(h_ref, g_ref, w_ref, wvt_ref, q_ref, k_ref, vt_ref):
    m = _rms(h_ref[...], g_ref[...]).astype(BF16)
    d = q_ref.shape[1]
    dh = d // SB_HEADS
    q_ref[...] = (_dot(m, w_ref[:, 0:d]) * (dh ** -0.5 * LOG2E)).astype(BF16)
    k_ref[...] = _dot(m, w_ref[:, d:2 * d]).astype(BF16)
    _store_key_blocks(vt_ref, _dot_nt(wvt_ref[...], m))


def _proj_odd(h, g, w_qkv):
    n, d = h.shape
    tm = min(TOKEN_TILE, n)
    tk = min(ATTN_TILE, tm)
    row = lambda i: (i, 0)
    fixed = lambda i: (0, 0)
    out = jax.ShapeDtypeStruct((n, d), BF16)
    w_qk = w_qkv[:, :2 * d].astype(BF16)
    w_vt = w_qkv[:, 2 * d:].T.astype(BF16)
    return pl.pallas_call(
        _proj_odd_kernel,
        grid=(n // tm,),
        in_specs=[pl.BlockSpec((tm, d), row), pl.BlockSpec((1, d), fixed),
                  pl.BlockSpec(w_qk.shape, fixed), pl.BlockSpec(w_vt.shape, fixed)],
        out_specs=[pl.BlockSpec((tm, d), row), pl.BlockSpec((tm, d), row),
                   pl.BlockSpec((tm // tk, d, tk), lambda i: (i, 0, 0))],
        out_shape=[out, out, jax.ShapeDtypeStruct((n // tk, d, tk), BF16)],
        compiler_params=_cparams(("parallel",)),
        name="proj_odd",
    )(h, g.reshape(1, d), w_qk, w_vt)


def _out_kernel(h_ref, a_ref, b_ref, w_ref, g_ref, o_ref):
    half = a_ref.shape[1]
    y = _dot(a_ref[...], w_ref[0:half, :]) + _dot(b_ref[...], w_ref[half:, :])
    o_ref[...] = h_ref[...] + _rms(y, g_ref[...])


def _out_proj(h, a, a_col, b, b_col, w_out, g):
    n, d = h.shape
    tm = min(TOKEN_TILE, n)
    half = w_out.shape[0] // 2
    row = lambda i: (i, 0)
    fixed = lambda i: (0, 0)
    return pl.pallas_call(
        _out_kernel,
        grid=(n // tm,),
        in_specs=[pl.BlockSpec((tm, d), row),
                  pl.BlockSpec((tm, half), lambda i: (i, a_col)),
                  pl.BlockSpec((tm, half), lambda i: (i, b_col)),
                  pl.BlockSpec(w_out.shape, fixed), pl.BlockSpec((1, d), fixed)],
        out_specs=pl.BlockSpec((tm, d), row),
        out_shape=jax.ShapeDtypeStruct((n, d), F32),
        compiler_params=_cparams(("parallel",)),
        name="out_proj",
    )(h, a, b, w_out.astype(BF16), g.reshape(1, d))


def _diff_kernel(lam_ref, g_ref, q_ref, k_ref, vt_ref, o_ref, m_ref, l_ref, acc_ref, *, lambda_init):
    i = pl.program_id(2)
    tk = vt_ref.shape[2]
    dv = DIFF_DV
    nsub = q_ref.shape[0] // tk
    base = i * nsub
    lane = lax.broadcasted_iota(jnp.int32, (tk, LANES), 1)
    causal = (lax.broadcasted_iota(jnp.int32, (tk, tk), 0) <= lax.broadcasted_iota(jnp.int32, (tk, tk), 1))

    def q_map(a, mi):
        q = q_ref[a * tk:(a + 1) * tk, :]
        keep = (lane < DIFF_DH) if mi == 0 else (lane >= DIFF_DH)
        return jnp.where(keep, q, jnp.zeros_like(q))

    def run_tiles(tiles):
        scores = [_dot_nt(kb, q_map(a, mi)) for a, mi, kb, _, _, _ in tiles]
        pending = []
        for (a, mi, _, vtb, first, masked), s in zip(tiles, scores):
            if masked:
                s = jnp.where(causal, s, -jnp.inf)
            m_blk = jnp.max(s, axis=0, keepdims=True)
            if first:
                alpha = None
                m_new = m_blk
            else:
                m_prev = m_ref[a, mi]
                m_new = jnp.maximum(m_prev, m_blk)
                alpha = jnp.exp2(m_prev - m_new)
            m_ref[a, mi] = m_new
            p = jnp.exp2(s - m_new).astype(BF16)
            pending.append((a, mi, alpha, _dot(vtb, p)))
        for a, mi, alpha, pv in pending:
            if alpha is None:
                acc_ref[a, mi] = pv[:dv]
                l_ref[a, mi] = pv[dv:dv + 1]
            else:
                acc_ref[a, mi] = alpha * acc_ref[a, mi] + pv[:dv]
                l_ref[a, mi] = alpha * l_ref[a, mi] + pv[dv:dv + 1]

    def key_block(j):
        return k_ref[pl.ds(pl.multiple_of(j * tk, tk), tk), :], vt_ref[j]

    band = []
    for c in range(nsub):
        kb, vtb = key_block(base + c)
        band += [(a, mi, kb, vtb, c == 0, a == c) for a in range(c, nsub) for mi in range(2)]
    run_tiles(band)

    kv_unroll = DIFF_KV_UNROLL if nsub % DIFF_KV_UNROLL == 0 else 1

    def body(step, carry):
        tiles = []
        for u in range(kv_unroll):
            kb, vtb = key_block(step * kv_unroll + u)
            tiles += [(a, mi, kb, vtb, False, False) for a in range(nsub) for mi in range(2)]
        run_tiles(tiles)
        return carry

    lax.fori_loop(0, base // kv_unroll, body, 0)

    lp = lam_ref[...]
    lam = (jnp.exp(jnp.sum(lp[0:1] * lp[1:2], axis=-1, keepdims=True))
           - jnp.exp(jnp.sum(lp[2:3] * lp[3:4], axis=-1, keepdims=True)) + lambda_init)
    for a in range(nsub):
        o = acc_ref[a, 0] / l_ref[a, 0] - lam * (acc_ref[a, 1] / l_ref[a, 1])
        o = o * lax.rsqrt(jnp.mean(o * o, axis=0, keepdims=True) + EPS) * (1.0 - lambda_init)
        o_ref[a * tk:(a + 1) * tk, :] = (o.T * g_ref[...]).astype(o_ref.dtype)


def _diff_attention(dq, dk, dvt, lam_params, subln_g, lambda_init, batch, seq):
    n, wd = dq.shape
    tk = dvt.shape[2]
    tq = min(ATTN_QSUB * tk, seq)
    nq = seq // tq
    nsub = tq // tk
    qspec = pl.BlockSpec((tq, LANES), lambda b, h, i: (b * nq + i, h))
    kspec = pl.BlockSpec((seq, LANES), lambda b, h, i: (b, h))
    vtspec = pl.BlockSpec((seq // tk, DIFF_DV + ONES_ROWS, tk), lambda b, h, i: (b, h, 0))
    fixed = lambda b, h, i: (0, 0)
    return pl.pallas_call(
        functools.partial(_diff_kernel, lambda_init=lambda_init),
        grid=(batch, DIFF_HEADS, nq),
        in_specs=[pl.BlockSpec(lam_params.shape, fixed), pl.BlockSpec((1, DIFF_DV), fixed), qspec, kspec, vtspec],
        out_specs=qspec,
        out_shape=jax.ShapeDtypeStruct((n, wd), BF16),
        scratch_shapes=[pltpu.VMEM((nsub, 2, 1, tk), F32), pltpu.VMEM((nsub, 2, 1, tk), F32),
                        pltpu.VMEM((nsub, 2, DIFF_DV, tk), F32)],
        compiler_params=_cparams(("parallel", "parallel", "arbitrary")),
        name="diff_attn",
    )(lam_params, subln_g.reshape(1, DIFF_DV), dq, dk, dvt)


def _gla_kernel(q_ref, k_ref, la_ref, v_ref, gg_ref, g_ref, o_ref, state_ref):
    c = GLA_CHUNK
    sub = GLA_SUB
    nsub = c // sub

    @pl.when(pl.program_id(2) == 0)
    def _():
        state_ref[...] = jnp.zeros(state_ref.shape, F32)

    r_i = lax.broadcasted_iota(jnp.int32, (c, c), 0)
    c_i = lax.broadcasted_iota(jnp.int32, (c, c), 1)
    tril = (c_i <= r_i).astype(BF16)
    off_mask = c_i < (r_i // sub) * sub
    diag_mask = c_i <= r_i
    lane3 = lax.broadcasted_iota(jnp.int32, (nsub, sub, c), 2)
    blk3 = lax.broadcasted_iota(jnp.int32, (nsub, sub, c), 0)
    g = g_ref[...]

    def chunk(ci, carry):
        rows = pl.ds(pl.multiple_of(ci * c, c), c)
        q = q_ref[rows, :]
        k = k_ref[rows, :]
        v = v_ref[rows, :].astype(BF16)
        la_hi, la_mid, la_lo = _split3(la_ref[rows, :])
        cum = _dot(tril, la_hi) + _dot(tril, la_mid) + _dot(tril, la_lo)
        last = cum[c - 1:c, :]
        state = state_ref[...]
        o = _dot_nt((q * jnp.exp(cum)).astype(BF16), state.astype(BF16))

        pieces = [jnp.zeros((sub, c), F32)]
        for blk in range(1, nsub):
            lo = blk * sub
            ref = cum[lo:lo + 1, :]
            qs = q[lo:lo + sub, :] * jnp.exp(cum[lo:lo + sub, :] - ref)
            ks = k * jnp.exp(jnp.minimum(ref - cum, 0.0))
            pieces.append(_dot_nt(qs.astype(BF16), ks.astype(BF16)))
        a_off = jnp.concatenate(pieces, axis=0)

        q3 = q.reshape(nsub, sub, LANES)
        k3 = k.reshape(nsub, sub, LANES)
        c3 = cum.reshape(nsub, sub, LANES)
        a3 = jnp.zeros((nsub, sub, c), F32)
        for j in range(sub):
            dec = jnp.exp(jnp.minimum(c3 - c3[:, j:j + 1, :], 0.0))
            col = jnp.sum(q3 * k3[:, j:j + 1, :] * dec, axis=-1, keepdims=True)
            a3 = jnp.where(lane3 == blk3 * sub + j, col, a3)
        a_diag = a3.reshape(c, c)

        attn = jnp.where(off_mask, a_off, jnp.where(diag_mask, a_diag, 0.0))
        o = o + _dot(attn.astype(BF16), v)

        k_dec = k * jnp.exp(last - cum)
        state_ref[...] = state * jnp.exp(last) + _dot_tn(v, k_dec.astype(BF16))

        gate = gg_ref[rows, :]
        o_ref[rows, :] = (_rms(o, g) * (gate * jax.nn.sigmoid(gate))).astype(o_ref.dtype)
        return carry

    lax.fori_loop(0, q_ref.shape[0] // c, chunk, 0)


def _gla(gq, gk, la, gv, gg, norm_g, batch, seq):
    n, wd = gv.shape
    t = min(GLA_TILE, seq)
    nt = seq // t
    spec = pl.BlockSpec((t, LANES), lambda b, h, i: (b * nt + i, h))
    fixed = lambda b, h, i: (0, 0)
    return pl.pallas_call(
        _gla_kernel,
        grid=(batch, GLA_HEADS, nt),
        in_specs=[spec, spec, spec, spec, spec, pl.BlockSpec((1, GLA_DV), fixed)],
        out_specs=spec,
        out_shape=jax.ShapeDtypeStruct((n, wd), BF16),
        scratch_shapes=[pltpu.VMEM((GLA_DV, LANES), F32)],
        compiler_params=_cparams(("parallel", "parallel", "arbitrary")),
        name="gla",
    )(gq, gk, la, gv, gg, norm_g.reshape(1, GLA_DV))


SUM_ROWS = 8


def _sb_kernel(q_ref, k_ref, vt_ref, o_ref, acc_ref):
    i = pl.program_id(2)
    tk = vt_ref.shape[2]
    nsub = q_ref.shape[0] // tk
    base = i * nsub
    r_s = lax.broadcasted_iota(jnp.int32, (tk + SUM_ROWS, tk), 0)
    c_s = lax.broadcasted_iota(jnp.int32, (tk + SUM_ROWS, tk), 1)
    suffix_mat = ((c_s > r_s) | (r_s >= tk)).astype(BF16)
    strict = (lax.broadcasted_iota(jnp.int32, (tk, tk), 0) < lax.broadcasted_iota(jnp.int32, (tk, tk), 1))

    def run_tiles(tiles, later):
        later = list(later)
        logits = [_dot_nt(kb, q_ref[a * tk:(a + 1) * tk, :]) for a, kb, _, _ in tiles]
        mids = []
        for (a, _, _, masked), z in zip(tiles, logits):
            log_term = jnp.log2(1.0 + jnp.exp2(-jnp.abs(z)))
            log_beta = jnp.minimum(z, 0.0) - log_term
            log_keep = log_beta - z
            if masked:
                log_keep = jnp.where(strict, log_keep, 0.0)
            mids.append((log_beta, _dot(suffix_mat, log_keep.astype(BF16))))
        pending = []
        for (a, _, vtb, masked), (log_beta, sums) in zip(tiles, mids):
            arg = log_beta + sums[:tk]
            if later[a] is not None:
                arg = arg + later[a]
            att = jnp.exp2(arg)
            if masked:
                att = jnp.where(strict, att, 0.0)
            pending.append((a, later[a] is None, _dot(vtb, att.astype(BF16))))
            later[a] = sums[tk:tk + 1] if later[a] is None else later[a] + sums[tk:tk + 1]
        for a, first, contrib in pending:
            if first:
                acc_ref[a] = contrib
            else:
                acc_ref[a] += contrib
        return tuple(later)

    def key_block(j):
        return k_ref[pl.ds(pl.multiple_of(j * tk, tk), tk), :], vt_ref[j]

    def retire(later_a, still_valid):
        return jnp.where(still_valid, later_a, -jnp.inf)

    def lockstep_tiles(offset, later):
        tiles, kept = [], []
        for a in range(nsub):
            j = base + a - offset
            kb, vtb = key_block(jnp.maximum(j, 0))
            tiles.append((a, kb, vtb, False))
            kept.append(retire(later[a], j >= 0))
        return tiles, kept

    diag = []
    for a in range(nsub):
        kb, vtb = key_block(base + a)
        diag.append((a, kb, vtb, True))
    later = run_tiles(diag, [None] * nsub)
    later = run_tiles(*lockstep_tiles(1, later))

    def alive(later):
        return (jnp.max(functools.reduce(jnp.maximum, later)) > LOG2_FLUSH).astype(jnp.int32)

    def cond(carry):
        step, live, _ = carry
        return (step < base + nsub - 2) & (live > 0)

    def body(carry):
        step, _, later = carry
        later = run_tiles(*lockstep_tiles(step + 2, later))
        return step + 1, alive(later), later

    lax.while_loop(cond, body, (jnp.int32(0), alive(later), later))
    for a in range(nsub):
        o_ref[a * tk:(a + 1) * tk, :] = acc_ref[a].T.astype(o_ref.dtype)


def _stick_breaking(q, k, vt, batch, seq):
    n, d = q.shape
    tk = vt.shape[2]
    tq = min(ATTN_QSUB * tk, seq)
    nq = seq // tq
    qspec = pl.BlockSpec((tq, LANES), lambda b, h, i: (b * nq + i, h))
    kspec = pl.BlockSpec((seq, LANES), lambda b, h, i: (b, h))
    vtspec = pl.BlockSpec((seq // tk, LANES, tk), lambda b, h, i: (b, h, 0))
    return pl.pallas_call(
        _sb_kernel,
        grid=(batch, SB_HEADS, nq),
        in_specs=[qspec, kspec, vtspec],
        out_specs=qspec,
        out_shape=jax.ShapeDtypeStruct((n, d), BF16),
        scratch_shapes=[pltpu.VMEM((tq // tk, LANES, tk), F32)],
        compiler_params=_cparams(("parallel", "parallel", "arbitrary")),
        name="stick_breaking",
    )(q, k, vt)


def kernel(x, ffn_pre_g, ffn_post_g, ffn_w_gate, ffn_w_up, ffn_w_down, mix_pre_g, mix_post_g, hyb_w_in, hyb_w_out, diff_lambda, diff_subln_g, gla_w_a2, gla_b_a, gla_norm_g, sb_w_qkv, sb_w_out):
    batch, seq, d = x.shape
    depth = ffn_pre_g.shape[0]
    assert d // SB_HEADS == LANES and DIFF_DV == LANES and GLA_DV == LANES
    h = x.reshape(batch * seq, d)

    def ffn(h, layer, j):
        return _ffn(h, ffn_pre_g[layer, j], ffn_w_gate[layer, j], ffn_w_up[layer, j], ffn_w_down[layer, j],
                    ffn_post_g[layer, j])

    for layer in range(depth):
        h = ffn(h, layer, 0)
        if layer % 2 == 0:
            e = layer // 2
            lambda_init = 0.8 - 0.6 * math.exp(-0.3 * layer)
            dq, dk, dv, gq, gk, la, gv, gg = _proj_even(h, mix_pre_g[layer], hyb_w_in[e], gla_w_a2[e], gla_b_a[e])
            a_out = _diff_attention(dq, dk, dv, diff_lambda[e], diff_subln_g[e], lambda_init, batch, seq)
            b_out = _gla(gq, gk, la, gv, gg, gla_norm_g[e], batch, seq)
            h = _out_proj(h, a_out, 0, b_out, 0, hyb_w_out[e], mix_post_g[layer])
        else:
            o = layer // 2
            q, k, v = _proj_odd(h, mix_pre_g[layer], sb_w_qkv[o])
            att = _stick_breaking(q, k, v, batch, seq)
            h = _out_proj(h, att, 0, att, 1, sb_w_out[o], mix_post_g[layer])
        h = ffn(h, layer, 1)
    return h.reshape(batch, seq, d)
```

```python
import functools
import math

import jax
import jax.numpy as jnp
from jax import lax
from jax.experimental import pallas as pl
from jax.experimental.pallas import tpu as pltpu

F32 = jnp.float32
BF16 = jnp.bfloat16

EPS = 1e-6
DIFF_HEADS = 4
DIFF_DH = 64
DIFF_DV = 2 * DIFF_DH
GLA_HEADS = 4
GLA_DK = 64
GLA_DV = 128
GLA_RANK = 16
GLA_TAU = 16.0
GLA_CHUNK = 64
GLA_SUB = 16
SB_HEADS = 8

LANES = 128
VMEM_LIMIT = 56 * 1024 * 1024

TOKEN_TILE = 512
ATTN_TILE = 256
ATTN_QSUB = 4
DIFF_KV_UNROLL = 4
ONES_ROWS = 16
LOG2_FLUSH = -150.0
LOG2E = math.log2(math.e)
GLA_TILE = 512
FFN_PARTS = 2


def _cparams(sem):
    return pltpu.CompilerParams(dimension_semantics=sem, vmem_limit_bytes=VMEM_LIMIT)


def _rms(x, g):
    return x * lax.rsqrt(jnp.mean(x * x, axis=-1, keepdims=True) + EPS) * g


def _dot(a, b):
    return jnp.dot(a, b, preferred_element_type=F32)


def _dot_nt(a, b):
    return lax.dot_general(a, b, (((1,), (1,)), ((), ())), preferred_element_type=F32)


def _dot_tn(a, b):
    return lax.dot_general(a, b, (((0,), (0,)), ((), ())), preferred_element_type=F32)


def _split3(x):
    hi = x.astype(BF16)
    r1 = x - hi.astype(F32)
    mid = r1.astype(BF16)
    lo = (r1 - mid.astype(F32)).astype(BF16)
    return hi, mid, lo


def _softplus_parts(z):
    return jnp.maximum(z, 0.0), jnp.log(1.0 + jnp.exp(-jnp.abs(z)))


def _ffn_kernel(h_ref, gpre_ref, wgu_ref, wd_ref, gpost_ref, o_ref):
    tm = h_ref.shape[0]
    dff = wd_ref.shape[0]
    parts = [pl.ds(p * (tm // FFN_PARTS), tm // FFN_PARTS) for p in range(FFN_PARTS)]
    xns = [_rms(h_ref[r, :], gpre_ref[...]).astype(BF16) for r in parts]
    acts = []
    for xn in xns:
        gu = _dot(xn, wgu_ref[...])
        gate = gu[:, :dff]
        acts.append((gate * jax.nn.sigmoid(gate) * gu[:, dff:]).astype(BF16))
    fs = [_dot(act, wd_ref[...]) for act in acts]
    for r, f in zip(parts, fs):
        o_ref[r, :] = h_ref[r, :] + 0.5 * _rms(f, gpost_ref[...])


def _ffn(h, g_pre, wg, wu, wd, g_post):
    n, d = h.shape
    dff = wg.shape[1]
    tm = min(TOKEN_TILE, n)
    row = lambda i: (i, 0)
    fixed = lambda i: (0, 0)
    return pl.pallas_call(
        _ffn_kernel,
        grid=(n // tm,),
        in_specs=[pl.BlockSpec((tm, d), row), pl.BlockSpec((1, d), fixed),
                  pl.BlockSpec((d, 2 * dff), fixed),
                  pl.BlockSpec((dff, d), fixed), pl.BlockSpec((1, d), fixed)],
        out_specs=pl.BlockSpec((tm, d), row),
        out_shape=jax.ShapeDtypeStruct((n, d), F32),
        compiler_params=_cparams(("parallel",)),
        name="ffn",
    )(h, g_pre.reshape(1, d), jnp.concatenate([wg, wu], axis=1).astype(BF16), wd.astype(BF16),
      g_post.reshape(1, d))


def _store_key_blocks(vt_ref, vt):
    tk = vt_ref.shape[2]
    for c in range(vt_ref.shape[0]):
        vt_ref[c] = vt[:, c * tk:(c + 1) * tk].astype(vt_ref.dtype)


def _proj_even_kernel(h_ref, g_ref, w_ref, wvt_ref, ones_ref, wga_ref, wa2_ref, ba_ref,
                      dq_ref, dk_ref, dvt_ref, gq_ref, gk_ref, la_ref, gv_ref, gg_ref):
    m = _rms(h_ref[...], g_ref[...]).astype(BF16)
    wd = dq_ref.shape[1]

    def seg(i):
        return _dot(m, w_ref[:, i * wd:(i + 1) * wd])

    dq_ref[...] = (seg(0) * (DIFF_DH ** -0.5 * LOG2E)).astype(BF16)
    dk_ref[...] = seg(1).astype(BF16)
    _store_key_blocks(dvt_ref, _dot_nt(wvt_ref[...], m) + ones_ref[...])
    gq_ref[...] = seg(2) * GLA_DK ** -0.5
    gk_ref[...] = seg(3)
    gv_ref[...] = seg(4)
    gg_ref[...] = seg(5)
    ga = _dot(m, wga_ref[...])
    ga_hi, ga_lo, _ = _split3(ga)
    w_hi, w_lo, _ = _split3(wa2_ref[...])
    x = _dot(ga_hi, w_hi) + _dot(ga_lo, w_hi) + _dot(ga_hi, w_lo) + ba_ref[...]
    relu_neg, log_term = _softplus_parts(-x)
    la_ref[...] = -(relu_neg + log_term) * (1.0 / GLA_TAU)


def _pad_heads(w, heads, dh):
    lead = w.shape[:-1]
    w = w.reshape(lead + (heads, dh))
    w = jnp.pad(w, [(0, 0)] * len(lead) + [(0, 0), (0, LANES - dh)])
    return w.reshape(lead + (heads * LANES,))


def _proj_even(h, g, w_in, w_a2, b_a):
    n, d = h.shape
    tm = min(TOKEN_TILE, n)
    wd = DIFF_HEADS * DIFF_DV
    c = 0
    parts = []
    for width in (wd, wd, wd, GLA_HEADS * GLA_DK, GLA_HEADS * GLA_DK, GLA_HEADS * GLA_DV, GLA_HEADS * GLA_DV):
        parts.append(w_in[:, c:c + width])
        c += width
    parts[3] = _pad_heads(parts[3], GLA_HEADS, GLA_DK)
    parts[4] = _pad_heads(parts[4], GLA_HEADS, GLA_DK)
    w_vt = parts.pop(2).T.reshape(DIFF_HEADS, DIFF_DV, d)
    w_vt = jnp.pad(w_vt, ((0, 0), (0, ONES_ROWS), (0, 0))).reshape(-1, d).astype(BF16)
    ones_col = jnp.pad(jnp.zeros((DIFF_HEADS, DIFF_DV, 1), F32), ((0, 0), (0, ONES_ROWS), (0, 0)),
                       constant_values=1.0).reshape(-1, 1)
    vt_rows = w_vt.shape[0]
    w_main = jnp.concatenate(parts, axis=1).astype(BF16)
    tk = min(ATTN_TILE, tm)
    w_ga = jnp.pad(w_in[:, c:c + GLA_RANK], ((0, 0), (0, LANES - GLA_RANK))).astype(BF16)
    wa2 = jnp.pad(_pad_heads(w_a2, GLA_HEADS, GLA_DK), ((0, LANES - GLA_RANK), (0, 0)))
    ba = _pad_heads(b_a, GLA_HEADS, GLA_DK).reshape(1, wd)
    row = lambda i: (i, 0)
    fixed = lambda i: (0, 0)
    out_bf = jax.ShapeDtypeStruct((n, wd), BF16)
    out_f = jax.ShapeDtypeStruct((n, wd), F32)
    out_vt = jax.ShapeDtypeStruct((n // tk, vt_rows, tk), BF16)
    rowspec = pl.BlockSpec((tm, wd), row)
    vtspec = pl.BlockSpec((tm // tk, vt_rows, tk), lambda i: (i, 0, 0))
    return pl.pallas_call(
        _proj_even_kernel,
        grid=(n // tm,),
        in_specs=[pl.BlockSpec((tm, d), row), pl.BlockSpec((1, d), fixed),
                  pl.BlockSpec(w_main.shape, fixed), pl.BlockSpec(w_vt.shape, fixed),
                  pl.BlockSpec(ones_col.shape, fixed), pl.BlockSpec(w_ga.shape, fixed),
                  pl.BlockSpec(wa2.shape, fixed), pl.BlockSpec((1, wd), fixed)],
        out_specs=[rowspec, rowspec, vtspec] + [rowspec] * 5,
        out_shape=[out_bf, out_bf, out_vt, out_f, out_f, out_f, out_f, out_f],
        compiler_params=_cparams(("parallel",)),
        name="proj_even",
    )(h, g.reshape(1, d), w_main, w_vt, ones_col, w_ga, wa2, ba)


def _proj_odd_kernel(h_ref, g_ref, w_ref, wvt_ref, q_ref, k_ref, vt_ref):
    m = _rms(h_ref[...], g_ref[...]).astype(BF16)
    d = q_ref.shape[1]
    dh = d // SB_HEADS
    q_ref[...] = (_dot(m, w_ref[:, 0:d]) * (dh ** -0.5 * LOG2E)).astype(BF16)
    k_ref[...] = _dot(m, w_ref[:, d:2 * d]).astype(BF16)
    _store_key_blocks(vt_ref, _dot_nt(wvt_ref[...], m))


def _proj_odd(h, g, w_qkv):
    n, d = h.shape
    tm = min(TOKEN_TILE, n)
    tk = min(ATTN_TILE, tm)
    row = lambda i: (i, 0)
    fixed = lambda i: (0, 0)
    out = jax.ShapeDtypeStruct((n, d), BF16)
    w_qk = w_qkv[:, :2 * d].astype(BF16)
    w_vt = w_qkv[:, 2 * d:].T.astype(BF16)
    return pl.pallas_call(
        _proj_odd_kernel,
        grid=(n // tm,),
        in_specs=[pl.BlockSpec((tm, d), row), pl.BlockSpec((1, d), fixed),
                  pl.BlockSpec(w_qk.shape, fixed), pl.BlockSpec(w_vt.shape, fixed)],
        out_specs=[pl.BlockSpec((tm, d), row), pl.BlockSpec((tm, d), row),
                   pl.BlockSpec((tm // tk, d, tk), lambda i: (i, 0, 0))],
        out_shape=[out, out, jax.ShapeDtypeStruct((n // tk, d, tk), BF16)],
        compiler_params=_cparams(("parallel",)),
        name="proj_odd",
    )(h, g.reshape(1, d), w_qk, w_vt)


def _out_kernel(h_ref, a_ref, b_ref, w_ref, g_ref, o_ref):
    half = a_ref.shape[1]
    y = _dot(a_ref[...], w_ref[0:half, :]) + _dot(b_ref[...], w_ref[half:, :])
    o_ref[...] = h_ref[...] + _rms(y, g_ref[...])


def _out_proj(h, a, a_col, b, b_col, w_out, g):
    n, d = h.shape
    tm = min(TOKEN_TILE, n)
    half = w_out.shape[0] // 2
    row = lambda i: (i, 0)
    fixed = lambda i: (0, 0)
    return pl.pallas_call(
        _out_kernel,
        grid=(n // tm,),
        in_specs=[pl.BlockSpec((tm, d), row),
                  pl.BlockSpec((tm, half), lambda i: (i, a_col)),
                  pl.BlockSpec((tm, half), lambda i: (i, b_col)),
                  pl.BlockSpec(w_out.shape, fixed), pl.BlockSpec((1, d), fixed)],
        out_specs=pl.BlockSpec((tm, d), row),
        out_shape=jax.ShapeDtypeStruct((n, d), F32),
        compiler_params=_cparams(("parallel",)),
        name="out_proj",
    )(h, a, b, w_out.astype(BF16), g.reshape(1, d))


def _diff_kernel(lam_ref, g_ref, q_ref, k_ref, vt_ref, o_ref, m_ref, l_ref, acc_ref, *, lambda_init):
    i = pl.program_id(2)
    tk = vt_ref.shape[2]
    dv = DIFF_DV
    nsub = q_ref.shape[0] // tk
    base = i * nsub
    lane = lax.broadcasted_iota(jnp.int32, (tk, LANES), 1)
    causal = (lax.broadcasted_iota(jnp.int32, (tk, tk), 0) <= lax.broadcasted_iota(jnp.int32, (tk, tk), 1))

    def q_map(a, mi):
        q = q_ref[a * tk:(a + 1) * tk, :]
        keep = (lane < DIFF_DH) if mi == 0 else (lane >= DIFF_DH)
        return jnp.where(keep, q, jnp.zeros_like(q))

    def run_tiles(tiles):
        scores = [_dot_nt(kb, q_map(a, mi)) for a, mi, kb, _, _, _ in tiles]
        pending = []
        for (a, mi, _, vtb, first, masked), s in zip(tiles, scores):
            if masked:
                s = jnp.where(causal, s, -jnp.inf)
            m_blk = jnp.max(s, axis=0, keepdims=True)
            if first:
                alpha = None
                m_new = m_blk
            else:
                m_prev = m_ref[a, mi]
                m_new = jnp.maximum(m_prev, m_blk)
                alpha = jnp.exp2(m_prev - m_new)
            m_ref[a, mi] = m_new
            p = jnp.exp2(s - m_new).astype(BF16)
            pending.append((a, mi, alpha, _dot(vtb, p)))
        for a, mi, alpha, pv in pending:
            if alpha is None:
                acc_ref[a, mi] = pv[:dv]
                l_ref[a, mi] = pv[dv:dv + 1]
            else:
                acc_ref[a, mi] = alpha * acc_ref[a, mi] + pv[:dv]
                l_ref[a, mi] = alpha * l_ref[a, mi] + pv[dv:dv + 1]

    def key_block(j):
        return k_ref[pl.ds(pl.multiple_of(j * tk, tk), tk), :], vt_ref[j]

    band = []
    for c in range(nsub):
        kb, vtb = key_block(base + c)
        band += [(a, mi, kb, vtb, c == 0, a == c) for a in range(c, nsub) for mi in range(2)]
    run_tiles(band)

    kv_unroll = DIFF_KV_UNROLL if nsub % DIFF_KV_UNROLL == 0 else 1

    def body(step, carry):
        tiles = []
        for u in range(kv_unroll):
            kb, vtb = key_block(step * kv_unroll + u)
            tiles += [(a, mi, kb, vtb, False, False) for a in range(nsub) for mi in range(2)]
        run_tiles(tiles)
        return carry

    lax.fori_loop(0, base // kv_unroll, body, 0)

    lp = lam_ref[...]
    lam = (jnp.exp(jnp.sum(lp[0:1] * lp[1:2], axis=-1, keepdims=True))
           - jnp.exp(jnp.sum(lp[2:3] * lp[3:4], axis=-1, keepdims=True)) + lambda_init)
    for a in range(nsub):
        o = acc_ref[a, 0] / l_ref[a, 0] - lam * (acc_ref[a, 1] / l_ref[a, 1])
        o = o * lax.rsqrt(jnp.mean(o * o, axis=0, keepdims=True) + EPS) * (1.0 - lambda_init)
        o_ref[a * tk:(a + 1) * tk, :] = (o.T * g_ref[...]).astype(o_ref.dtype)


def _diff_attention(dq, dk, dvt, lam_params, subln_g, lambda_init, batch, seq):
    n, wd = dq.shape
    tk = dvt.shape[2]
    tq = min(ATTN_QSUB * tk, seq)
    nq = seq // tq
    nsub = tq // tk
    qspec = pl.BlockSpec((tq, LANES), lambda b, h, i: (b * nq + i, h))
    kspec = pl.BlockSpec((seq, LANES), lambda b, h, i: (b, h))
    vtspec = pl.BlockSpec((seq // tk, DIFF_DV + ONES_ROWS, tk), lambda b, h, i: (b, h, 0))
    fixed = lambda b, h, i: (0, 0)
    return pl.pallas_call(
        functools.partial(_diff_kernel, lambda_init=lambda_init),
        grid=(batch, DIFF_HEADS, nq),
        in_specs=[pl.BlockSpec(lam_params.shape, fixed), pl.BlockSpec((1, DIFF_DV), fixed), qspec, kspec, vtspec],
        out_specs=qspec,
        out_shape=jax.ShapeDtypeStruct((n, wd), BF16),
        scratch_shapes=[pltpu.VMEM((nsub, 2, 1, tk), F32), pltpu.VMEM((nsub, 2, 1, tk), F32),
                        pltpu.VMEM((nsub, 2, DIFF_DV, tk), F32)],
        compiler_params=_cparams(("parallel", "parallel", "arbitrary")),
        name="diff_attn",
    )(lam_params, subln_g.reshape(1, DIFF_DV), dq, dk, dvt)


def _gla_kernel(q_ref, k_ref, la_ref, v_ref, gg_ref, g_ref, o_ref, state_ref):
    c = GLA_CHUNK
    sub = GLA_SUB
    nsub = c // sub
    heads = state_ref.shape[0]

    @pl.when(pl.program_id(1) == 0)
    def _():
        state_ref[...] = jnp.zeros(state_ref.shape, F32)

    r_i = lax.broadcasted_iota(jnp.int32, (c, c), 0)
    c_i = lax.broadcasted_iota(jnp.int32, (c, c), 1)
    tril = (c_i <= r_i).astype(BF16)
    off_mask = c_i < (r_i // sub) * sub
    diag_mask = c_i <= r_i
    lane3 = lax.broadcasted_iota(jnp.int32, (nsub, sub, c), 2)
    blk3 = lax.broadcasted_iota(jnp.int32, (nsub, sub, c), 0)
    g = g_ref[...]

    def chunk(ci, carry):
        rows = pl.ds(pl.multiple_of(ci * c, c), c)
        cols = [slice(h * LANES, (h + 1) * LANES) for h in range(heads)]

        cums = []
        for h in range(heads):
            la_hi, la_mid, la_lo = _split3(la_ref[rows, cols[h]])
            cums.append(_dot(tril, la_hi) + _dot(tril, la_mid) + _dot(tril, la_lo))

        mids = []
        for h in range(heads):
            q = q_ref[rows, cols[h]]
            k = k_ref[rows, cols[h]]
            v = v_ref[rows, cols[h]].astype(BF16)
            cum = cums[h]
            last = cum[c - 1:c, :]
            state = state_ref[h]
            o_inter = _dot_nt((q * jnp.exp(cum)).astype(BF16), state.astype(BF16))
            k_dec = k * jnp.exp(last - cum)
            new_state = state * jnp.exp(last) + _dot_tn(v, k_dec.astype(BF16))

            pieces = [jnp.zeros((sub, c), F32)]
            for blk in range(1, nsub):
                lo = blk * sub
                ref = cum[lo:lo + 1, :]
                qs = q[lo:lo + sub, :] * jnp.exp(cum[lo:lo + sub, :] - ref)
                ks = k * jnp.exp(jnp.minimum(ref - cum, 0.0))
                pieces.append(_dot_nt(qs.astype(BF16), ks.astype(BF16)))
            a_off = jnp.concatenate(pieces, axis=0)

            q3 = q.reshape(nsub, sub, LANES)
            k3 = k.reshape(nsub, sub, LANES)
            c3 = cum.reshape(nsub, sub, LANES)
            a3 = jnp.zeros((nsub, sub, c), F32)
            for j in range(sub):
                dec = jnp.exp(jnp.minimum(c3 - c3[:, j:j + 1, :], 0.0))
                col = jnp.sum(q3 * k3[:, j:j + 1, :] * dec, axis=-1, keepdims=True)
                a3 = jnp.where(lane3 == blk3 * sub + j, col, a3)
            a_diag = a3.reshape(c, c)

            attn = jnp.where(off_mask, a_off, jnp.where(diag_mask, a_diag, 0.0))
            mids.append((o_inter, _dot(attn.astype(BF16), v), new_state))

        for h in range(heads):
            o_inter, o_intra, new_state = mids[h]
            state_ref[h] = new_state
            gate = gg_ref[rows, cols[h]]
            o_ref[rows, cols[h]] = (_rms(o_inter + o_intra, g) * (gate * jax.nn.sigmoid(gate))).astype(o_ref.dtype)
        return carry

    lax.fori_loop(0, q_ref.shape[0] // c, chunk, 0)


def _gla(gq, gk, la, gv, gg, norm_g, batch, seq):
    n, wd = gv.shape
    t = min(GLA_TILE, seq)
    nt = seq // t
    spec = pl.BlockSpec((t, wd), lambda b, i: (b * nt + i, 0))
    return pl.pallas_call(
        _gla_kernel,
        grid=(batch, nt),
        in_specs=[spec, spec, spec, spec, spec, pl.BlockSpec((1, GLA_DV), lambda b, i: (0, 0))],
        out_specs=spec,
        out_shape=jax.ShapeDtypeStruct((n, wd), BF16),
        scratch_shapes=[pltpu.VMEM((GLA_HEADS, GLA_DV, LANES), F32)],
        compiler_params=_cparams(("parallel", "arbitrary")),
        name="gla",
    )(gq, gk, la, gv, gg, norm_g.reshape(1, GLA_DV))


SUM_ROWS = 8


def _sb_kernel(q_ref, k_ref, vt_ref, o_ref, acc_ref):
    i = pl.program_id(2)
    tk = vt_ref.shape[2]
    nsub = q_ref.shape[0] // tk
    base = i * nsub
    r_s = lax.broadcasted_iota(jnp.int32, (tk + SUM_ROWS, tk), 0)
    c_s = lax.broadcasted_iota(jnp.int32, (tk + SUM_ROWS, tk), 1)
    suffix_mat = ((c_s > r_s) | (r_s >= tk)).astype(BF16)
    strict = (lax.broadcasted_iota(jnp.int32, (tk, tk), 0) < lax.broadcasted_iota(jnp.int32, (tk, tk), 1))

    def run_tiles(tiles, later):
        later = list(later)
        logits = [_dot_nt(kb, q_ref[a * tk:(a + 1) * tk, :]) for a, kb, _, _ in tiles]
        mids = []
        for (a, _, _, masked), z in zip(tiles, logits):
            log_term = jnp.log2(1.0 + jnp.exp2(-jnp.abs(z)))
            log_beta = jnp.minimum(z, 0.0) - log_term
            log_keep = log_beta - z
            if masked:
                log_keep = jnp.where(strict, log_keep, 0.0)
            mids.append((log_beta, _dot(suffix_mat, log_keep.astype(BF16))))
        pending = []
        for (a, _, vtb, masked), (log_beta, sums) in zip(tiles, mids):
            arg = log_beta + sums[:tk]
            if later[a] is not None:
                arg = arg + later[a]
            att = jnp.exp2(arg)
            if masked:
                att = jnp.where(strict, att, 0.0)
            pending.append((a, later[a] is None, _dot(vtb, att.astype(BF16))))
            later[a] = sums[tk:tk + 1] if later[a] is None else later[a] + sums[tk:tk + 1]
        for a, first, contrib in pending:
            if first:
                acc_ref[a] = contrib
            else:
                acc_ref[a] += contrib
        return tuple(later)

    def key_block(j):
        return k_ref[pl.ds(pl.multiple_of(j * tk, tk), tk), :], vt_ref[j]

    def retire(later_a, still_valid):
        return jnp.where(still_valid, later_a, -jnp.inf)

    def lockstep_tiles(offset, later):
        tiles, kept = [], []
        for a in range(nsub):
            j = base + a - offset
            kb, vtb = key_block(jnp.maximum(j, 0))
            tiles.append((a, kb, vtb, False))
            kept.append(retire(later[a], j >= 0))
        return tiles, kept

    diag = []
    for a in range(nsub):
        kb, vtb = key_block(base + a)
        diag.append((a, kb, vtb, True))
    later = run_tiles(diag, [None] * nsub)
    later = run_tiles(*lockstep_tiles(1, later))

    def alive(later):
        return (jnp.max(functools.reduce(jnp.maximum, later)) > LOG2_FLUSH).astype(jnp.int32)

    def cond(carry):
        step, live, _ = carry
        return (step < base + nsub - 2) & (live > 0)

    def body(carry):
        step, _, later = carry
        later = run_tiles(*lockstep_tiles(step + 2, later))
        return step + 1, alive(later), later

    lax.while_loop(cond, body, (jnp.int32(0), alive(later), later))
    for a in range(nsub):
        o_ref[a * tk:(a + 1) * tk, :] = acc_ref[a].T.astype(o_ref.dtype)


def _stick_breaking(q, k, vt, batch, seq):
    n, d = q.shape
    tk = vt.shape[2]
    tq = min(ATTN_QSUB * tk, seq)
    nq = seq // tq
    qspec = pl.BlockSpec((tq, LANES), lambda b, h, i: (b * nq + i, h))
    kspec = pl.BlockSpec((seq, LANES), lambda b, h, i: (b, h))
    vtspec = pl.BlockSpec((seq // tk, LANES, tk), lambda b, h, i: (b, h, 0))
    return pl.pallas_call(
        _sb_kernel,
        grid=(batch, SB_HEADS, nq),
        in_specs=[qspec, kspec, vtspec],
        out_specs=qspec,
        out_shape=jax.ShapeDtypeStruct((n, d), BF16),
        scratch_shapes=[pltpu.VMEM((tq // tk, LANES, tk), F32)],
        compiler_params=_cparams(("parallel", "parallel", "arbitrary")),
        name="stick_breaking",
    )(q, k, vt)


def kernel(x, ffn_pre_g, ffn_post_g, ffn_w_gate, ffn_w_up, ffn_w_down, mix_pre_g, mix_post_g, hyb_w_in, hyb_w_out, diff_lambda, diff_subln_g, gla_w_a2, gla_b_a, gla_norm_g, sb_w_qkv, sb_w_out):
    batch, seq, d = x.shape
    depth = ffn_pre_g.shape[0]
    assert d // SB_HEADS == LANES and DIFF_DV == LANES and GLA_DV == LANES
    h = x.reshape(batch * seq, d)

    def ffn(h, layer, j):
        return _ffn(h, ffn_pre_g[layer, j], ffn_w_gate[layer, j], ffn_w_up[layer, j], ffn_w_down[layer, j],
                    ffn_post_g[layer, j])

    for layer in range(depth):
        h = ffn(h, layer, 0)
        if layer % 2 == 0:
            e = layer // 2
            lambda_init = 0.8 - 0.6 * math.exp(-0.3 * layer)
            dq, dk, dv, gq, gk, la, gv, gg = _proj_even(h, mix_pre_g[layer], hyb_w_in[e], gla_w_a2[e], gla_b_a[e])
            a_out = _diff_attention(dq, dk, dv, diff_lambda[e], diff_subln_g[e], lambda_init, batch, seq)
            b_out = _gla(gq, gk, la, gv, gg, gla_norm_g[e], batch, seq)
            h = _out_proj(h, a_out, 0, b_out, 0, hyb_w_out[e], mix_post_g[layer])
        else:
            o = layer // 2
            q, k, v = _proj_odd(h, mix_pre_g[layer], sb_w_qkv[o])
            att = _stick_breaking(q, k, v, batch, seq)
            h = _out_proj(h, att, 0, att, 1, sb_w_out[o], mix_post_g[layer])
        h = ffn(h, layer, 1)
    return h.reshape(batch, seq, d)
```

```python
import functools
import math

import jax
import jax.numpy as jnp
from jax import lax
from jax.experimental import pallas as pl
from jax.experimental.pallas import tpu as pltpu

F32 = jnp.float32
BF16 = jnp.bfloat16

EPS = 1e-6
DIFF_HEADS = 4
DIFF_DH = 64
DIFF_DV = 2 * DIFF_DH
GLA_HEADS = 4
GLA_DK = 64
GLA_DV = 128
GLA_RANK = 16
GLA_TAU = 16.0
GLA_CHUNK = 64
GLA_SUB = 16
SB_HEADS = 8

LANES = 128
VMEM_LIMIT = 56 * 1024 * 1024

TOKEN_TILE = 512
ATTN_TILE = 256
ATTN_QSUB = 4
SB_QSUB = 8
DIFF_KV_UNROLL = 4
ONES_ROWS = 16
LOG2_FLUSH = -150.0
LOG2E = math.log2(math.e)
GLA_TILE = 512
FFN_PARTS = 2


def _cparams(sem):
    return pltpu.CompilerParams(dimension_semantics=sem, vmem_limit_bytes=VMEM_LIMIT)


def _rms(x, g):
    return x * lax.rsqrt(jnp.mean(x * x, axis=-1, keepdims=True) + EPS) * g


def _dot(a, b):
    return jnp.dot(a, b, preferred_element_type=F32)


def _dot_nt(a, b):
    return lax.dot_general(a, b, (((1,), (1,)), ((), ())), preferred_element_type=F32)


def _dot_tn(a, b):
    return lax.dot_general(a, b, (((0,), (0,)), ((), ())), preferred_element_type=F32)


def _split3(x):
    hi = x.astype(BF16)
    r1 = x - hi.astype(F32)
    mid = r1.astype(BF16)
    lo = (r1 - mid.astype(F32)).astype(BF16)
    return hi, mid, lo


def _softplus_parts(z):
    return jnp.maximum(z, 0.0), jnp.log(1.0 + jnp.exp(-jnp.abs(z)))


def _ffn_body(hs, gpre_ref, wgu_ref, wd_ref, gpost_ref):
    dff = wd_ref.shape[0]
    xns = [_rms(h, gpre_ref[...]).astype(BF16) for h in hs]
    acts = []
    for xn in xns:
        gu = _dot(xn, wgu_ref[...])
        gate = gu[:, :dff]
        acts.append((gate * jax.nn.sigmoid(gate) * gu[:, dff:]).astype(BF16))
    fs = [_dot(act, wd_ref[...]) for act in acts]
    return [h + 0.5 * _rms(f, gpost_ref[...]) for h, f in zip(hs, fs)]


def _row_parts(tm):
    return [pl.ds(p * (tm // FFN_PARTS), tm // FFN_PARTS) for p in range(FFN_PARTS)]


def _ffn_kernel(h_ref, gpre_ref, wgu_ref, wd_ref, gpost_ref, o_ref):
    parts = _row_parts(h_ref.shape[0])
    outs = _ffn_body([h_ref[r, :] for r in parts], gpre_ref, wgu_ref, wd_ref, gpost_ref)
    for r, o in zip(parts, outs):
        o_ref[r, :] = o


def _mix_ffn_kernel(h_ref, a_ref, b_ref, wo_ref, gmix_ref, gpre_ref, wgu_ref, wd_ref, gpost_ref, o_ref):
    parts = _row_parts(h_ref.shape[0])
    half = a_ref.shape[1]
    ys = [_dot(a_ref[r, :], wo_ref[0:half, :]) + _dot(b_ref[r, :], wo_ref[half:, :]) for r in parts]
    hs = [h_ref[r, :] + _rms(y, gmix_ref[...]) for r, y in zip(parts, ys)]
    outs = _ffn_body(hs, gpre_ref, wgu_ref, wd_ref, gpost_ref)
    for r, o in zip(parts, outs):
        o_ref[r, :] = o


def _ffn(h, g_pre, wg, wu, wd, g_post, mix=None):
    n, d = h.shape
    dff = wg.shape[1]
    tm = min(TOKEN_TILE, n)
    row = lambda i: (i, 0)
    fixed = lambda i: (0, 0)
    ffn_specs = [pl.BlockSpec((1, d), fixed), pl.BlockSpec((d, 2 * dff), fixed),
                 pl.BlockSpec((dff, d), fixed), pl.BlockSpec((1, d), fixed)]
    ffn_args = (g_pre.reshape(1, d), jnp.concatenate([wg, wu], axis=1).astype(BF16), wd.astype(BF16),
                g_post.reshape(1, d))
    if mix is None:
        body, specs, args, name = _ffn_kernel, [], (), "ffn"
    else:
        a, a_col, b, b_col, w_out, g_mix = mix
        half = w_out.shape[0] // 2
        body, name = _mix_ffn_kernel, "mix_ffn"
        specs = [pl.BlockSpec((tm, half), lambda i: (i, a_col)), pl.BlockSpec((tm, half), lambda i: (i, b_col)),
                 pl.BlockSpec(w_out.shape, fixed), pl.BlockSpec((1, d), fixed)]
        args = (a, b, w_out.astype(BF16), g_mix.reshape(1, d))
    return pl.pallas_call(
        body,
        grid=(n // tm,),
        in_specs=[pl.BlockSpec((tm, d), row)] + specs + ffn_specs,
        out_specs=pl.BlockSpec((tm, d), row),
        out_shape=jax.ShapeDtypeStruct((n, d), F32),
        compiler_params=_cparams(("parallel",)),
        name=name,
    )(h, *args, *ffn_args)


def _store_key_blocks(vt_ref, vt):
    tk = vt_ref.shape[2]
    for c in range(vt_ref.shape[0]):
        vt_ref[c] = vt[:, c * tk:(c + 1) * tk].astype(vt_ref.dtype)


def _proj_even_kernel(h_ref, g_ref, w_ref, wvt_ref, ones_ref, wga_ref, wa2_ref, ba_ref,
                      dq_ref, dk_ref, dvt_ref, gq_ref, gk_ref, la_ref, gv_ref, gg_ref):
    wd = dq_ref.shape[1]
    tk = dvt_ref.shape[2]
    w_hi, w_lo, _ = _split3(wa2_ref[...])
    parts = _row_parts(h_ref.shape[0])
    ms = [_rms(h_ref[r, :], g_ref[...]).astype(BF16) for r in parts]
    for p, (r, m) in enumerate(zip(parts, ms)):
        def seg(i):
            return _dot(m, w_ref[:, i * wd:(i + 1) * wd])

        dq_ref[r, :] = (seg(0) * (DIFF_DH ** -0.5 * LOG2E)).astype(BF16)
        dk_ref[r, :] = seg(1).astype(BF16)
        vt = _dot_nt(wvt_ref[...], m) + ones_ref[...]
        blocks = vt.shape[1] // tk
        for c in range(blocks):
            dvt_ref[p * blocks + c] = vt[:, c * tk:(c + 1) * tk].astype(dvt_ref.dtype)
        gq_ref[r, :] = seg(2) * GLA_DK ** -0.5
        gk_ref[r, :] = seg(3)
        gv_ref[r, :] = seg(4)
        gg_ref[r, :] = seg(5)
        ga_hi, ga_lo, _ = _split3(_dot(m, wga_ref[...]))
        x = _dot(ga_hi, w_hi) + _dot(ga_lo, w_hi) + _dot(ga_hi, w_lo) + ba_ref[...]
        relu_neg, log_term = _softplus_parts(-x)
        la_ref[r, :] = -(relu_neg + log_term) * (1.0 / GLA_TAU)


def _pad_heads(w, heads, dh):
    lead = w.shape[:-1]
    w = w.reshape(lead + (heads, dh))
    w = jnp.pad(w, [(0, 0)] * len(lead) + [(0, 0), (0, LANES - dh)])
    return w.reshape(lead + (heads * LANES,))


def _proj_even(h, g, w_in, w_a2, b_a):
    n, d = h.shape
    tm = min(TOKEN_TILE, n)
    wd = DIFF_HEADS * DIFF_DV
    c = 0
    parts = []
    for width in (wd, wd, wd, GLA_HEADS * GLA_DK, GLA_HEADS * GLA_DK, GLA_HEADS * GLA_DV, GLA_HEADS * GLA_DV):
        parts.append(w_in[:, c:c + width])
        c += width
    parts[3] = _pad_heads(parts[3], GLA_HEADS, GLA_DK)
    parts[4] = _pad_heads(parts[4], GLA_HEADS, GLA_DK)
    w_vt = parts.pop(2).T.reshape(DIFF_HEADS, DIFF_DV, d)
    w_vt = jnp.pad(w_vt, ((0, 0), (0, ONES_ROWS), (0, 0))).reshape(-1, d).astype(BF16)
    ones_col = jnp.pad(jnp.zeros((DIFF_HEADS, DIFF_DV, 1), F32), ((0, 0), (0, ONES_ROWS), (0, 0)),
                       constant_values=1.0).reshape(-1, 1)
    vt_rows = w_vt.shape[0]
    w_main = jnp.concatenate(parts, axis=1).astype(BF16)
    tk = min(ATTN_TILE, tm)
    assert (tm // FFN_PARTS) % tk == 0
    w_ga = jnp.pad(w_in[:, c:c + GLA_RANK], ((0, 0), (0, LANES - GLA_RANK))).astype(BF16)
    wa2 = jnp.pad(_pad_heads(w_a2, GLA_HEADS, GLA_DK), ((0, LANES - GLA_RANK), (0, 0)))
    ba = _pad_heads(b_a, GLA_HEADS, GLA_DK).reshape(1, wd)
    row = lambda i: (i, 0)
    fixed = lambda i: (0, 0)
    out_bf = jax.ShapeDtypeStruct((n, wd), BF16)
    out_f = jax.ShapeDtypeStruct((n, wd), F32)
    out_vt = jax.ShapeDtypeStruct((n // tk, vt_rows, tk), BF16)
    rowspec = pl.BlockSpec((tm, wd), row)
    vtspec = pl.BlockSpec((tm // tk, vt_rows, tk), lambda i: (i, 0, 0))
    return pl.pallas_call(
        _proj_even_kernel,
        grid=(n // tm,),
        in_specs=[pl.BlockSpec((tm, d), row), pl.BlockSpec((1, d), fixed),
                  pl.BlockSpec(w_main.shape, fixed), pl.BlockSpec(w_vt.shape, fixed),
                  pl.BlockSpec(ones_col.shape, fixed), pl.BlockSpec(w_ga.shape, fixed),
                  pl.BlockSpec(wa2.shape, fixed), pl.BlockSpec((1, wd), fixed)],
        out_specs=[rowspec, rowspec, vtspec] + [rowspec] * 5,
        out_shape=[out_bf, out_bf, out_vt, out_f, out_f, out_f, out_f, out_f],
        compiler_params=_cparams(("parallel",)),
        name="proj_even",
    )(h, g.reshape(1, d), w_main, w_vt, ones_col, w_ga, wa2, ba)


def _proj_odd_kernel(h_ref, g_ref, w_ref, wvt_ref, q_ref, k_ref, vt_ref):
    m = _rms(h_ref[...], g_ref[...]).astype(BF16)
    d = q_ref.shape[1]
    dh = d // SB_HEADS
    q_ref[...] = (_dot(m, w_ref[:, 0:d]) * (dh ** -0.5 * LOG2E)).astype(BF16)
    k_ref[...] = _dot(m, w_ref[:, d:2 * d]).astype(BF16)
    _store_key_blocks(vt_ref, _dot_nt(wvt_ref[...], m))


def _proj_odd(h, g, w_qkv):
    n, d = h.shape
    tm = min(TOKEN_TILE, n)
    tk = min(ATTN_TILE, tm)
    row = lambda i: (i, 0)
    fixed = lambda i: (0, 0)
    out = jax.ShapeDtypeStruct((n, d), BF16)
    w_qk = w_qkv[:, :2 * d].astype(BF16)
    w_vt = w_qkv[:, 2 * d:].T.astype(BF16)
    return pl.pallas_call(
        _proj_odd_kernel,
        grid=(n // tm,),
        in_specs=[pl.BlockSpec((tm, d), row), pl.BlockSpec((1, d), fixed),
                  pl.BlockSpec(w_qk.shape, fixed), pl.BlockSpec(w_vt.shape, fixed)],
        out_specs=[pl.BlockSpec((tm, d), row), pl.BlockSpec((tm, d), row),
                   pl.BlockSpec((tm // tk, d, tk), lambda i: (i, 0, 0))],
        out_shape=[out, out, jax.ShapeDtypeStruct((n // tk, d, tk), BF16)],
        compiler_params=_cparams(("parallel",)),
        name="proj_odd",
    )(h, g.reshape(1, d), w_qk, w_vt)


def _diff_kernel(lam_ref, g_ref, q_ref, k_ref, vt_ref, o_ref, m_ref, l_ref, acc_ref, *, lambda_init):
    i = pl.program_id(2)
    tk = vt_ref.shape[2]
    dv = DIFF_DV
    nsub = q_ref.shape[0] // tk
    base = i * nsub
    lane = lax.broadcasted_iota(jnp.int32, (tk, LANES), 1)
    causal = (lax.broadcasted_iota(jnp.int32, (tk, tk), 0) <= lax.broadcasted_iota(jnp.int32, (tk, tk), 1))

    def q_map(a, mi):
        q = q_ref[a * tk:(a + 1) * tk, :]
        keep = (lane < DIFF_DH) if mi == 0 else (lane >= DIFF_DH)
        return jnp.where(keep, q, jnp.zeros_like(q))

    def run_tiles(tiles):
        scores = [_dot_nt(kb, q_map(a, mi)) for a, mi, kb, _, _, _ in tiles]
        pending = []
        for (a, mi, _, vtb, first, masked), s in zip(tiles, scores):
            if masked:
                s = jnp.where(causal, s, -jnp.inf)
            m_blk = jnp.max(s, axis=0, keepdims=True)
            if first:
                alpha = None
                m_new = m_blk
            else:
                m_prev = m_ref[a, mi]
                m_new = jnp.maximum(m_prev, m_blk)
                alpha = jnp.exp2(m_prev - m_new)
            m_ref[a, mi] = m_new
            p = jnp.exp2(s - m_new).astype(BF16)
            pending.append((a, mi, alpha, _dot(vtb, p)))
        for a, mi, alpha, pv in pending:
            if alpha is None:
                acc_ref[a, mi] = pv[:dv]
                l_ref[a, mi] = pv[dv:dv + 1]
            else:
                acc_ref[a, mi] = alpha * acc_ref[a, mi] + pv[:dv]
                l_ref[a, mi] = alpha * l_ref[a, mi] + pv[dv:dv + 1]

    def key_block(j):
        return k_ref[pl.ds(pl.multiple_of(j * tk, tk), tk), :], vt_ref[j]

    band = []
    for c in range(nsub):
        kb, vtb = key_block(base + c)
        band += [(a, mi, kb, vtb, c == 0, a == c) for a in range(c, nsub) for mi in range(2)]
    run_tiles(band)

    kv_unroll = DIFF_KV_UNROLL if nsub % DIFF_KV_UNROLL == 0 else 1

    def body(step, carry):
        tiles = []
        for u in range(kv_unroll):
            kb, vtb = key_block(step * kv_unroll + u)
            tiles += [(a, mi, kb, vtb, False, False) for a in range(nsub) for mi in range(2)]
        run_tiles(tiles)
        return carry

    lax.fori_loop(0, base // kv_unroll, body, 0)

    lp = lam_ref[...]
    lam = (jnp.exp(jnp.sum(lp[0:1] * lp[1:2], axis=-1, keepdims=True))
           - jnp.exp(jnp.sum(lp[2:3] * lp[3:4], axis=-1, keepdims=True)) + lambda_init)
    for a in range(nsub):
        o = acc_ref[a, 0] / l_ref[a, 0] - lam * (acc_ref[a, 1] / l_ref[a, 1])
        o = o * lax.rsqrt(jnp.mean(o * o, axis=0, keepdims=True) + EPS) * (1.0 - lambda_init)
        o_ref[a * tk:(a + 1) * tk, :] = (o.T * g_ref[...]).astype(o_ref.dtype)


def _diff_attention(dq, dk, dvt, lam_params, subln_g, lambda_init, batch, seq):
    n, wd = dq.shape
    tk = dvt.shape[2]
    tq = min(ATTN_QSUB * tk, seq)
    nq = seq // tq
    nsub = tq // tk
    qspec = pl.BlockSpec((tq, LANES), lambda b, h, i: (b * nq + i, h))
    kspec = pl.BlockSpec((seq, LANES), lambda b, h, i: (b, h))
    vtspec = pl.BlockSpec((seq // tk, DIFF_DV + ONES_ROWS, tk), lambda b, h, i: (b, h, 0))
    fixed = lambda b, h, i: (0, 0)
    return pl.pallas_call(
        functools.partial(_diff_kernel, lambda_init=lambda_init),
        grid=(batch, DIFF_HEADS, nq),
        in_specs=[pl.BlockSpec(lam_params.shape, fixed), pl.BlockSpec((1, DIFF_DV), fixed), qspec, kspec, vtspec],
        out_specs=qspec,
        out_shape=jax.ShapeDtypeStruct((n, wd), BF16),
        scratch_shapes=[pltpu.VMEM((nsub, 2, 1, tk), F32), pltpu.VMEM((nsub, 2, 1, tk), F32),
                        pltpu.VMEM((nsub, 2, DIFF_DV, tk), F32)],
        compiler_params=_cparams(("parallel", "parallel", "arbitrary")),
        name="diff_attn",
    )(lam_params, subln_g.reshape(1, DIFF_DV), dq, dk, dvt)


def _gla_kernel(q_ref, k_ref, la_ref, v_ref, gg_ref, g_ref, o_ref, state_ref):
    c = GLA_CHUNK
    sub = GLA_SUB
    nsub = c // sub
    heads = state_ref.shape[0]

    @pl.when(pl.program_id(1) == 0)
    def _():
        state_ref[...] = jnp.zeros(state_ref.shape, F32)

    r_i = lax.broadcasted_iota(jnp.int32, (c, c), 0)
    c_i = lax.broadcasted_iota(jnp.int32, (c, c), 1)
    tril = (c_i <= r_i).astype(BF16)
    off_mask = c_i < (r_i // sub) * sub
    diag_mask = c_i <= r_i
    lane3 = lax.broadcasted_iota(jnp.int32, (nsub, sub, c), 2)
    blk3 = lax.broadcasted_iota(jnp.int32, (nsub, sub, c), 0)
    g = g_ref[...]

    def chunk(ci, carry):
        rows = pl.ds(pl.multiple_of(ci * c, c), c)
        cols = [slice(h * LANES, (h + 1) * LANES) for h in range(heads)]

        cums = []
        for h in range(heads):
            la_hi, la_mid, la_lo = _split3(la_ref[rows, cols[h]])
            cums.append(_dot(tril, la_hi) + _dot(tril, la_mid) + _dot(tril, la_lo))

        mids = []
        for h in range(heads):
            q = q_ref[rows, cols[h]]
            k = k_ref[rows, cols[h]]
            v = v_ref[rows, cols[h]].astype(BF16)
            cum = cums[h]
            last = cum[c - 1:c, :]
            state = state_ref[h]
            o_inter = _dot_nt((q * jnp.exp(cum)).astype(BF16), state.astype(BF16))
            k_dec = k * jnp.exp(last - cum)
            new_state = state * jnp.exp(last) + _dot_tn(v, k_dec.astype(BF16))

            pieces = [jnp.zeros((sub, c), F32)]
            for blk in range(1, nsub):
                lo = blk * sub
                ref = cum[lo:lo + 1, :]
                qs = q[lo:lo + sub, :] * jnp.exp(cum[lo:lo + sub, :] - ref)
                ks = k * jnp.exp(jnp.minimum(ref - cum, 0.0))
                pieces.append(_dot_nt(qs.astype(BF16), ks.astype(BF16)))
            a_off = jnp.concatenate(pieces, axis=0)

            q3 = q.reshape(nsub, sub, LANES)
            k3 = k.reshape(nsub, sub, LANES)
            c3 = cum.reshape(nsub, sub, LANES)
            a3 = jnp.zeros((nsub, sub, c), F32)
            for j in range(sub):
                dec = jnp.exp(jnp.minimum(c3 - c3[:, j:j + 1, :], 0.0))
                col = jnp.sum(q3 * k3[:, j:j + 1, :] * dec, axis=-1, keepdims=True)
                a3 = jnp.where(lane3 == blk3 * sub + j, col, a3)
            a_diag = a3.reshape(c, c)

            attn = jnp.where(off_mask, a_off, jnp.where(diag_mask, a_diag, 0.0))
            mids.append((o_inter, _dot(attn.astype(BF16), v), new_state))

        for h in range(heads):
            o_inter, o_intra, new_state = mids[h]
            state_ref[h] = new_state
            gate = gg_ref[rows, cols[h]]
            o_ref[rows, cols[h]] = (_rms(o_inter + o_intra, g) * (gate * jax.nn.sigmoid(gate))).astype(o_ref.dtype)
        return carry

    lax.fori_loop(0, q_ref.shape[0] // c, chunk, 0)


def _gla(gq, gk, la, gv, gg, norm_g, batch, seq):
    n, wd = gv.shape
    t = min(GLA_TILE, seq)
    nt = seq // t
    spec = pl.BlockSpec((t, wd), lambda b, i: (b * nt + i, 0))
    return pl.pallas_call(
        _gla_kernel,
        grid=(batch, nt),
        in_specs=[spec, spec, spec, spec, spec, pl.BlockSpec((1, GLA_DV), lambda b, i: (0, 0))],
        out_specs=spec,
        out_shape=jax.ShapeDtypeStruct((n, wd), BF16),
        scratch_shapes=[pltpu.VMEM((GLA_HEADS, GLA_DV, LANES), F32)],
        compiler_params=_cparams(("parallel", "arbitrary")),
        name="gla",
    )(gq, gk, la, gv, gg, norm_g.reshape(1, GLA_DV))


SUM_ROWS = 8


def _sb_kernel(q_ref, k_ref, vt_ref, o_ref, acc_ref):
    i = pl.program_id(2)
    tk = vt_ref.shape[2]
    nsub = q_ref.shape[0] // tk
    base = i * nsub
    r_s = lax.broadcasted_iota(jnp.int32, (tk + SUM_ROWS, tk), 0)
    c_s = lax.broadcasted_iota(jnp.int32, (tk + SUM_ROWS, tk), 1)
    suffix_mat = ((c_s > r_s) | (r_s >= tk)).astype(BF16)
    strict = (lax.broadcasted_iota(jnp.int32, (tk, tk), 0) < lax.broadcasted_iota(jnp.int32, (tk, tk), 1))

    def run_tiles(tiles, later):
        later = list(later)
        logits = [_dot_nt(kb, q_ref[a * tk:(a + 1) * tk, :]) for a, kb, _, _ in tiles]
        mids = []
        for (a, _, _, masked), z in zip(tiles, logits):
            log_term = jnp.log2(1.0 + jnp.exp2(-jnp.abs(z)))
            log_beta = jnp.minimum(z, 0.0) - log_term
            log_keep = log_beta - z
            if masked:
                log_keep = jnp.where(strict, log_keep, 0.0)
            mids.append((log_beta, _dot(suffix_mat, log_keep.astype(BF16))))
        pending = []
        for (a, _, vtb, masked), (log_beta, sums) in zip(tiles, mids):
            arg = log_beta + sums[:tk]
            if later[a] is not None:
                arg = arg + later[a]
            att = jnp.exp2(arg)
            if masked:
                att = jnp.where(strict, att, 0.0)
            pending.append((a, later[a] is None, _dot(vtb, att.astype(BF16))))
            later[a] = sums[tk:tk + 1] if later[a] is None else later[a] + sums[tk:tk + 1]
        for a, first, contrib in pending:
            if first:
                acc_ref[a] = contrib
            else:
                acc_ref[a] += contrib
        return tuple(later)

    def key_block(j):
        return k_ref[pl.ds(pl.multiple_of(j * tk, tk), tk), :], vt_ref[j]

    def retire(later_a, still_valid):
        return jnp.where(still_valid, later_a, -jnp.inf)

    def lockstep_tiles(offset, later):
        tiles, kept = [], []
        for a in range(nsub):
            j = base + a - offset
            kb, vtb = key_block(jnp.maximum(j, 0))
            tiles.append((a, kb, vtb, False))
            kept.append(retire(later[a], j >= 0))
        return tiles, kept

    diag = []
    for a in range(nsub):
        kb, vtb = key_block(base + a)
        diag.append((a, kb, vtb, True))
    later = run_tiles(diag, [None] * nsub)
    later = run_tiles(*lockstep_tiles(1, later))

    def alive(later):
        return (jnp.max(functools.reduce(jnp.maximum, later)) > LOG2_FLUSH).astype(jnp.int32)

    def cond(carry):
        step, live, _ = carry
        return (step < base + nsub - 2) & (live > 0)

    def body(carry):
        step, _, later = carry
        later = run_tiles(*lockstep_tiles(step + 2, later))
        return step + 1, alive(later), later

    lax.while_loop(cond, body, (jnp.int32(0), alive(later), later))
    for a in range(nsub):
        o_ref[a * tk:(a + 1) * tk, :] = acc_ref[a].T.astype(o_ref.dtype)


def _stick_breaking(q, k, vt, batch, seq):
    n, d = q.shape
    tk = vt.shape[2]
    tq = min(SB_QSUB * tk, seq)
    nq = seq // tq
    qspec = pl.BlockSpec((tq, LANES), lambda b, h, i: (b * nq + i, h))
    kspec = pl.BlockSpec((seq, LANES), lambda b, h, i: (b, h))
    vtspec = pl.BlockSpec((seq // tk, LANES, tk), lambda b, h, i: (b, h, 0))
    return pl.pallas_call(
        _sb_kernel,
        grid=(batch, SB_HEADS, nq),
        in_specs=[qspec, kspec, vtspec],
        out_specs=qspec,
        out_shape=jax.ShapeDtypeStruct((n, d), BF16),
        scratch_shapes=[pltpu.VMEM((tq // tk, LANES, tk), F32)],
        compiler_params=_cparams(("parallel", "parallel", "arbitrary")),
        name="stick_breaking",
    )(q, k, vt)


def kernel(x, ffn_pre_g, ffn_post_g, ffn_w_gate, ffn_w_up, ffn_w_down, mix_pre_g, mix_post_g, hyb_w_in, hyb_w_out, diff_lambda, diff_subln_g, gla_w_a2, gla_b_a, gla_norm_g, sb_w_qkv, sb_w_out):
    batch, seq, d = x.shape
    depth = ffn_pre_g.shape[0]
    assert d // SB_HEADS == LANES and DIFF_DV == LANES and GLA_DV == LANES
    h = x.reshape(batch * seq, d)

    def ffn(h, layer, j, mix=None):
        return _ffn(h, ffn_pre_g[layer, j], ffn_w_gate[layer, j], ffn_w_up[layer, j], ffn_w_down[layer, j],
                    ffn_post_g[layer, j], mix)

    for layer in range(depth):
        h = ffn(h, layer, 0)
        if layer % 2 == 0:
            e = layer // 2
            lambda_init = 0.8 - 0.6 * math.exp(-0.3 * layer)
            dq, dk, dvt, gq, gk, la, gv, gg = _proj_even(h, mix_pre_g[layer], hyb_w_in[e], gla_w_a2[e], gla_b_a[e])
            a_out = _diff_attention(dq, dk, dvt, diff_lambda[e], diff_subln_g[e], lambda_init, batch, seq)
            b_out = _gla(gq, gk, la, gv, gg, gla_norm_g[e], batch, seq)
            mix = (a_out, 0, b_out, 0, hyb_w_out[e], mix_post_g[layer])
        else:
            o = layer // 2
            q, k, vt = _proj_odd(h, mix_pre_g[layer], sb_w_qkv[o])
            att = _stick_breaking(q, k, vt, batch, seq)
            mix = (att, 0, att, 1, sb_w_out[o], mix_post_g[layer])
        h = ffn(h, layer, 1, mix)
    return h.reshape(batch, seq, d)
```

```python
import functools
import math

import jax
import jax.numpy as jnp
from jax import lax
from jax.experimental import pallas as pl
from jax.experimental.pallas import tpu as pltpu

F32 = jnp.float32
BF16 = jnp.bfloat16

EPS = 1e-6
DIFF_HEADS = 4
DIFF_DH = 64
DIFF_DV = 2 * DIFF_DH
GLA_HEADS = 4
GLA_DK = 64
GLA_DV = 128
GLA_RANK = 16
GLA_TAU = 16.0
GLA_CHUNK = 64
GLA_UNROLL = 4
GLA_SUB = 8
SB_HEADS = 8

LANES = 128
VMEM_LIMIT = 56 * 1024 * 1024

TOKEN_TILE = 512
ATTN_TILE = 256
ATTN_QSUB = 4
SB_QSUB = 8
DIFF_KV_UNROLL = 4
ONES_ROWS = 16
LOG2_FLUSH = -150.0
LOG2E = math.log2(math.e)
GLA_TILE = 512
FFN_PARTS = 2


def _cparams(sem):
    return pltpu.CompilerParams(dimension_semantics=sem, vmem_limit_bytes=VMEM_LIMIT)


def _rms(x, g):
    return x * lax.rsqrt(jnp.mean(x * x, axis=-1, keepdims=True) + EPS) * g


def _dot(a, b):
    return jnp.dot(a, b, preferred_element_type=F32)


def _dot_nt(a, b):
    return lax.dot_general(a, b, (((1,), (1,)), ((), ())), preferred_element_type=F32)


def _dot_tn(a, b):
    return lax.dot_general(a, b, (((0,), (0,)), ((), ())), preferred_element_type=F32)


def _split3(x):
    hi = x.astype(BF16)
    r1 = x - hi.astype(F32)
    mid = r1.astype(BF16)
    lo = (r1 - mid.astype(F32)).astype(BF16)
    return hi, mid, lo


def _softplus_parts(z):
    return jnp.maximum(z, 0.0), jnp.log(1.0 + jnp.exp(-jnp.abs(z)))


def _ffn_body(hs, gpre_ref, wgu_ref, wd_ref, gpost_ref):
    dff = wd_ref.shape[0]
    xns = [_rms(h, gpre_ref[...]).astype(BF16) for h in hs]
    acts = []
    for xn in xns:
        gu = _dot(xn, wgu_ref[...])
        gate = gu[:, :dff]
        acts.append((gate * jax.nn.sigmoid(gate) * gu[:, dff:]).astype(BF16))
    fs = [_dot(act, wd_ref[...]) for act in acts]
    return [h + 0.5 * _rms(f, gpost_ref[...]) for h, f in zip(hs, fs)]


def _row_parts(tm):
    return [pl.ds(p * (tm // FFN_PARTS), tm // FFN_PARTS) for p in range(FFN_PARTS)]


def _ffn_kernel(h_ref, gpre_ref, wgu_ref, wd_ref, gpost_ref, o_ref):
    parts = _row_parts(h_ref.shape[0])
    outs = _ffn_body([h_ref[r, :] for r in parts], gpre_ref, wgu_ref, wd_ref, gpost_ref)
    for r, o in zip(parts, outs):
        o_ref[r, :] = o


def _mix_ffn_kernel(h_ref, a_ref, b_ref, wo_ref, gmix_ref, gpre_ref, wgu_ref, wd_ref, gpost_ref, o_ref):
    parts = _row_parts(h_ref.shape[0])
    half = a_ref.shape[1]
    ys = [_dot(a_ref[r, :], wo_ref[0:half, :]) + _dot(b_ref[r, :], wo_ref[half:, :]) for r in parts]
    hs = [h_ref[r, :] + _rms(y, gmix_ref[...]) for r, y in zip(parts, ys)]
    outs = _ffn_body(hs, gpre_ref, wgu_ref, wd_ref, gpost_ref)
    for r, o in zip(parts, outs):
        o_ref[r, :] = o


def _ffn(h, g_pre, wgu_all, wd_all, idx, g_post, mix=None):
    n, d = h.shape
    dff = wd_all.shape[1]
    tm = min(TOKEN_TILE, n)
    row = lambda i: (i, 0)
    fixed = lambda i: (0, 0)
    slab = lambda i: (idx, 0, 0)
    ffn_specs = [pl.BlockSpec((1, d), fixed), pl.BlockSpec((None, d, 2 * dff), slab),
                 pl.BlockSpec((None, dff, d), slab), pl.BlockSpec((1, d), fixed)]
    ffn_args = (g_pre.reshape(1, d), wgu_all, wd_all, g_post.reshape(1, d))
    if mix is None:
        body, specs, args, name = _ffn_kernel, [], (), "ffn"
    else:
        a, a_col, b, b_col, w_out, g_mix = mix
        half = w_out.shape[0] // 2
        body, name = _mix_ffn_kernel, "mix_ffn"
        specs = [pl.BlockSpec((tm, half), lambda i: (i, a_col)), pl.BlockSpec((tm, half), lambda i: (i, b_col)),
                 pl.BlockSpec(w_out.shape, fixed), pl.BlockSpec((1, d), fixed)]
        args = (a, b, w_out.astype(BF16), g_mix.reshape(1, d))
    return pl.pallas_call(
        body,
        grid=(n // tm,),
        in_specs=[pl.BlockSpec((tm, d), row)] + specs + ffn_specs,
        out_specs=pl.BlockSpec((tm, d), row),
        out_shape=jax.ShapeDtypeStruct((n, d), F32),
        compiler_params=_cparams(("parallel",)),
        name=name,
    )(h, *args, *ffn_args)


def _store_key_blocks(vt_ref, vt):
    tk = vt_ref.shape[2]
    for c in range(vt_ref.shape[0]):
        vt_ref[c] = vt[:, c * tk:(c + 1) * tk].astype(vt_ref.dtype)


def _proj_even_kernel(h_ref, g_ref, w_ref, wvt_ref, ones_ref, wga_ref, wa2_ref, ba_ref,
                      dq_ref, dk_ref, dvt_ref, gq_ref, gk_ref, la_ref, gv_ref, gg_ref):
    wd = dq_ref.shape[1]
    tk = dvt_ref.shape[2]
    w_hi, w_lo, _ = _split3(wa2_ref[...])
    parts = _row_parts(h_ref.shape[0])
    ms = [_rms(h_ref[r, :], g_ref[...]).astype(BF16) for r in parts]
    for p, (r, m) in enumerate(zip(parts, ms)):
        def seg(i):
            return _dot(m, w_ref[:, i * wd:(i + 1) * wd])

        dq_ref[r, :] = (seg(0) * (DIFF_DH ** -0.5 * LOG2E)).astype(BF16)
        dk_ref[r, :] = seg(1).astype(BF16)
        vt = _dot_nt(wvt_ref[...], m) + ones_ref[...]
        blocks = vt.shape[1] // tk
        for c in range(blocks):
            dvt_ref[p * blocks + c] = vt[:, c * tk:(c + 1) * tk].astype(dvt_ref.dtype)
        gq_ref[r, :] = seg(2) * GLA_DK ** -0.5
        gk_ref[r, :] = seg(3)
        gv_ref[r, :] = seg(4)
        gg_ref[r, :] = seg(5)
        ga_hi, ga_lo, _ = _split3(_dot(m, wga_ref[...]))
        x = _dot(ga_hi, w_hi) + _dot(ga_lo, w_hi) + _dot(ga_hi, w_lo) + ba_ref[...]
        relu_neg, log_term = _softplus_parts(-x)
        la_ref[r, :] = -(relu_neg + log_term) * (1.0 / GLA_TAU)


def _pad_heads(w, heads, dh):
    lead = w.shape[:-1]
    w = w.reshape(lead + (heads, dh))
    w = jnp.pad(w, [(0, 0)] * len(lead) + [(0, 0), (0, LANES - dh)])
    return w.reshape(lead + (heads * LANES,))


def _proj_even(h, g, w_in, w_a2, b_a):
    n, d = h.shape
    tm = min(TOKEN_TILE, n)
    wd = DIFF_HEADS * DIFF_DV
    c = 0
    parts = []
    for width in (wd, wd, wd, GLA_HEADS * GLA_DK, GLA_HEADS * GLA_DK, GLA_HEADS * GLA_DV, GLA_HEADS * GLA_DV):
        parts.append(w_in[:, c:c + width])
        c += width
    parts[3] = _pad_heads(parts[3], GLA_HEADS, GLA_DK)
    parts[4] = _pad_heads(parts[4], GLA_HEADS, GLA_DK)
    w_vt = parts.pop(2).T.reshape(DIFF_HEADS, DIFF_DV, d)
    w_vt = jnp.pad(w_vt, ((0, 0), (0, ONES_ROWS), (0, 0))).reshape(-1, d).astype(BF16)
    ones_col = jnp.pad(jnp.zeros((DIFF_HEADS, DIFF_DV, 1), F32), ((0, 0), (0, ONES_ROWS), (0, 0)),
                       constant_values=1.0).reshape(-1, 1)
    vt_rows = w_vt.shape[0]
    w_main = jnp.concatenate(parts, axis=1).astype(BF16)
    tk = min(ATTN_TILE, tm)
    assert (tm // FFN_PARTS) % tk == 0
    w_ga = jnp.pad(w_in[:, c:c + GLA_RANK], ((0, 0), (0, LANES - GLA_RANK))).astype(BF16)
    wa2 = jnp.pad(_pad_heads(w_a2, GLA_HEADS, GLA_DK), ((0, LANES - GLA_RANK), (0, 0)))
    ba = _pad_heads(b_a, GLA_HEADS, GLA_DK).reshape(1, wd)
    row = lambda i: (i, 0)
    fixed = lambda i: (0, 0)
    out_bf = jax.ShapeDtypeStruct((n, wd), BF16)
    out_f = jax.ShapeDtypeStruct((n, wd), F32)
    out_vt = jax.ShapeDtypeStruct((n // tk, vt_rows, tk), BF16)
    rowspec = pl.BlockSpec((tm, wd), row)
    vtspec = pl.BlockSpec((tm // tk, vt_rows, tk), lambda i: (i, 0, 0))
    return pl.pallas_call(
        _proj_even_kernel,
        grid=(n // tm,),
        in_specs=[pl.BlockSpec((tm, d), row), pl.BlockSpec((1, d), fixed),
                  pl.BlockSpec(w_main.shape, fixed), pl.BlockSpec(w_vt.shape, fixed),
                  pl.BlockSpec(ones_col.shape, fixed), pl.BlockSpec(w_ga.shape, fixed),
                  pl.BlockSpec(wa2.shape, fixed), pl.BlockSpec((1, wd), fixed)],
        out_specs=[rowspec, rowspec, vtspec] + [rowspec] * 5,
        out_shape=[out_bf, out_bf, out_vt, out_f, out_f, out_f, out_f, out_f],
        compiler_params=_cparams(("parallel",)),
        name="proj_even",
    )(h, g.reshape(1, d), w_main, w_vt, ones_col, w_ga, wa2, ba)


def _proj_odd_kernel(h_ref, g_ref, w_ref, wvt_ref, q_ref, k_ref, vt_ref):
    m = _rms(h_ref[...], g_ref[...]).astype(BF16)
    d = q_ref.shape[1]
    dh = d // SB_HEADS
    q_ref[...] = (_dot(m, w_ref[:, 0:d]) * (dh ** -0.5 * LOG2E)).astype(BF16)
    k_ref[...] = _dot(m, w_ref[:, d:2 * d]).astype(BF16)
    _store_key_blocks(vt_ref, _dot_nt(wvt_ref[...], m))


def _proj_odd(h, g, w_qkv):
    n, d = h.shape
    tm = min(TOKEN_TILE, n)
    tk = min(ATTN_TILE, tm)
    row = lambda i: (i, 0)
    fixed = lambda i: (0, 0)
    out = jax.ShapeDtypeStruct((n, d), BF16)
    w_qk = w_qkv[:, :2 * d].astype(BF16)
    w_vt = w_qkv[:, 2 * d:].T.astype(BF16)
    return pl.pallas_call(
        _proj_odd_kernel,
        grid=(n // tm,),
        in_specs=[pl.BlockSpec((tm, d), row), pl.BlockSpec((1, d), fixed),
                  pl.BlockSpec(w_qk.shape, fixed), pl.BlockSpec(w_vt.shape, fixed)],
        out_specs=[pl.BlockSpec((tm, d), row), pl.BlockSpec((tm, d), row),
                   pl.BlockSpec((tm // tk, d, tk), lambda i: (i, 0, 0))],
        out_shape=[out, out, jax.ShapeDtypeStruct((n // tk, d, tk), BF16)],
        compiler_params=_cparams(("parallel",)),
        name="proj_odd",
    )(h, g.reshape(1, d), w_qk, w_vt)


def _diff_kernel(lam_ref, g_ref, q_ref, k_ref, vt_ref, o_ref, m_ref, l_ref, acc_ref, *, lambda_init):
    i = pl.program_id(2)
    tk = vt_ref.shape[2]
    dv = DIFF_DV
    nsub = q_ref.shape[0] // tk
    base = i * nsub
    lane = lax.broadcasted_iota(jnp.int32, (tk, LANES), 1)
    causal = (lax.broadcasted_iota(jnp.int32, (tk, tk), 0) <= lax.broadcasted_iota(jnp.int32, (tk, tk), 1))

    def q_map(a, mi):
        q = q_ref[a * tk:(a + 1) * tk, :]
        keep = (lane < DIFF_DH) if mi == 0 else (lane >= DIFF_DH)
        return jnp.where(keep, q, jnp.zeros_like(q))

    def run_tiles(tiles):
        scores = [_dot_nt(kb, q_map(a, mi)) for a, mi, kb, _, _, _ in tiles]
        pending = []
        for (a, mi, _, vtb, first, masked), s in zip(tiles, scores):
            if masked:
                s = jnp.where(causal, s, -jnp.inf)
            m_blk = jnp.max(s, axis=0, keepdims=True)
            if first:
                alpha = None
                m_new = m_blk
            else:
                m_prev = m_ref[a, mi]
                m_new = jnp.maximum(m_prev, m_blk)
                alpha = jnp.exp2(m_prev - m_new)
            m_ref[a, mi] = m_new
            p = jnp.exp2(s - m_new).astype(BF16)
            pending.append((a, mi, alpha, _dot(vtb, p)))
        for a, mi, alpha, pv in pending:
            if alpha is None:
                acc_ref[a, mi] = pv[:dv]
                l_ref[a, mi] = pv[dv:dv + 1]
            else:
                acc_ref[a, mi] = alpha * acc_ref[a, mi] + pv[:dv]
                l_ref[a, mi] = alpha * l_ref[a, mi] + pv[dv:dv + 1]

    def key_block(j):
        return k_ref[pl.ds(pl.multiple_of(j * tk, tk), tk), :], vt_ref[j]

    band = []
    for c in range(nsub):
        kb, vtb = key_block(base + c)
        band += [(a, mi, kb, vtb, c == 0, a == c) for a in range(c, nsub) for mi in range(2)]
    run_tiles(band)

    kv_unroll = DIFF_KV_UNROLL if nsub % DIFF_KV_UNROLL == 0 else 1

    def body(step, carry):
        tiles = []
        for u in range(kv_unroll):
            kb, vtb = key_block(step * kv_unroll + u)
            tiles += [(a, mi, kb, vtb, False, False) for a in range(nsub) for mi in range(2)]
        run_tiles(tiles)
        return carry

    lax.fori_loop(0, base // kv_unroll, body, 0)

    lp = lam_ref[...]
    lam = (jnp.exp(jnp.sum(lp[0:1] * lp[1:2], axis=-1, keepdims=True))
           - jnp.exp(jnp.sum(lp[2:3] * lp[3:4], axis=-1, keepdims=True)) + lambda_init)
    for a in range(nsub):
        o = acc_ref[a, 0] / l_ref[a, 0] - lam * (acc_ref[a, 1] / l_ref[a, 1])
        o = o * lax.rsqrt(jnp.mean(o * o, axis=0, keepdims=True) + EPS) * (1.0 - lambda_init)
        o_ref[a * tk:(a + 1) * tk, :] = (o.T * g_ref[...]).astype(o_ref.dtype)


def _diff_attention(dq, dk, dvt, lam_params, subln_g, lambda_init, batch, seq):
    n, wd = dq.shape
    tk = dvt.shape[2]
    tq = min(ATTN_QSUB * tk, seq)
    nq = seq // tq
    nsub = tq // tk
    qspec = pl.BlockSpec((tq, LANES), lambda b, h, i: (b * nq + i, h))
    kspec = pl.BlockSpec((seq, LANES), lambda b, h, i: (b, h))
    vtspec = pl.BlockSpec((seq // tk, DIFF_DV + ONES_ROWS, tk), lambda b, h, i: (b, h, 0))
    fixed = lambda b, h, i: (0, 0)
    return pl.pallas_call(
        functools.partial(_diff_kernel, lambda_init=lambda_init),
        grid=(batch, DIFF_HEADS, nq),
        in_specs=[pl.BlockSpec(lam_params.shape, fixed), pl.BlockSpec((1, DIFF_DV), fixed), qspec, kspec, vtspec],
        out_specs=qspec,
        out_shape=jax.ShapeDtypeStruct((n, wd), BF16),
        scratch_shapes=[pltpu.VMEM((nsub, 2, 1, tk), F32), pltpu.VMEM((nsub, 2, 1, tk), F32),
                        pltpu.VMEM((nsub, 2, DIFF_DV, tk), F32)],
        compiler_params=_cparams(("parallel", "parallel", "arbitrary")),
        name="diff_attn",
    )(lam_params, subln_g.reshape(1, DIFF_DV), dq, dk, dvt)


def _gla_kernel(q_ref, k_ref, la_ref, v_ref, gg_ref, g_ref, o_ref, state_ref):
    c = GLA_CHUNK
    sub = GLA_SUB
    nsub = c // sub
    heads = state_ref.shape[0]

    @pl.when(pl.program_id(1) == 0)
    def _():
        state_ref[...] = jnp.zeros(state_ref.shape, F32)

    r_i = lax.broadcasted_iota(jnp.int32, (c, c), 0)
    c_i = lax.broadcasted_iota(jnp.int32, (c, c), 1)
    tril = (c_i <= r_i).astype(BF16)
    levels = []
    s = c // 2
    while s >= sub:
        levels.append((s, (r_i // (2 * s) == c_i // (2 * s)) & (r_i % (2 * s) >= s) & (c_i % (2 * s) < s)))
        s //= 2
    diag_mask = (r_i // sub == c_i // sub) & (c_i <= r_i)
    lane3 = lax.broadcasted_iota(jnp.int32, (nsub, sub, c), 2)
    blk3 = lax.broadcasted_iota(jnp.int32, (nsub, sub, c), 0)
    g = g_ref[...]

    def step(si, carry):
        cols = [slice(h * LANES, (h + 1) * LANES) for h in range(heads)]
        rows = [pl.ds(pl.multiple_of((si * GLA_UNROLL + u) * c, c), c) for u in range(GLA_UNROLL)]

        cums = {}
        for u in range(GLA_UNROLL):
            for h in range(heads):
                la_hi, la_mid, la_lo = _split3(la_ref[rows[u], cols[h]])
                cums[u, h] = _dot(tril, la_hi) + _dot(tril, la_mid) + _dot(tril, la_lo)

        states = [state_ref[h] for h in range(heads)]
        mids = {}
        for u in range(GLA_UNROLL):
            for h in range(heads):
                q = q_ref[rows[u], cols[h]]
                k = k_ref[rows[u], cols[h]]
                v = v_ref[rows[u], cols[h]].astype(BF16)
                cum = cums[u, h]
                last = cum[c - 1:c, :]
                state = states[h]
                o_inter = _dot_nt((q * jnp.exp(cum)).astype(BF16), state.astype(BF16))
                k_dec = k * jnp.exp(last - cum)
                states[h] = state * jnp.exp(last) + _dot_tn(v, k_dec.astype(BF16))

                a_levels = []
                for s, _ in levels:
                    ref = jnp.concatenate([jnp.broadcast_to(cum[lo + s:lo + s + 1, :], (2 * s, LANES))
                                           for lo in range(0, c, 2 * s)], axis=0)
                    qs = q * jnp.exp(jnp.minimum(cum - ref, 0.0))
                    ks = k * jnp.exp(jnp.minimum(ref - cum, 0.0))
                    a_levels.append(_dot_nt(qs.astype(BF16), ks.astype(BF16)))

                q3 = q.reshape(nsub, sub, LANES)
                k3 = k.reshape(nsub, sub, LANES)
                c3 = cum.reshape(nsub, sub, LANES)
                a3 = jnp.zeros((nsub, sub, c), F32)
                for j in range(sub):
                    dec = jnp.exp(jnp.minimum(c3 - c3[:, j:j + 1, :], 0.0))
                    col = jnp.sum(q3 * k3[:, j:j + 1, :] * dec, axis=-1, keepdims=True)
                    a3 = jnp.where(lane3 == blk3 * sub + j, col, a3)
                a_diag = a3.reshape(c, c)

                attn = jnp.where(diag_mask, a_diag, 0.0)
                for (_, mask), a_level in zip(levels, a_levels):
                    attn = jnp.where(mask, a_level, attn)
                mids[u, h] = (o_inter, _dot(attn.astype(BF16), v))

        for h in range(heads):
            state_ref[h] = states[h]
        for u in range(GLA_UNROLL):
            for h in range(heads):
                o_inter, o_intra = mids[u, h]
                gate = gg_ref[rows[u], cols[h]]
                o_ref[rows[u], cols[h]] = (_rms(o_inter + o_intra, g)
                                           * (gate * jax.nn.sigmoid(gate))).astype(o_ref.dtype)
        return carry

    lax.fori_loop(0, q_ref.shape[0] // (c * GLA_UNROLL), step, 0)


def _gla(gq, gk, la, gv, gg, norm_g, batch, seq):
    n, wd = gv.shape
    t = min(GLA_TILE, seq)
    nt = seq // t
    spec = pl.BlockSpec((t, wd), lambda b, i: (b * nt + i, 0))
    return pl.pallas_call(
        _gla_kernel,
        grid=(batch, nt),
        in_specs=[spec, spec, spec, spec, spec, pl.BlockSpec((1, GLA_DV), lambda b, i: (0, 0))],
        out_specs=spec,
        out_shape=jax.ShapeDtypeStruct((n, wd), BF16),
        scratch_shapes=[pltpu.VMEM((GLA_HEADS, GLA_DV, LANES), F32)],
        compiler_params=_cparams(("parallel", "arbitrary")),
        name="gla",
    )(gq, gk, la, gv, gg, norm_g.reshape(1, GLA_DV))


SUM_ROWS = 8


def _sb_kernel(q_ref, k_ref, vt_ref, o_ref, acc_ref):
    i = pl.program_id(2)
    tk = vt_ref.shape[2]
    nsub = q_ref.shape[0] // tk
    base = i * nsub
    r_s = lax.broadcasted_iota(jnp.int32, (tk + SUM_ROWS, tk), 0)
    c_s = lax.broadcasted_iota(jnp.int32, (tk + SUM_ROWS, tk), 1)
    suffix_mat = ((c_s > r_s) | (r_s >= tk)).astype(BF16)
    strict = (lax.broadcasted_iota(jnp.int32, (tk, tk), 0) < lax.broadcasted_iota(jnp.int32, (tk, tk), 1))

    def run_tiles(tiles, later):
        later = list(later)
        logits = [_dot_nt(kb, q_ref[a * tk:(a + 1) * tk, :]) for a, kb, _, _ in tiles]
        mids = []
        for (a, _, _, masked), z in zip(tiles, logits):
            log_term = jnp.log2(1.0 + jnp.exp2(-jnp.abs(z)))
            log_beta = jnp.minimum(z, 0.0) - log_term
            log_keep = log_beta - z
            if masked:
                log_keep = jnp.where(strict, log_keep, 0.0)
            mids.append((log_beta, _dot(suffix_mat, log_keep.astype(BF16))))
        pending = []
        for (a, _, vtb, masked), (log_beta, sums) in zip(tiles, mids):
            arg = log_beta + sums[:tk]
            if later[a] is not None:
                arg = arg + later[a]
            att = jnp.exp2(arg)
            if masked:
                att = jnp.where(strict, att, 0.0)
            pending.append((a, later[a] is None, _dot(vtb, att.astype(BF16))))
            later[a] = sums[tk:tk + 1] if later[a] is None else later[a] + sums[tk:tk + 1]
        for a, first, contrib in pending:
            if first:
                acc_ref[a] = contrib
            else:
                acc_ref[a] += contrib
        return tuple(later)

    def key_block(j):
        return k_ref[pl.ds(pl.multiple_of(j * tk, tk), tk), :], vt_ref[j]

    def retire(later_a, still_valid):
        return jnp.where(still_valid, later_a, -jnp.inf)

    def lockstep_tiles(offset, later):
        tiles, kept = [], []
        for a in range(nsub):
            j = base + a - offset
            kb, vtb = key_block(jnp.maximum(j, 0))
            tiles.append((a, kb, vtb, False))
            kept.append(retire(later[a], j >= 0))
        return tiles, kept

    diag = []
    for a in range(nsub):
        kb, vtb = key_block(base + a)
        diag.append((a, kb, vtb, True))
    later = run_tiles(diag, [None] * nsub)
    later = run_tiles(*lockstep_tiles(1, later))

    def alive(later):
        return (jnp.max(functools.reduce(jnp.maximum, later)) > LOG2_FLUSH).astype(jnp.int32)

    def cond(carry):
        step, live, _ = carry
        return (step < base + nsub - 2) & (live > 0)

    def body(carry):
        step, _, later = carry
        later = run_tiles(*lockstep_tiles(step + 2, later))
        return step + 1, alive(later), later

    lax.while_loop(cond, body, (jnp.int32(0), alive(later), later))
    for a in range(nsub):
        o_ref[a * tk:(a + 1) * tk, :] = acc_ref[a].T.astype(o_ref.dtype)


def _stick_breaking(q, k, vt, batch, seq):
    n, d = q.shape
    tk = vt.shape[2]
    tq = min(SB_QSUB * tk, seq)
    nq = seq // tq
    qspec = pl.BlockSpec((tq, LANES), lambda b, h, i: (b * nq + i, h))
    kspec = pl.BlockSpec((seq, LANES), lambda b, h, i: (b, h))
    vtspec = pl.BlockSpec((seq // tk, LANES, tk), lambda b, h, i: (b, h, 0))
    return pl.pallas_call(
        _sb_kernel,
        grid=(batch, SB_HEADS, nq),
        in_specs=[qspec, kspec, vtspec],
        out_specs=qspec,
        out_shape=jax.ShapeDtypeStruct((n, d), BF16),
        scratch_shapes=[pltpu.VMEM((tq // tk, LANES, tk), F32)],
        compiler_params=_cparams(("parallel", "parallel", "arbitrary")),
        name="stick_breaking",
    )(q, k, vt)


def kernel(x, ffn_pre_g, ffn_post_g, ffn_w_gate, ffn_w_up, ffn_w_down, mix_pre_g, mix_post_g, hyb_w_in, hyb_w_out, diff_lambda, diff_subln_g, gla_w_a2, gla_b_a, gla_norm_g, sb_w_qkv, sb_w_out):
    batch, seq, d = x.shape
    depth = ffn_pre_g.shape[0]
    assert d // SB_HEADS == LANES and DIFF_DV == LANES and GLA_DV == LANES
    h = x.reshape(batch * seq, d)

    n_ffn = ffn_w_gate.shape[1]
    dff = ffn_w_gate.shape[-1]
    wgu_all = jnp.concatenate([ffn_w_gate, ffn_w_up], axis=-1).astype(BF16).reshape(depth * n_ffn, d, 2 * dff)
    wd_all = ffn_w_down.astype(BF16).reshape(depth * n_ffn, dff, d)

    def ffn(h, layer, j, mix=None):
        return _ffn(h, ffn_pre_g[layer, j], wgu_all, wd_all, layer * n_ffn + j, ffn_post_g[layer, j], mix)

    for layer in range(depth):
        h = ffn(h, layer, 0)
        if layer % 2 == 0:
            e = layer // 2
            lambda_init = 0.8 - 0.6 * math.exp(-0.3 * layer)
            dq, dk, dvt, gq, gk, la, gv, gg = _proj_even(h, mix_pre_g[layer], hyb_w_in[e], gla_w_a2[e], gla_b_a[e])
            a_out = _diff_attention(dq, dk, dvt, diff_lambda[e], diff_subln_g[e], lambda_init, batch, seq)
            b_out = _gla(gq, gk, la, gv, gg, gla_norm_g[e], batch, seq)
            mix = (a_out, 0, b_out, 0, hyb_w_out[e], mix_post_g[layer])
        else:
            o = layer // 2
            q, k, vt = _proj_odd(h, mix_pre_g[layer], sb_w_qkv[o])
            att = _stick_breaking(q, k, vt, batch, seq)
            mix = (att, 0, att, 1, sb_w_out[o], mix_post_g[layer])
        h = ffn(h, layer, 1, mix)
    return h.reshape(batch, seq, d)
```

```python
import functools
import math

import jax
import jax.numpy as jnp
from jax import lax
from jax.experimental import pallas as pl
from jax.experimental.pallas import tpu as pltpu

F32 = jnp.float32
BF16 = jnp.bfloat16

EPS = 1e-6
DIFF_HEADS = 4
DIFF_DH = 64
DIFF_DV = 2 * DIFF_DH
GLA_HEADS = 4
GLA_DK = 64
GLA_DV = 128
GLA_RANK = 16
GLA_TAU = 16.0
GLA_CHUNK = 64
GLA_UNROLL = 4
GLA_SUB = 8
SB_HEADS = 8

LANES = 128
VMEM_LIMIT = 56 * 1024 * 1024

TOKEN_TILE = 512
ATTN_TILE = 256
ATTN_QSUB = 4
SB_QSUB = 8
DIFF_KV_UNROLL = 4
ONES_ROWS = 16
LOG2_FLUSH = -150.0
LOG2E = math.log2(math.e)
GLA_TILE = 512
FFN_TILE = 1024
ROW_PART = 256


def _cparams(sem):
    return pltpu.CompilerParams(dimension_semantics=sem, vmem_limit_bytes=VMEM_LIMIT)


def _picked(stacked, idx):
    return stacked, pl.BlockSpec((None,) + stacked.shape[1:], lambda *_: (idx,) + (0,) * (stacked.ndim - 1))


def _rms(x, g):
    return x * lax.rsqrt(jnp.mean(x * x, axis=-1, keepdims=True) + EPS) * g


def _dot(a, b):
    return jnp.dot(a, b, preferred_element_type=F32)


def _dot_nt(a, b):
    return lax.dot_general(a, b, (((1,), (1,)), ((), ())), preferred_element_type=F32)


def _dot_tn(a, b):
    return lax.dot_general(a, b, (((0,), (0,)), ((), ())), preferred_element_type=F32)


def _split3(x):
    hi = x.astype(BF16)
    r1 = x - hi.astype(F32)
    mid = r1.astype(BF16)
    lo = (r1 - mid.astype(F32)).astype(BF16)
    return hi, mid, lo


def _softplus_parts(z):
    return jnp.maximum(z, 0.0), jnp.log(1.0 + jnp.exp(-jnp.abs(z)))


def _ffn_body(hs, gpre_ref, wgu_ref, wd_ref, gpost_ref):
    dff = wd_ref.shape[0]
    xns = [_rms(h, gpre_ref[...]).astype(BF16) for h in hs]
    acts = []
    for xn in xns:
        gu = _dot(xn, wgu_ref[...])
        gate = gu[:, :dff]
        acts.append((gate * jax.nn.sigmoid(gate) * gu[:, dff:]).astype(BF16))
    fs = [_dot(act, wd_ref[...]) for act in acts]
    return [h + 0.5 * _rms(f, gpost_ref[...]) for h, f in zip(hs, fs)]


def _row_parts(tm):
    part = min(ROW_PART, tm)
    return [pl.ds(p * part, part) for p in range(tm // part)]


def _ffn_kernel(h_ref, gpre_ref, wgu_ref, wd_ref, gpost_ref, o_ref):
    parts = _row_parts(h_ref.shape[0])
    outs = _ffn_body([h_ref[r, :] for r in parts], gpre_ref, wgu_ref, wd_ref, gpost_ref)
    for r, o in zip(parts, outs):
        o_ref[r, :] = o


def _mix_ffn_kernel(h_ref, a_ref, b_ref, wo_ref, gmix_ref, gpre_ref, wgu_ref, wd_ref, gpost_ref, o_ref):
    parts = _row_parts(h_ref.shape[0])
    half = a_ref.shape[1]
    ys = [_dot(a_ref[r, :], wo_ref[0:half, :]) + _dot(b_ref[r, :], wo_ref[half:, :]) for r in parts]
    hs = [h_ref[r, :] + _rms(y, gmix_ref[...]) for r, y in zip(parts, ys)]
    outs = _ffn_body(hs, gpre_ref, wgu_ref, wd_ref, gpost_ref)
    for r, o in zip(parts, outs):
        o_ref[r, :] = o


def _ffn(h, g_pre, wgu_all, wd_all, idx, g_post, mix=None):
    n, d = h.shape
    dff = wd_all.shape[1]
    tm = min(FFN_TILE, n)
    row = lambda i: (i, 0)
    g_pre, g_pre_spec = _picked(*g_pre)
    g_post, g_post_spec = _picked(*g_post)
    wgu_all, wgu_spec = _picked(wgu_all, idx)
    wd_all, wd_spec = _picked(wd_all, idx)
    ffn_specs = [g_pre_spec, wgu_spec, wd_spec, g_post_spec]
    ffn_args = (g_pre, wgu_all, wd_all, g_post)
    if mix is None:
        body, specs, args, name = _ffn_kernel, [], (), "ffn"
    else:
        a, a_col, b, b_col, w_out, g_mix = mix
        w_out, w_out_spec = _picked(*w_out)
        g_mix, g_mix_spec = _picked(*g_mix)
        half = w_out.shape[1] // 2
        body, name = _mix_ffn_kernel, "mix_ffn"
        specs = [pl.BlockSpec((tm, half), lambda i: (i, a_col)), pl.BlockSpec((tm, half), lambda i: (i, b_col)),
                 w_out_spec, g_mix_spec]
        args = (a, b, w_out, g_mix)
    return pl.pallas_call(
        body,
        grid=(n // tm,),
        in_specs=[pl.BlockSpec((tm, d), row)] + specs + ffn_specs,
        out_specs=pl.BlockSpec((tm, d), row),
        out_shape=jax.ShapeDtypeStruct((n, d), F32),
        compiler_params=_cparams(("parallel",)),
        name=name,
    )(h, *args, *ffn_args)


def _store_key_blocks(vt_ref, vt):
    tk = vt_ref.shape[2]
    for c in range(vt_ref.shape[0]):
        vt_ref[c] = vt[:, c * tk:(c + 1) * tk].astype(vt_ref.dtype)


def _proj_even_kernel(h_ref, g_ref, w_ref, wvt_ref, ones_ref, wga_ref, wa2_ref, ba_ref,
                      dq_ref, dk_ref, dvt_ref, gq_ref, gk_ref, la_ref, gv_ref, gg_ref):
    wd = dq_ref.shape[1]
    tk = dvt_ref.shape[2]
    w_hi, w_lo, _ = _split3(wa2_ref[...])
    parts = _row_parts(h_ref.shape[0])
    ms = [_rms(h_ref[r, :], g_ref[...]).astype(BF16) for r in parts]
    for p, (r, m) in enumerate(zip(parts, ms)):
        def seg(i):
            return _dot(m, w_ref[:, i * wd:(i + 1) * wd])

        dq_ref[r, :] = (seg(0) * (DIFF_DH ** -0.5 * LOG2E)).astype(BF16)
        dk_ref[r, :] = seg(1).astype(BF16)
        vt = _dot_nt(wvt_ref[...], m) + ones_ref[...]
        blocks = vt.shape[1] // tk
        for c in range(blocks):
            dvt_ref[p * blocks + c] = vt[:, c * tk:(c + 1) * tk].astype(dvt_ref.dtype)
        gq_ref[r, :] = seg(2) * GLA_DK ** -0.5
        gk_ref[r, :] = seg(3)
        gv_ref[r, :] = seg(4)
        gg_ref[r, :] = seg(5)
        ga_hi, ga_lo, _ = _split3(_dot(m, wga_ref[...]))
        x = _dot(ga_hi, w_hi) + _dot(ga_lo, w_hi) + _dot(ga_hi, w_lo) + ba_ref[...]
        relu_neg, log_term = _softplus_parts(-x)
        la_ref[r, :] = -(relu_neg + log_term) * (1.0 / GLA_TAU)


def _pad_heads(w, heads, dh):
    lead = w.shape[:-1]
    w = w.reshape(lead + (heads, dh))
    w = jnp.pad(w, [(0, 0)] * len(lead) + [(0, 0), (0, LANES - dh)])
    return w.reshape(lead + (heads * LANES,))


def _proj_even(h, g, w_in, w_a2, b_a):
    n, d = h.shape
    g, g_spec = _picked(*g)
    tm = min(TOKEN_TILE, n)
    wd = DIFF_HEADS * DIFF_DV
    c = 0
    parts = []
    for width in (wd, wd, wd, GLA_HEADS * GLA_DK, GLA_HEADS * GLA_DK, GLA_HEADS * GLA_DV, GLA_HEADS * GLA_DV):
        parts.append(w_in[:, c:c + width])
        c += width
    parts[3] = _pad_heads(parts[3], GLA_HEADS, GLA_DK)
    parts[4] = _pad_heads(parts[4], GLA_HEADS, GLA_DK)
    w_vt = parts.pop(2).T.reshape(DIFF_HEADS, DIFF_DV, d)
    w_vt = jnp.pad(w_vt, ((0, 0), (0, ONES_ROWS), (0, 0))).reshape(-1, d).astype(BF16)
    ones_col = jnp.pad(jnp.zeros((DIFF_HEADS, DIFF_DV, 1), F32), ((0, 0), (0, ONES_ROWS), (0, 0)),
                       constant_values=1.0).reshape(-1, 1)
    vt_rows = w_vt.shape[0]
    w_main = jnp.concatenate(parts, axis=1).astype(BF16)
    tk = min(ATTN_TILE, tm)
    assert min(ROW_PART, tm) % tk == 0
    w_ga = jnp.pad(w_in[:, c:c + GLA_RANK], ((0, 0), (0, LANES - GLA_RANK))).astype(BF16)
    wa2 = jnp.pad(_pad_heads(w_a2, GLA_HEADS, GLA_DK), ((0, LANES - GLA_RANK), (0, 0)))
    ba = _pad_heads(b_a, GLA_HEADS, GLA_DK).reshape(1, wd)
    row = lambda i: (i, 0)
    fixed = lambda i: (0, 0)
    out_bf = jax.ShapeDtypeStruct((n, wd), BF16)
    out_f = jax.ShapeDtypeStruct((n, wd), F32)
    out_vt = jax.ShapeDtypeStruct((n // tk, vt_rows, tk), BF16)
    rowspec = pl.BlockSpec((tm, wd), row)
    vtspec = pl.BlockSpec((tm // tk, vt_rows, tk), lambda i: (i, 0, 0))
    return pl.pallas_call(
        _proj_even_kernel,
        grid=(n // tm,),
        in_specs=[pl.BlockSpec((tm, d), row), g_spec,
                  pl.BlockSpec(w_main.shape, fixed), pl.BlockSpec(w_vt.shape, fixed),
                  pl.BlockSpec(ones_col.shape, fixed), pl.BlockSpec(w_ga.shape, fixed),
                  pl.BlockSpec(wa2.shape, fixed), pl.BlockSpec((1, wd), fixed)],
        out_specs=[rowspec, rowspec, vtspec] + [rowspec] * 5,
        out_shape=[out_bf, out_bf, out_vt, out_f, out_f, out_f, out_f, out_f],
        compiler_params=_cparams(("parallel",)),
        name="proj_even",
    )(h, g, w_main, w_vt, ones_col, w_ga, wa2, ba)


def _proj_odd_kernel(h_ref, g_ref, w_ref, wvt_ref, q_ref, k_ref, vt_ref):
    m = _rms(h_ref[...], g_ref[...]).astype(BF16)
    d = q_ref.shape[1]
    dh = d // SB_HEADS
    q_ref[...] = (_dot(m, w_ref[:, 0:d]) * (dh ** -0.5 * LOG2E)).astype(BF16)
    k_ref[...] = _dot(m, w_ref[:, d:2 * d]).astype(BF16)
    _store_key_blocks(vt_ref, _dot_nt(wvt_ref[...], m))


def _proj_odd(h, g, w_qkv):
    n, d = h.shape
    g, g_spec = _picked(*g)
    tm = min(TOKEN_TILE, n)
    tk = min(ATTN_TILE, tm)
    row = lambda i: (i, 0)
    fixed = lambda i: (0, 0)
    out = jax.ShapeDtypeStruct((n, d), BF16)
    w_qk = w_qkv[:, :2 * d].astype(BF16)
    w_vt = w_qkv[:, 2 * d:].T.astype(BF16)
    return pl.pallas_call(
        _proj_odd_kernel,
        grid=(n // tm,),
        in_specs=[pl.BlockSpec((tm, d), row), g_spec,
                  pl.BlockSpec(w_qk.shape, fixed), pl.BlockSpec(w_vt.shape, fixed)],
        out_specs=[pl.BlockSpec((tm, d), row), pl.BlockSpec((tm, d), row),
                   pl.BlockSpec((tm // tk, d, tk), lambda i: (i, 0, 0))],
        out_shape=[out, out, jax.ShapeDtypeStruct((n // tk, d, tk), BF16)],
        compiler_params=_cparams(("parallel",)),
        name="proj_odd",
    )(h, g, w_qk, w_vt)


def _diff_kernel(lam_ref, g_ref, q_ref, k_ref, vt_ref, o_ref, m_ref, l_ref, acc_ref, *, lambda_init):
    i = pl.program_id(2)
    tk = vt_ref.shape[2]
    dv = DIFF_DV
    nsub = q_ref.shape[0] // tk
    base = i * nsub
    lane = lax.broadcasted_iota(jnp.int32, (tk, LANES), 1)
    causal = (lax.broadcasted_iota(jnp.int32, (tk, tk), 0) <= lax.broadcasted_iota(jnp.int32, (tk, tk), 1))

    def q_map(a, mi):
        q = q_ref[a * tk:(a + 1) * tk, :]
        keep = (lane < DIFF_DH) if mi == 0 else (lane >= DIFF_DH)
        return jnp.where(keep, q, jnp.zeros_like(q))

    def run_tiles(tiles):
        scores = [_dot_nt(kb, q_map(a, mi)) for a, mi, kb, _, _, _ in tiles]
        pending = []
        for (a, mi, _, vtb, first, masked), s in zip(tiles, scores):
            if masked:
                s = jnp.where(causal, s, -jnp.inf)
            m_blk = jnp.max(s, axis=0, keepdims=True)
            if first:
                alpha = None
                m_new = m_blk
            else:
                m_prev = m_ref[a, mi]
                m_new = jnp.maximum(m_prev, m_blk)
                alpha = jnp.exp2(m_prev - m_new)
            m_ref[a, mi] = m_new
            p = jnp.exp2(s - m_new).astype(BF16)
            pending.append((a, mi, alpha, _dot(vtb, p)))
        for a, mi, alpha, pv in pending:
            if alpha is None:
                acc_ref[a, mi] = pv[:dv]
                l_ref[a, mi] = pv[dv:dv + 1]
            else:
                acc_ref[a, mi] = alpha * acc_ref[a, mi] + pv[:dv]
                l_ref[a, mi] = alpha * l_ref[a, mi] + pv[dv:dv + 1]

    def key_block(j):
        return k_ref[pl.ds(pl.multiple_of(j * tk, tk), tk), :], vt_ref[j]

    band = []
    for c in range(nsub):
        kb, vtb = key_block(base + c)
        band += [(a, mi, kb, vtb, c == 0, a == c) for a in range(c, nsub) for mi in range(2)]
    run_tiles(band)

    kv_unroll = DIFF_KV_UNROLL if nsub % DIFF_KV_UNROLL == 0 else 1

    def body(step, carry):
        tiles = []
        for u in range(kv_unroll):
            kb, vtb = key_block(step * kv_unroll + u)
            tiles += [(a, mi, kb, vtb, False, False) for a in range(nsub) for mi in range(2)]
        run_tiles(tiles)
        return carry

    lax.fori_loop(0, base // kv_unroll, body, 0)

    lp = lam_ref[...]
    lam = (jnp.exp(jnp.sum(lp[0:1] * lp[1:2], axis=-1, keepdims=True))
           - jnp.exp(jnp.sum(lp[2:3] * lp[3:4], axis=-1, keepdims=True)) + lambda_init)
    for a in range(nsub):
        o = acc_ref[a, 0] / l_ref[a, 0] - lam * (acc_ref[a, 1] / l_ref[a, 1])
        o = o * lax.rsqrt(jnp.mean(o * o, axis=0, keepdims=True) + EPS) * (1.0 - lambda_init)
        o_ref[a * tk:(a + 1) * tk, :] = (o.T * g_ref[...]).astype(o_ref.dtype)


def _diff_attention(dq, dk, dvt, lam_params, subln_g, lambda_init, batch, seq):
    n, wd = dq.shape
    lam_params, lam_spec = _picked(*lam_params)
    subln_g, g_spec = _picked(*subln_g)
    tk = dvt.shape[2]
    tq = min(ATTN_QSUB * tk, seq)
    nq = seq // tq
    nsub = tq // tk
    qspec = pl.BlockSpec((tq, LANES), lambda b, h, i: (b * nq + i, h))
    kspec = pl.BlockSpec((seq, LANES), lambda b, h, i: (b, h))
    vtspec = pl.BlockSpec((seq // tk, DIFF_DV + ONES_ROWS, tk), lambda b, h, i: (b, h, 0))
    return pl.pallas_call(
        functools.partial(_diff_kernel, lambda_init=lambda_init),
        grid=(batch, DIFF_HEADS, nq),
        in_specs=[lam_spec, g_spec, qspec, kspec, vtspec],
        out_specs=qspec,
        out_shape=jax.ShapeDtypeStruct((n, wd), BF16),
        scratch_shapes=[pltpu.VMEM((nsub, 2, 1, tk), F32), pltpu.VMEM((nsub, 2, 1, tk), F32),
                        pltpu.VMEM((nsub, 2, DIFF_DV, tk), F32)],
        compiler_params=_cparams(("parallel", "parallel", "arbitrary")),
        name="diff_attn",
    )(lam_params, subln_g, dq, dk, dvt)


def _gla_kernel(q_ref, k_ref, la_ref, v_ref, gg_ref, g_ref, o_ref, state_ref):
    c = GLA_CHUNK
    sub = GLA_SUB
    nsub = c // sub
    heads = state_ref.shape[0]

    @pl.when(pl.program_id(1) == 0)
    def _():
        state_ref[...] = jnp.zeros(state_ref.shape, F32)

    r_i = lax.broadcasted_iota(jnp.int32, (c, c), 0)
    c_i = lax.broadcasted_iota(jnp.int32, (c, c), 1)
    tril = (c_i <= r_i).astype(BF16)
    levels = []
    s = c // 2
    while s >= sub:
        levels.append((s, (r_i // (2 * s) == c_i // (2 * s)) & (r_i % (2 * s) >= s) & (c_i % (2 * s) < s)))
        s //= 2
    diag_mask = (r_i // sub == c_i // sub) & (c_i <= r_i)
    lane3 = lax.broadcasted_iota(jnp.int32, (nsub, sub, c), 2)
    blk3 = lax.broadcasted_iota(jnp.int32, (nsub, sub, c), 0)
    g = g_ref[...]

    def step(si, carry):
        cols = [slice(h * LANES, (h + 1) * LANES) for h in range(heads)]
        rows = [pl.ds(pl.multiple_of((si * GLA_UNROLL + u) * c, c), c) for u in range(GLA_UNROLL)]

        cums = {}
        for u in range(GLA_UNROLL):
            for h in range(heads):
                la_hi, la_mid, la_lo = _split3(la_ref[rows[u], cols[h]])
                cums[u, h] = _dot(tril, la_hi) + _dot(tril, la_mid) + _dot(tril, la_lo)

        states = [state_ref[h] for h in range(heads)]
        mids = {}
        for u in range(GLA_UNROLL):
            for h in range(heads):
                q = q_ref[rows[u], cols[h]]
                k = k_ref[rows[u], cols[h]]
                v = v_ref[rows[u], cols[h]].astype(BF16)
                cum = cums[u, h]
                last = cum[c - 1:c, :]
                state = states[h]
                o_inter = _dot_nt((q * jnp.exp(cum)).astype(BF16), state.astype(BF16))
                k_dec = k * jnp.exp(last - cum)
                states[h] = state * jnp.exp(last) + _dot_tn(v, k_dec.astype(BF16))

                a_levels = []
                for s, _ in levels:
                    ref = jnp.concatenate([jnp.broadcast_to(cum[lo + s:lo + s + 1, :], (2 * s, LANES))
                                           for lo in range(0, c, 2 * s)], axis=0)
                    qs = q * jnp.exp(jnp.minimum(cum - ref, 0.0))
                    ks = k * jnp.exp(jnp.minimum(ref - cum, 0.0))
                    a_levels.append(_dot_nt(qs.astype(BF16), ks.astype(BF16)))

                q3 = q.reshape(nsub, sub, LANES)
                k3 = k.reshape(nsub, sub, LANES)
                c3 = cum.reshape(nsub, sub, LANES)
                a3 = jnp.zeros((nsub, sub, c), F32)
                for j in range(sub):
                    dec = jnp.exp(jnp.minimum(c3 - c3[:, j:j + 1, :], 0.0))
                    col = jnp.sum(q3 * k3[:, j:j + 1, :] * dec, axis=-1, keepdims=True)
                    a3 = jnp.where(lane3 == blk3 * sub + j, col, a3)
                a_diag = a3.reshape(c, c)

                attn = jnp.where(diag_mask, a_diag, 0.0)
                for (_, mask), a_level in zip(levels, a_levels):
                    attn = jnp.where(mask, a_level, attn)
                mids[u, h] = (o_inter, _dot(attn.astype(BF16), v))

        for h in range(heads):
            state_ref[h] = states[h]
        for u in range(GLA_UNROLL):
            for h in range(heads):
                o_inter, o_intra = mids[u, h]
                gate = gg_ref[rows[u], cols[h]]
                o_ref[rows[u], cols[h]] = (_rms(o_inter + o_intra, g)
                                           * (gate * jax.nn.sigmoid(gate))).astype(o_ref.dtype)
        return carry

    lax.fori_loop(0, q_ref.shape[0] // (c * GLA_UNROLL), step, 0)


def _gla(gq, gk, la, gv, gg, norm_g, batch, seq):
    n, wd = gv.shape
    norm_g, g_spec = _picked(*norm_g)
    t = min(GLA_TILE, seq)
    nt = seq // t
    spec = pl.BlockSpec((t, wd), lambda b, i: (b * nt + i, 0))
    return pl.pallas_call(
        _gla_kernel,
        grid=(batch, nt),
        in_specs=[spec, spec, spec, spec, spec, g_spec],
        out_specs=spec,
        out_shape=jax.ShapeDtypeStruct((n, wd), BF16),
        scratch_shapes=[pltpu.VMEM((GLA_HEADS, GLA_DV, LANES), F32)],
        compiler_params=_cparams(("parallel", "arbitrary")),
        name="gla",
    )(gq, gk, la, gv, gg, norm_g)


SUM_ROWS = 8


def _sb_kernel(q_ref, k_ref, vt_ref, o_ref, acc_ref):
    i = pl.program_id(2)
    tk = vt_ref.shape[2]
    nsub = q_ref.shape[0] // tk
    base = i * nsub
    r_s = lax.broadcasted_iota(jnp.int32, (tk + SUM_ROWS, tk), 0)
    c_s = lax.broadcasted_iota(jnp.int32, (tk + SUM_ROWS, tk), 1)
    suffix_mat = ((c_s > r_s) | (r_s >= tk)).astype(BF16)
    strict = (lax.broadcasted_iota(jnp.int32, (tk, tk), 0) < lax.broadcasted_iota(jnp.int32, (tk, tk), 1))

    def run_tiles(tiles, later):
        later = list(later)
        logits = [_dot_nt(kb, q_ref[a * tk:(a + 1) * tk, :]) for a, kb, _, _ in tiles]
        mids = []
        for (a, _, _, masked), z in zip(tiles, logits):
            log_term = jnp.log2(1.0 + jnp.exp2(-jnp.abs(z)))
            log_beta = jnp.minimum(z, 0.0) - log_term
            log_keep = log_beta - z
            if masked:
                log_keep = jnp.where(strict, log_keep, 0.0)
            mids.append((log_beta, _dot(suffix_mat, log_keep.astype(BF16))))
        pending = []
        for (a, _, vtb, masked), (log_beta, sums) in zip(tiles, mids):
            arg = log_beta + sums[:tk]
            if later[a] is not None:
                arg = arg + later[a]
            att = jnp.exp2(arg)
            if masked:
                att = jnp.where(strict, att, 0.0)
            pending.append((a, later[a] is None, _dot(vtb, att.astype(BF16))))
            later[a] = sums[tk:tk + 1] if later[a] is None else later[a] + sums[tk:tk + 1]
        for a, first, contrib in pending:
            if first:
                acc_ref[a] = contrib
            else:
                acc_ref[a] += contrib
        return tuple(later)

    def key_block(j):
        return k_ref[pl.ds(pl.multiple_of(j * tk, tk), tk), :], vt_ref[j]

    def retire(later_a, still_valid):
        return jnp.where(still_valid, later_a, -jnp.inf)

    def lockstep_tiles(offset, later):
        tiles, kept = [], []
        for a in range(nsub):
            j = base + a - offset
            kb, vtb = key_block(jnp.maximum(j, 0))
            tiles.append((a, kb, vtb, False))
            kept.append(retire(later[a], j >= 0))
        return tiles, kept

    diag = []
    for a in range(nsub):
        kb, vtb = key_block(base + a)
        diag.append((a, kb, vtb, True))
    later = run_tiles(diag, [None] * nsub)
    later = run_tiles(*lockstep_tiles(1, later))

    def alive(later):
        return (jnp.max(functools.reduce(jnp.maximum, later)) > LOG2_FLUSH).astype(jnp.int32)

    def cond(carry):
        step, live, _ = carry
        return (step < base + nsub - 2) & (live > 0)

    def body(carry):
        step, _, later = carry
        later = run_tiles(*lockstep_tiles(step + 2, later))
        return step + 1, alive(later), later

    lax.while_loop(cond, body, (jnp.int32(0), alive(later), later))
    for a in range(nsub):
        o_ref[a * tk:(a + 1) * tk, :] = acc_ref[a].T.astype(o_ref.dtype)


def _stick_breaking(q, k, vt, batch, seq):
    n, d = q.shape
    tk = vt.shape[2]
    tq = min(SB_QSUB * tk, seq)
    nq = seq // tq
    qspec = pl.BlockSpec((tq, LANES), lambda b, h, i: (b * nq + i, h))
    kspec = pl.BlockSpec((seq, LANES), lambda b, h, i: (b, h))
    vtspec = pl.BlockSpec((seq // tk, LANES, tk), lambda b, h, i: (b, h, 0))
    return pl.pallas_call(
        _sb_kernel,
        grid=(batch, SB_HEADS, nq),
        in_specs=[qspec, kspec, vtspec],
        out_specs=qspec,
        out_shape=jax.ShapeDtypeStruct((n, d), BF16),
        scratch_shapes=[pltpu.VMEM((tq // tk, LANES, tk), F32)],
        compiler_params=_cparams(("parallel", "parallel", "arbitrary")),
        name="stick_breaking",
    )(q, k, vt)


def kernel(x, ffn_pre_g, ffn_post_g, ffn_w_gate, ffn_w_up, ffn_w_down, mix_pre_g, mix_post_g, hyb_w_in, hyb_w_out, diff_lambda, diff_subln_g, gla_w_a2, gla_b_a, gla_norm_g, sb_w_qkv, sb_w_out):
    batch, seq, d = x.shape
    depth = ffn_pre_g.shape[0]
    assert d // SB_HEADS == LANES and DIFF_DV == LANES and GLA_DV == LANES
    h = x.reshape(batch * seq, d)

    n_ffn = ffn_w_gate.shape[1]
    dff = ffn_w_gate.shape[-1]
    wgu_all = jnp.concatenate([ffn_w_gate, ffn_w_up], axis=-1).astype(BF16).reshape(depth * n_ffn, d, 2 * dff)
    wd_all = ffn_w_down.astype(BF16).reshape(depth * n_ffn, dff, d)

    ffn_pre = ffn_pre_g.reshape(depth * n_ffn, 1, d)
    ffn_post = ffn_post_g.reshape(depth * n_ffn, 1, d)
    mix_pre = mix_pre_g.reshape(depth, 1, d)
    mix_post = mix_post_g.reshape(depth, 1, d)
    subln = diff_subln_g.reshape(-1, 1, DIFF_DV)
    gla_g = gla_norm_g.reshape(-1, 1, GLA_DV)
    hyb_out = hyb_w_out.astype(BF16)
    sb_out = sb_w_out.astype(BF16)

    def ffn(h, layer, j, mix=None):
        idx = layer * n_ffn + j
        return _ffn(h, (ffn_pre, idx), wgu_all, wd_all, idx, (ffn_post, idx), mix)

    for layer in range(depth):
        h = ffn(h, layer, 0)
        if layer % 2 == 0:
            e = layer // 2
            lambda_init = 0.8 - 0.6 * math.exp(-0.3 * layer)
            dq, dk, dvt, gq, gk, la, gv, gg = _proj_even(h, (mix_pre, layer), hyb_w_in[e], gla_w_a2[e], gla_b_a[e])
            a_out = _diff_attention(dq, dk, dvt, (diff_lambda, e), (subln, e), lambda_init, batch, seq)
            b_out = _gla(gq, gk, la, gv, gg, (gla_g, e), batch, seq)
            mix = (a_out, 0, b_out, 0, (hyb_out, e), (mix_post, layer))
        else:
            o = layer // 2
            q, k, vt = _proj_odd(h, (mix_pre, layer), sb_w_qkv[o])
            att = _stick_breaking(q, k, vt, batch, seq)
            mix = (att, 0, att, 1, (sb_out, o), (mix_post, layer))
        h = ffn(h, layer, 1, mix)
    return h.reshape(batch, seq, d)
```

```python
import functools
import math

import jax
import jax.numpy as jnp
from jax import lax
from jax.experimental import pallas as pl
from jax.experimental.pallas import tpu as pltpu

F32 = jnp.float32
BF16 = jnp.bfloat16

EPS = 1e-6
DIFF_HEADS = 4
DIFF_DH = 64
DIFF_DV = 2 * DIFF_DH
GLA_HEADS = 4
GLA_DK = 64
GLA_DV = 128
GLA_RANK = 16
GLA_TAU = 16.0
GLA_CHUNK = 64
GLA_UNROLL = 8
GLA_SUB = 8
SB_HEADS = 8

LANES = 128
VMEM_LIMIT = 56 * 1024 * 1024

TOKEN_TILE = 512
ATTN_TILE = 256
ATTN_QSUB = 4
SB_QSUB = 8
DIFF_KV_UNROLL = 4
DIFF_KV_GROUP = 2
ONES_ROWS = 16
LOG2_FLUSH = -150.0
LOG2E = math.log2(math.e)
GLA_TILE = 512
FFN_TILE = 1024
ROW_PART = 256


def _cparams(sem):
    return pltpu.CompilerParams(dimension_semantics=sem, vmem_limit_bytes=VMEM_LIMIT)


def _picked(stacked, idx):
    return stacked, pl.BlockSpec((None,) + stacked.shape[1:], lambda *_: (idx,) + (0,) * (stacked.ndim - 1))


def _rms(x, g):
    return x * lax.rsqrt(jnp.mean(x * x, axis=-1, keepdims=True) + EPS) * g


def _dot(a, b):
    return jnp.dot(a, b, preferred_element_type=F32)


def _dot_nt(a, b):
    return lax.dot_general(a, b, (((1,), (1,)), ((), ())), preferred_element_type=F32)


def _dot_tn(a, b):
    return lax.dot_general(a, b, (((0,), (0,)), ((), ())), preferred_element_type=F32)


def _split3(x):
    hi = x.astype(BF16)
    r1 = x - hi.astype(F32)
    mid = r1.astype(BF16)
    lo = (r1 - mid.astype(F32)).astype(BF16)
    return hi, mid, lo


def _softplus_parts(z):
    return jnp.maximum(z, 0.0), jnp.log(1.0 + jnp.exp(-jnp.abs(z)))


def _ffn_body(hs, gpre_ref, wgu_ref, wd_ref, gpost_ref):
    dff = wd_ref.shape[0]
    xns = [_rms(h, gpre_ref[...]).astype(BF16) for h in hs]
    acts = []
    for xn in xns:
        gu = _dot(xn, wgu_ref[...])
        gate = gu[:, :dff]
        acts.append((gate * jax.nn.sigmoid(gate) * gu[:, dff:]).astype(BF16))
    fs = [_dot(act, wd_ref[...]) for act in acts]
    return [h + 0.5 * _rms(f, gpost_ref[...]) for h, f in zip(hs, fs)]


def _row_parts(tm):
    part = min(ROW_PART, tm)
    return [pl.ds(p * part, part) for p in range(tm // part)]


def _ffn_kernel(h_ref, gpre_ref, wgu_ref, wd_ref, gpost_ref, o_ref):
    parts = _row_parts(h_ref.shape[0])
    outs = _ffn_body([h_ref[r, :] for r in parts], gpre_ref, wgu_ref, wd_ref, gpost_ref)
    for r, o in zip(parts, outs):
        o_ref[r, :] = o


def _mix_ffn_kernel(h_ref, a_ref, b_ref, wo_ref, gmix_ref, gpre_ref, wgu_ref, wd_ref, gpost_ref, o_ref):
    parts = _row_parts(h_ref.shape[0])
    half = a_ref.shape[1]
    ys = [_dot(a_ref[r, :], wo_ref[0:half, :]) + _dot(b_ref[r, :], wo_ref[half:, :]) for r in parts]
    hs = [h_ref[r, :] + _rms(y, gmix_ref[...]) for r, y in zip(parts, ys)]
    outs = _ffn_body(hs, gpre_ref, wgu_ref, wd_ref, gpost_ref)
    for r, o in zip(parts, outs):
        o_ref[r, :] = o


def _ffn(h, g_pre, wgu_all, wd_all, idx, g_post, mix=None):
    n, d = h.shape
    dff = wd_all.shape[1]
    tm = min(FFN_TILE, n)
    row = lambda i: (i, 0)
    g_pre, g_pre_spec = _picked(*g_pre)
    g_post, g_post_spec = _picked(*g_post)
    wgu_all, wgu_spec = _picked(wgu_all, idx)
    wd_all, wd_spec = _picked(wd_all, idx)
    ffn_specs = [g_pre_spec, wgu_spec, wd_spec, g_post_spec]
    ffn_args = (g_pre, wgu_all, wd_all, g_post)
    if mix is None:
        body, specs, args, name = _ffn_kernel, [], (), "ffn"
    else:
        a, a_col, b, b_col, w_out, g_mix = mix
        w_out, w_out_spec = _picked(*w_out)
        g_mix, g_mix_spec = _picked(*g_mix)
        half = w_out.shape[1] // 2
        body, name = _mix_ffn_kernel, "mix_ffn"
        specs = [pl.BlockSpec((tm, half), lambda i: (i, a_col)), pl.BlockSpec((tm, half), lambda i: (i, b_col)),
                 w_out_spec, g_mix_spec]
        args = (a, b, w_out, g_mix)
    return pl.pallas_call(
        body,
        grid=(n // tm,),
        in_specs=[pl.BlockSpec((tm, d), row)] + specs + ffn_specs,
        out_specs=pl.BlockSpec((tm, d), row),
        out_shape=jax.ShapeDtypeStruct((n, d), F32),
        compiler_params=_cparams(("parallel",)),
        name=name,
    )(h, *args, *ffn_args)


def _store_key_blocks(vt_ref, vt):
    tk = vt_ref.shape[2]
    for c in range(vt_ref.shape[0]):
        vt_ref[c] = vt[:, c * tk:(c + 1) * tk].astype(vt_ref.dtype)


def _proj_even_kernel(h_ref, g_ref, w_ref, wvt_ref, ones_ref, wga_ref, wa2_ref, ba_ref,
                      dq_ref, dk_ref, dvt_ref, gq_ref, gk_ref, la_ref, gv_ref, gg_ref):
    tk = dvt_ref.shape[2]
    w_hi, w_lo, _ = _split3(wa2_ref[...])
    widths = [r.shape[1] for r in (dq_ref, dk_ref, gq_ref, gk_ref, gv_ref, gg_ref)]
    starts = [sum(widths[:i]) for i in range(len(widths))]
    parts = _row_parts(h_ref.shape[0])
    ms = [_rms(h_ref[r, :], g_ref[...]).astype(BF16) for r in parts]
    for p, (r, m) in enumerate(zip(parts, ms)):
        def seg(i):
            return _dot(m, w_ref[:, starts[i]:starts[i] + widths[i]])

        dq_ref[r, :] = (seg(0) * (DIFF_DH ** -0.5 * LOG2E)).astype(BF16)
        dk_ref[r, :] = seg(1).astype(BF16)
        vt = _dot_nt(wvt_ref[...], m) + ones_ref[...]
        blocks = vt.shape[1] // tk
        for c in range(blocks):
            dvt_ref[p * blocks + c] = vt[:, c * tk:(c + 1) * tk].astype(dvt_ref.dtype)
        gq_ref[r, :] = seg(2) * GLA_DK ** -0.5
        gk_ref[r, :] = seg(3)
        gv_ref[r, :] = seg(4).astype(BF16)
        gg_ref[r, :] = seg(5)
        ga_hi, ga_lo, _ = _split3(_dot(m, wga_ref[...]))
        x = _dot(ga_hi, w_hi) + _dot(ga_lo, w_hi) + _dot(ga_hi, w_lo) + ba_ref[...]
        relu_neg, log_term = _softplus_parts(-x)
        la_ref[r, :] = -(relu_neg + log_term) * (1.0 / GLA_TAU)


def _proj_even(h, g, w_in, w_a2, b_a):
    n, d = h.shape
    g, g_spec = _picked(*g)
    tm = min(TOKEN_TILE, n)
    wd = DIFF_HEADS * DIFF_DV
    wk = GLA_HEADS * GLA_DK
    c = 0
    parts = []
    for width in (wd, wd, wd, wk, wk, GLA_HEADS * GLA_DV, GLA_HEADS * GLA_DV):
        parts.append(w_in[:, c:c + width])
        c += width
    w_vt = parts.pop(2).T.reshape(DIFF_HEADS, DIFF_DV, d)
    w_vt = jnp.pad(w_vt, ((0, 0), (0, ONES_ROWS), (0, 0))).reshape(-1, d).astype(BF16)
    ones_col = jnp.pad(jnp.zeros((DIFF_HEADS, DIFF_DV, 1), F32), ((0, 0), (0, ONES_ROWS), (0, 0)),
                       constant_values=1.0).reshape(-1, 1)
    vt_rows = w_vt.shape[0]
    w_main = jnp.concatenate(parts, axis=1).astype(BF16)
    tk = min(ATTN_TILE, tm)
    assert min(ROW_PART, tm) % tk == 0
    w_ga = jnp.pad(w_in[:, c:c + GLA_RANK], ((0, 0), (0, LANES - GLA_RANK))).astype(BF16)
    wa2 = jnp.pad(w_a2, ((0, LANES - GLA_RANK), (0, 0)))
    ba = b_a.reshape(1, wk)
    row = lambda i: (i, 0)
    fixed = lambda i: (0, 0)
    out_bf = jax.ShapeDtypeStruct((n, wd), BF16)
    out_f = jax.ShapeDtypeStruct((n, wd), F32)
    out_k = jax.ShapeDtypeStruct((n, wk), F32)
    out_vt = jax.ShapeDtypeStruct((n // tk, vt_rows, tk), BF16)
    rowspec = pl.BlockSpec((tm, wd), row)
    kspec = pl.BlockSpec((tm, wk), row)
    vtspec = pl.BlockSpec((tm // tk, vt_rows, tk), lambda i: (i, 0, 0))
    return pl.pallas_call(
        _proj_even_kernel,
        grid=(n // tm,),
        in_specs=[pl.BlockSpec((tm, d), row), g_spec,
                  pl.BlockSpec(w_main.shape, fixed), pl.BlockSpec(w_vt.shape, fixed),
                  pl.BlockSpec(ones_col.shape, fixed), pl.BlockSpec(w_ga.shape, fixed),
                  pl.BlockSpec(wa2.shape, fixed), pl.BlockSpec((1, wk), fixed)],
        out_specs=[rowspec, rowspec, vtspec, kspec, kspec, kspec, rowspec, rowspec],
        out_shape=[out_bf, out_bf, out_vt, out_k, out_k, out_k, out_bf, out_f],
        compiler_params=_cparams(("parallel",)),
        name="proj_even",
    )(h, g, w_main, w_vt, ones_col, w_ga, wa2, ba)


def _proj_odd_kernel(h_ref, g_ref, w_ref, wvt_ref, q_ref, k_ref, vt_ref):
    m = _rms(h_ref[...], g_ref[...]).astype(BF16)
    d = q_ref.shape[1]
    dh = d // SB_HEADS
    q_ref[...] = (_dot(m, w_ref[:, 0:d]) * (dh ** -0.5 * LOG2E)).astype(BF16)
    k_ref[...] = _dot(m, w_ref[:, d:2 * d]).astype(BF16)
    _store_key_blocks(vt_ref, _dot_nt(wvt_ref[...], m))


def _proj_odd(h, g, w_qkv):
    n, d = h.shape
    g, g_spec = _picked(*g)
    tm = min(TOKEN_TILE, n)
    tk = min(ATTN_TILE, tm)
    row = lambda i: (i, 0)
    fixed = lambda i: (0, 0)
    out = jax.ShapeDtypeStruct((n, d), BF16)
    w_qk = w_qkv[:, :2 * d].astype(BF16)
    w_vt = w_qkv[:, 2 * d:].T.astype(BF16)
    return pl.pallas_call(
        _proj_odd_kernel,
        grid=(n // tm,),
        in_specs=[pl.BlockSpec((tm, d), row), g_spec,
                  pl.BlockSpec(w_qk.shape, fixed), pl.BlockSpec(w_vt.shape, fixed)],
        out_specs=[pl.BlockSpec((tm, d), row), pl.BlockSpec((tm, d), row),
                   pl.BlockSpec((tm // tk, d, tk), lambda i: (i, 0, 0))],
        out_shape=[out, out, jax.ShapeDtypeStruct((n // tk, d, tk), BF16)],
        compiler_params=_cparams(("parallel",)),
        name="proj_odd",
    )(h, g, w_qk, w_vt)


def _diff_kernel(lam_ref, g_ref, q_ref, k_ref, vt_ref, o_ref, m_ref, l_ref, acc_ref, *, lambda_init):
    i = pl.program_id(2)
    tk = vt_ref.shape[2]
    dv = DIFF_DV
    nsub = q_ref.shape[0] // tk
    base = i * nsub
    lane = lax.broadcasted_iota(jnp.int32, (tk, LANES), 1)
    causal = (lax.broadcasted_iota(jnp.int32, (tk, tk), 0) <= lax.broadcasted_iota(jnp.int32, (tk, tk), 1))

    def q_map(a, mi):
        q = q_ref[a * tk:(a + 1) * tk, :]
        keep = (lane < DIFF_DH) if mi == 0 else (lane >= DIFF_DH)
        return jnp.where(keep, q, jnp.zeros_like(q))

    def run_tiles(tiles):
        scores = [[_dot_nt(kb, q_map(a, mi)) for kb in kbs] for a, mi, kbs, _, _, _ in tiles]
        pending = []
        for (a, mi, _, vtbs, first, masked), ss in zip(tiles, scores):
            if masked:
                ss = [jnp.where(causal, s, -jnp.inf) for s in ss]
            m_blk = functools.reduce(jnp.maximum, [jnp.max(s, axis=0, keepdims=True) for s in ss])
            if first:
                alpha = None
                m_new = m_blk
            else:
                m_prev = m_ref[a, mi]
                m_new = jnp.maximum(m_prev, m_blk)
                alpha = jnp.exp2(m_prev - m_new)
            m_ref[a, mi] = m_new
            p = jnp.concatenate([jnp.exp2(s - m_new).astype(BF16) for s in ss], axis=0)
            vtb = jnp.concatenate(vtbs, axis=1)
            pending.append((a, mi, alpha, _dot(vtb, p)))
        for a, mi, alpha, pv in pending:
            if alpha is None:
                acc_ref[a, mi] = pv[:dv]
                l_ref[a, mi] = pv[dv:dv + 1]
            else:
                acc_ref[a, mi] = alpha * acc_ref[a, mi] + pv[:dv]
                l_ref[a, mi] = alpha * l_ref[a, mi] + pv[dv:dv + 1]

    def key_block(j):
        return k_ref[pl.ds(pl.multiple_of(j * tk, tk), tk), :], vt_ref[j]

    band = []
    for c in range(nsub):
        kb, vtb = key_block(base + c)
        band += [(a, mi, [kb], [vtb], c == 0, a == c) for a in range(c, nsub) for mi in range(2)]
    run_tiles(band)

    kv_unroll = DIFF_KV_UNROLL if nsub % DIFF_KV_UNROLL == 0 else 1
    group = DIFF_KV_GROUP if kv_unroll % DIFF_KV_GROUP == 0 else 1

    def body(step, carry):
        tiles = []
        for u in range(0, kv_unroll, group):
            blocks = [key_block(step * kv_unroll + u + w) for w in range(group)]
            kbs, vtbs = [b[0] for b in blocks], [b[1] for b in blocks]
            tiles += [(a, mi, kbs, vtbs, False, False) for a in range(nsub) for mi in range(2)]
        run_tiles(tiles)
        return carry

    lax.fori_loop(0, base // kv_unroll, body, 0)

    lp = lam_ref[...]
    lam = (jnp.exp(jnp.sum(lp[0:1] * lp[1:2], axis=-1, keepdims=True))
           - jnp.exp(jnp.sum(lp[2:3] * lp[3:4], axis=-1, keepdims=True)) + lambda_init)
    for a in range(nsub):
        o = acc_ref[a, 0] / l_ref[a, 0] - lam * (acc_ref[a, 1] / l_ref[a, 1])
        o = o * lax.rsqrt(jnp.mean(o * o, axis=0, keepdims=True) + EPS) * (1.0 - lambda_init)
        o_ref[a * tk:(a + 1) * tk, :] = (o.T * g_ref[...]).astype(o_ref.dtype)


def _diff_attention(dq, dk, dvt, lam_params, subln_g, lambda_init, batch, seq):
    n, wd = dq.shape
    lam_params, lam_spec = _picked(*lam_params)
    subln_g, g_spec = _picked(*subln_g)
    tk = dvt.shape[2]
    tq = min(ATTN_QSUB * tk, seq)
    nq = seq // tq
    nsub = tq // tk
    qspec = pl.BlockSpec((tq, LANES), lambda b, h, i: (b * nq + i, h))
    kspec = pl.BlockSpec((seq, LANES), lambda b, h, i: (b, h))
    vtspec = pl.BlockSpec((seq // tk, DIFF_DV + ONES_ROWS, tk), lambda b, h, i: (b, h, 0))
    return pl.pallas_call(
        functools.partial(_diff_kernel, lambda_init=lambda_init),
        grid=(batch, DIFF_HEADS, nq),
        in_specs=[lam_spec, g_spec, qspec, kspec, vtspec],
        out_specs=qspec,
        out_shape=jax.ShapeDtypeStruct((n, wd), BF16),
        scratch_shapes=[pltpu.VMEM((nsub, 2, 1, tk), F32), pltpu.VMEM((nsub, 2, 1, tk), F32),
                        pltpu.VMEM((nsub, 2, DIFF_DV, tk), F32)],
        compiler_params=_cparams(("parallel", "parallel", "arbitrary")),
        name="diff_attn",
    )(lam_params, subln_g, dq, dk, dvt)


def _gla_kernel(q_ref, k_ref, la_ref, v_ref, gg_ref, g_ref, o_ref, state_ref):
    c = GLA_CHUNK
    sub = GLA_SUB
    nsub = c // sub
    heads = state_ref.shape[0]
    pair_lane = lax.broadcasted_iota(jnp.int32, (c, LANES), 1)
    pair_lane3 = lax.broadcasted_iota(jnp.int32, (nsub, sub, LANES), 2)
    own = [pair_lane < GLA_DK, pair_lane >= GLA_DK]
    own3 = [pair_lane3 < GLA_DK, pair_lane3 >= GLA_DK]

    @pl.when(pl.program_id(1) == 0)
    def _():
        state_ref[...] = jnp.zeros(state_ref.shape, F32)

    r_i = lax.broadcasted_iota(jnp.int32, (c, c), 0)
    c_i = lax.broadcasted_iota(jnp.int32, (c, c), 1)
    tril = (c_i <= r_i).astype(BF16)
    levels = []
    s = c // 2
    while s >= sub:
        levels.append((s, (r_i // (2 * s) == c_i // (2 * s)) & (r_i % (2 * s) >= s) & (c_i % (2 * s) < s)))
        s //= 2
    diag_mask = (r_i // sub == c_i // sub) & (c_i <= r_i)
    lane3 = lax.broadcasted_iota(jnp.int32, (nsub, sub, c), 2)
    blk3 = lax.broadcasted_iota(jnp.int32, (nsub, sub, c), 0)
    g = g_ref[...]

    def step(si, carry):
        cols = [slice(h * LANES, (h + 1) * LANES) for h in range(heads)]
        rows = [pl.ds(pl.multiple_of((si * GLA_UNROLL + u) * c, c), c) for u in range(GLA_UNROLL)]

        pairs = [slice(p * LANES, (p + 1) * LANES) for p in range(heads // 2)]

        cums = {}
        for u in range(GLA_UNROLL):
            for p in range(heads // 2):
                la_hi, la_mid, la_lo = _split3(la_ref[rows[u], pairs[p]])
                cums[u, p] = _dot(tril, la_hi) + _dot(tril, la_mid) + _dot(tril, la_lo)

        states = [state_ref[h] for h in range(heads)]
        mids = {}
        for u in range(GLA_UNROLL):
            for p in range(heads // 2):
                q = q_ref[rows[u], pairs[p]]
                k = k_ref[rows[u], pairs[p]]
                cum = cums[u, p]
                last = cum[c - 1:c, :]
                q_dec = (q * jnp.exp(cum)).astype(BF16)
                k_dec = k * jnp.exp(last - cum)
                state_decay = jnp.exp(last)

                level_ops = []
                for s, _ in levels:
                    ref = jnp.concatenate([jnp.broadcast_to(cum[lo + s:lo + s + 1, :], (2 * s, LANES))
                                           for lo in range(0, c, 2 * s)], axis=0)
                    qs = q * jnp.exp(jnp.minimum(cum - ref, 0.0))
                    ks = k * jnp.exp(jnp.minimum(ref - cum, 0.0))
                    level_ops.append((qs, ks.astype(BF16)))

                q3 = q.reshape(nsub, sub, LANES)
                k3 = k.reshape(nsub, sub, LANES)
                c3 = cum.reshape(nsub, sub, LANES)
                a3 = [jnp.zeros((nsub, sub, c), F32) for _ in range(2)]
                for j in range(sub):
                    dec = jnp.exp(jnp.minimum(c3 - c3[:, j:j + 1, :], 0.0))
                    prod = q3 * k3[:, j:j + 1, :] * dec
                    for t in range(2):
                        col = jnp.sum(jnp.where(own3[t], prod, 0.0), axis=-1, keepdims=True)
                        a3[t] = jnp.where(lane3 == blk3 * sub + j, col, a3[t])

                for t in range(2):
                    h = 2 * p + t
                    v = v_ref[rows[u], cols[h]]
                    state = states[h]
                    o_inter = _dot_nt(q_dec, state.astype(BF16))
                    states[h] = state * state_decay + _dot_tn(v, jnp.where(own[t], k_dec, 0.0).astype(BF16))
                    attn = jnp.where(diag_mask, a3[t].reshape(c, c), 0.0)
                    for (_, mask), (qs, ks) in zip(levels, level_ops):
                        a_level = _dot_nt(jnp.where(own[t], qs, 0.0).astype(BF16), ks)
                        attn = jnp.where(mask, a_level, attn)
                    mids[u, h] = (o_inter, _dot(attn.astype(BF16), v))

        for h in range(heads):
            state_ref[h] = states[h]
        for u in range(GLA_UNROLL):
            for h in range(heads):
                o_inter, o_intra = mids[u, h]
                gate = gg_ref[rows[u], cols[h]]
                o_ref[rows[u], cols[h]] = (_rms(o_inter + o_intra, g)
                                           * (gate * jax.nn.sigmoid(gate))).astype(o_ref.dtype)
        return carry

    lax.fori_loop(0, q_ref.shape[0] // (c * GLA_UNROLL), step, 0)


def _gla(gq, gk, la, gv, gg, norm_g, batch, seq):
    n, wd = gv.shape
    norm_g, g_spec = _picked(*norm_g)
    t = min(GLA_TILE, seq)
    nt = seq // t
    spec = pl.BlockSpec((t, wd), lambda b, i: (b * nt + i, 0))
    kspec = pl.BlockSpec((t, gq.shape[1]), lambda b, i: (b * nt + i, 0))
    return pl.pallas_call(
        _gla_kernel,
        grid=(batch, nt),
        in_specs=[kspec, kspec, kspec, spec, spec, g_spec],
        out_specs=spec,
        out_shape=jax.ShapeDtypeStruct((n, wd), BF16),
        scratch_shapes=[pltpu.VMEM((GLA_HEADS, GLA_DV, LANES), F32)],
        compiler_params=_cparams(("parallel", "arbitrary")),
        name="gla",
    )(gq, gk, la, gv, gg, norm_g)


SUM_ROWS = 8


def _sb_kernel(q_ref, k_ref, vt_ref, o_ref, acc_ref):
    i = pl.program_id(2)
    tk = vt_ref.shape[2]
    nsub = q_ref.shape[0] // tk
    base = i * nsub
    r_s = lax.broadcasted_iota(jnp.int32, (tk + SUM_ROWS, tk), 0)
    c_s = lax.broadcasted_iota(jnp.int32, (tk + SUM_ROWS, tk), 1)
    suffix_mat = ((c_s > r_s) | (r_s >= tk)).astype(BF16)
    strict = (lax.broadcasted_iota(jnp.int32, (tk, tk), 0) < lax.broadcasted_iota(jnp.int32, (tk, tk), 1))

    def run_tiles(tiles, later):
        later = list(later)
        logits = [_dot_nt(kb, q_ref[a * tk:(a + 1) * tk, :]) for a, kb, _, _ in tiles]
        mids = []
        for (a, _, _, masked), z in zip(tiles, logits):
            log_term = jnp.log2(1.0 + jnp.exp2(-jnp.abs(z)))
            log_beta = jnp.minimum(z, 0.0) - log_term
            log_keep = log_beta - z
            if masked:
                log_keep = jnp.where(strict, log_keep, 0.0)
            mids.append((log_beta, _dot(suffix_mat, log_keep.astype(BF16))))
        pending = []
        for (a, _, vtb, masked), (log_beta, sums) in zip(tiles, mids):
            arg = log_beta + sums[:tk]
            if later[a] is not None:
                arg = arg + later[a]
            att = jnp.exp2(arg)
            if masked:
                att = jnp.where(strict, att, 0.0)
            pending.append((a, later[a] is None, _dot(vtb, att.astype(BF16))))
            later[a] = sums[tk:tk + 1] if later[a] is None else later[a] + sums[tk:tk + 1]
        for a, first, contrib in pending:
            if first:
                acc_ref[a] = contrib
            else:
                acc_ref[a] += contrib
        return tuple(later)

    def key_block(j):
        return k_ref[pl.ds(pl.multiple_of(j * tk, tk), tk), :], vt_ref[j]

    def retire(later_a, still_valid):
        return jnp.where(still_valid, later_a, -jnp.inf)

    def lockstep_tiles(offset, later):
        tiles, kept = [], []
        for a in range(nsub):
            j = base + a - offset
            kb, vtb = key_block(jnp.maximum(j, 0))
            tiles.append((a, kb, vtb, False))
            kept.append(retire(later[a], j >= 0))
        return tiles, kept

    diag = []
    for a in range(nsub):
        kb, vtb = key_block(base + a)
        diag.append((a, kb, vtb, True))
    later = run_tiles(diag, [None] * nsub)
    later = run_tiles(*lockstep_tiles(1, later))

    def alive(later):
        return (jnp.max(functools.reduce(jnp.maximum, later)) > LOG2_FLUSH).astype(jnp.int32)

    def cond(carry):
        step, live, _ = carry
        return (step < base + nsub - 2) & (live > 0)

    def body(carry):
        step, _, later = carry
        later = run_tiles(*lockstep_tiles(step + 2, later))
        return step + 1, alive(later), later

    lax.while_loop(cond, body, (jnp.int32(0), alive(later), later))
    for a in range(nsub):
        o_ref[a * tk:(a + 1) * tk, :] = acc_ref[a].T.astype(o_ref.dtype)


def _stick_breaking(q, k, vt, batch, seq):
    n, d = q.shape
    tk = vt.shape[2]
    tq = min(SB_QSUB * tk, seq)
    nq = seq // tq
    qspec = pl.BlockSpec((tq, LANES), lambda b, h, i: (b * nq + i, h))
    kspec = pl.BlockSpec((seq, LANES), lambda b, h, i: (b, h))
    vtspec = pl.BlockSpec((seq // tk, LANES, tk), lambda b, h, i: (b, h, 0))
    return pl.pallas_call(
        _sb_kernel,
        grid=(batch, SB_HEADS, nq),
        in_specs=[qspec, kspec, vtspec],
        out_specs=qspec,
        out_shape=jax.ShapeDtypeStruct((n, d), BF16),
        scratch_shapes=[pltpu.VMEM((tq // tk, LANES, tk), F32)],
        compiler_params=_cparams(("parallel", "parallel", "arbitrary")),
        name="stick_breaking",
    )(q, k, vt)


def kernel(x, ffn_pre_g, ffn_post_g, ffn_w_gate, ffn_w_up, ffn_w_down, mix_pre_g, mix_post_g, hyb_w_in, hyb_w_out, diff_lambda, diff_subln_g, gla_w_a2, gla_b_a, gla_norm_g, sb_w_qkv, sb_w_out):
    batch, seq, d = x.shape
    depth = ffn_pre_g.shape[0]
    assert d // SB_HEADS == LANES and DIFF_DV == LANES and GLA_DV == LANES
    h = x.reshape(batch * seq, d)

    n_ffn = ffn_w_gate.shape[1]
    dff = ffn_w_gate.shape[-1]
    wgu_all = jnp.concatenate([ffn_w_gate, ffn_w_up], axis=-1).astype(BF16).reshape(depth * n_ffn, d, 2 * dff)
    wd_all = ffn_w_down.astype(BF16).reshape(depth * n_ffn, dff, d)

    ffn_pre = ffn_pre_g.reshape(depth * n_ffn, 1, d)
    ffn_post = ffn_post_g.reshape(depth * n_ffn, 1, d)
    mix_pre = mix_pre_g.reshape(depth, 1, d)
    mix_post = mix_post_g.reshape(depth, 1, d)
    subln = diff_subln_g.reshape(-1, 1, DIFF_DV)
    gla_g = gla_norm_g.reshape(-1, 1, GLA_DV)
    hyb_out = hyb_w_out.astype(BF16)
    sb_out = sb_w_out.astype(BF16)

    def ffn(h, layer, j, mix=None):
        idx = layer * n_ffn + j
        return _ffn(h, (ffn_pre, idx), wgu_all, wd_all, idx, (ffn_post, idx), mix)

    for layer in range(depth):
        h = ffn(h, layer, 0)
        if layer % 2 == 0:
            e = layer // 2
            lambda_init = 0.8 - 0.6 * math.exp(-0.3 * layer)
            dq, dk, dvt, gq, gk, la, gv, gg = _proj_even(h, (mix_pre, layer), hyb_w_in[e], gla_w_a2[e], gla_b_a[e])
            a_out = _diff_attention(dq, dk, dvt, (diff_lambda, e), (subln, e), lambda_init, batch, seq)
            b_out = _gla(gq, gk, la, gv, gg, (gla_g, e), batch, seq)
            mix = (a_out, 0, b_out, 0, (hyb_out, e), (mix_post, layer))
        else:
            o = layer // 2
            q, k, vt = _proj_odd(h, (mix_pre, layer), sb_w_qkv[o])
            att = _stick_breaking(q, k, vt, batch, seq)
            mix = (att, 0, att, 1, (sb_out, o), (mix_post, layer))
        h = ffn(h, layer, 1, mix)
    return h.reshape(batch, seq, d)
```

```python
import functools
import math

import jax
import jax.numpy as jnp
from jax import lax
from jax.experimental import pallas as pl
from jax.experimental.pallas import tpu as pltpu

F32 = jnp.float32
BF16 = jnp.bfloat16

EPS = 1e-6
DIFF_HEADS = 4
DIFF_DH = 64
DIFF_DV = 2 * DIFF_DH
GLA_HEADS = 4
GLA_DK = 64
GLA_DV = 128
GLA_RANK = 16
GLA_TAU = 16.0
GLA_CHUNK = 64
GLA_UNROLL = 8
GLA_SUB = 8
SB_HEADS = 8

LANES = 128
VMEM_LIMIT = 56 * 1024 * 1024

TOKEN_TILE = 512
ATTN_TILE = 256
ATTN_QSUB = 4
SB_QSUB = 16
DIFF_KV_UNROLL = 4
DIFF_KV_GROUP = 2
ONES_ROWS = 16
LOG2_FLUSH = -150.0
LOG2E = math.log2(math.e)
GLA_TILE = 512
FFN_TILE = 1024
ROW_PART = 256


def _cparams(sem):
    return pltpu.CompilerParams(dimension_semantics=sem, vmem_limit_bytes=VMEM_LIMIT)


def _picked(stacked, idx):
    return stacked, pl.BlockSpec((None,) + stacked.shape[1:], lambda *_: (idx,) + (0,) * (stacked.ndim - 1))


def _rms(x, g):
    return x * lax.rsqrt(jnp.mean(x * x, axis=-1, keepdims=True) + EPS) * g


def _dot(a, b):
    return jnp.dot(a, b, preferred_element_type=F32)


def _dot_nt(a, b):
    return lax.dot_general(a, b, (((1,), (1,)), ((), ())), preferred_element_type=F32)


def _dot_tn(a, b):
    return lax.dot_general(a, b, (((0,), (0,)), ((), ())), preferred_element_type=F32)


def _split3(x):
    hi = x.astype(BF16)
    r1 = x - hi.astype(F32)
    mid = r1.astype(BF16)
    lo = (r1 - mid.astype(F32)).astype(BF16)
    return hi, mid, lo


def _neg_abs(x):
    bits = lax.bitcast_convert_type(x, jnp.int32) | jnp.int32(-2 ** 31)
    return lax.bitcast_convert_type(bits, F32)


def _softplus_parts(z):
    return jnp.maximum(z, 0.0), jnp.log(1.0 + jnp.exp(-jnp.abs(z)))


def _ffn_body(hs, gpre_ref, wgu_ref, wd_ref, gpost_ref):
    dff = wd_ref.shape[0]
    xns = [_rms(h, gpre_ref[...]).astype(BF16) for h in hs]
    acts = []
    for xn in xns:
        gu = _dot(xn, wgu_ref[...])
        gate = gu[:, :dff]
        acts.append((gate * jax.nn.sigmoid(gate) * gu[:, dff:]).astype(BF16))
    fs = [_dot(act, wd_ref[...]) for act in acts]
    return [h + 0.5 * _rms(f, gpost_ref[...]) for h, f in zip(hs, fs)]


def _row_parts(tm):
    part = min(ROW_PART, tm)
    return [pl.ds(p * part, part) for p in range(tm // part)]


def _ffn_kernel(h_ref, gpre_ref, wgu_ref, wd_ref, gpost_ref, o_ref):
    parts = _row_parts(h_ref.shape[0])
    outs = _ffn_body([h_ref[r, :] for r in parts], gpre_ref, wgu_ref, wd_ref, gpost_ref)
    for r, o in zip(parts, outs):
        o_ref[r, :] = o


def _mix_ffn_kernel(h_ref, a_ref, b_ref, wo_ref, gmix_ref, gpre_ref, wgu_ref, wd_ref, gpost_ref, o_ref):
    parts = _row_parts(h_ref.shape[0])
    half = a_ref.shape[1]
    ys = [_dot(a_ref[r, :], wo_ref[0:half, :]) + _dot(b_ref[r, :], wo_ref[half:, :]) for r in parts]
    hs = [h_ref[r, :] + _rms(y, gmix_ref[...]) for r, y in zip(parts, ys)]
    outs = _ffn_body(hs, gpre_ref, wgu_ref, wd_ref, gpost_ref)
    for r, o in zip(parts, outs):
        o_ref[r, :] = o


def _ffn(h, g_pre, wgu_all, wd_all, idx, g_post, mix=None):
    n, d = h.shape
    dff = wd_all.shape[1]
    tm = min(FFN_TILE, n)
    row = lambda i: (i, 0)
    g_pre, g_pre_spec = _picked(*g_pre)
    g_post, g_post_spec = _picked(*g_post)
    wgu_all, wgu_spec = _picked(wgu_all, idx)
    wd_all, wd_spec = _picked(wd_all, idx)
    ffn_specs = [g_pre_spec, wgu_spec, wd_spec, g_post_spec]
    ffn_args = (g_pre, wgu_all, wd_all, g_post)
    if mix is None:
        body, specs, args, name = _ffn_kernel, [], (), "ffn"
    else:
        a, a_col, b, b_col, w_out, g_mix = mix
        w_out, w_out_spec = _picked(*w_out)
        g_mix, g_mix_spec = _picked(*g_mix)
        half = w_out.shape[1] // 2
        body, name = _mix_ffn_kernel, "mix_ffn"
        specs = [pl.BlockSpec((tm, half), lambda i: (i, a_col)), pl.BlockSpec((tm, half), lambda i: (i, b_col)),
                 w_out_spec, g_mix_spec]
        args = (a, b, w_out, g_mix)
    return pl.pallas_call(
        body,
        grid=(n // tm,),
        in_specs=[pl.BlockSpec((tm, d), row)] + specs + ffn_specs,
        out_specs=pl.BlockSpec((tm, d), row),
        out_shape=jax.ShapeDtypeStruct((n, d), F32),
        compiler_params=_cparams(("parallel",)),
        name=name,
    )(h, *args, *ffn_args)


def _store_key_blocks(vt_ref, vt):
    tk = vt_ref.shape[2]
    for c in range(vt_ref.shape[0]):
        vt_ref[c] = vt[:, c * tk:(c + 1) * tk].astype(vt_ref.dtype)


def _proj_even_kernel(h_ref, g_ref, w_ref, wvt_ref, ones_ref, wga_ref, wa2_ref, ba_ref,
                      dq_ref, dk_ref, dvt_ref, gq_ref, gk_ref, la_ref, gv_ref, gg_ref):
    tk = dvt_ref.shape[2]
    w_hi, w_lo, _ = _split3(wa2_ref[...])
    widths = [r.shape[1] for r in (dq_ref, dk_ref, gq_ref, gk_ref, gv_ref, gg_ref)]
    starts = [sum(widths[:i]) for i in range(len(widths))]
    parts = _row_parts(h_ref.shape[0])
    ms = [_rms(h_ref[r, :], g_ref[...]).astype(BF16) for r in parts]
    for p, (r, m) in enumerate(zip(parts, ms)):
        def seg(i):
            return _dot(m, w_ref[:, starts[i]:starts[i] + widths[i]])

        dq_ref[r, :] = (seg(0) * (DIFF_DH ** -0.5 * LOG2E)).astype(BF16)
        dk_ref[r, :] = seg(1).astype(BF16)
        vt = _dot_nt(wvt_ref[...], m) + ones_ref[...]
        blocks = vt.shape[1] // tk
        for c in range(blocks):
            dvt_ref[p * blocks + c] = vt[:, c * tk:(c + 1) * tk].astype(dvt_ref.dtype)
        gq_ref[r, :] = seg(2) * GLA_DK ** -0.5
        gk_ref[r, :] = seg(3)
        gv_ref[r, :] = seg(4).astype(BF16)
        gg_ref[r, :] = seg(5)
        ga_hi, ga_lo, _ = _split3(_dot(m, wga_ref[...]))
        x = _dot(ga_hi, w_hi) + _dot(ga_lo, w_hi) + _dot(ga_hi, w_lo) + ba_ref[...]
        relu_neg, log_term = _softplus_parts(-x)
        la_ref[r, :] = -(relu_neg + log_term) * (1.0 / GLA_TAU)


def _proj_even(h, g, w_in, w_a2, b_a):
    n, d = h.shape
    g, g_spec = _picked(*g)
    tm = min(TOKEN_TILE, n)
    wd = DIFF_HEADS * DIFF_DV
    wk = GLA_HEADS * GLA_DK
    c = 0
    parts = []
    for width in (wd, wd, wd, wk, wk, GLA_HEADS * GLA_DV, GLA_HEADS * GLA_DV):
        parts.append(w_in[:, c:c + width])
        c += width
    w_vt = parts.pop(2).T.reshape(DIFF_HEADS, DIFF_DV, d)
    w_vt = jnp.pad(w_vt, ((0, 0), (0, ONES_ROWS), (0, 0))).reshape(-1, d).astype(BF16)
    ones_col = jnp.pad(jnp.zeros((DIFF_HEADS, DIFF_DV, 1), F32), ((0, 0), (0, ONES_ROWS), (0, 0)),
                       constant_values=1.0).reshape(-1, 1)
    vt_rows = w_vt.shape[0]
    w_main = jnp.concatenate(parts, axis=1).astype(BF16)
    tk = min(ATTN_TILE, tm)
    assert min(ROW_PART, tm) % tk == 0
    w_ga = jnp.pad(w_in[:, c:c + GLA_RANK], ((0, 0), (0, LANES - GLA_RANK))).astype(BF16)
    wa2 = jnp.pad(w_a2, ((0, LANES - GLA_RANK), (0, 0)))
    ba = b_a.reshape(1, wk)
    row = lambda i: (i, 0)
    fixed = lambda i: (0, 0)
    out_bf = jax.ShapeDtypeStruct((n, wd), BF16)
    out_f = jax.ShapeDtypeStruct((n, wd), F32)
    out_k = jax.ShapeDtypeStruct((n, wk), F32)
    out_vt = jax.ShapeDtypeStruct((n // tk, vt_rows, tk), BF16)
    rowspec = pl.BlockSpec((tm, wd), row)
    kspec = pl.BlockSpec((tm, wk), row)
    vtspec = pl.BlockSpec((tm // tk, vt_rows, tk), lambda i: (i, 0, 0))
    return pl.pallas_call(
        _proj_even_kernel,
        grid=(n // tm,),
        in_specs=[pl.BlockSpec((tm, d), row), g_spec,
                  pl.BlockSpec(w_main.shape, fixed), pl.BlockSpec(w_vt.shape, fixed),
                  pl.BlockSpec(ones_col.shape, fixed), pl.BlockSpec(w_ga.shape, fixed),
                  pl.BlockSpec(wa2.shape, fixed), pl.BlockSpec((1, wk), fixed)],
        out_specs=[rowspec, rowspec, vtspec, kspec, kspec, kspec, rowspec, rowspec],
        out_shape=[out_bf, out_bf, out_vt, out_k, out_k, out_k, out_bf, out_f],
        compiler_params=_cparams(("parallel",)),
        name="proj_even",
    )(h, g, w_main, w_vt, ones_col, w_ga, wa2, ba)


def _proj_odd_kernel(h_ref, g_ref, w_ref, wvt_ref, q_ref, k_ref, vt_ref):
    m = _rms(h_ref[...], g_ref[...]).astype(BF16)
    d = q_ref.shape[1]
    dh = d // SB_HEADS
    q_ref[...] = (_dot(m, w_ref[:, 0:d]) * (dh ** -0.5 * LOG2E)).astype(BF16)
    k_ref[...] = _dot(m, w_ref[:, d:2 * d]).astype(BF16)
    _store_key_blocks(vt_ref, _dot_nt(wvt_ref[...], m))


def _proj_odd(h, g, w_qkv):
    n, d = h.shape
    g, g_spec = _picked(*g)
    tm = min(TOKEN_TILE, n)
    tk = min(ATTN_TILE, tm)
    row = lambda i: (i, 0)
    fixed = lambda i: (0, 0)
    out = jax.ShapeDtypeStruct((n, d), BF16)
    w_qk = w_qkv[:, :2 * d].astype(BF16)
    w_vt = w_qkv[:, 2 * d:].T.astype(BF16)
    return pl.pallas_call(
        _proj_odd_kernel,
        grid=(n // tm,),
        in_specs=[pl.BlockSpec((tm, d), row), g_spec,
                  pl.BlockSpec(w_qk.shape, fixed), pl.BlockSpec(w_vt.shape, fixed)],
        out_specs=[pl.BlockSpec((tm, d), row), pl.BlockSpec((tm, d), row),
                   pl.BlockSpec((tm // tk, d, tk), lambda i: (i, 0, 0))],
        out_shape=[out, out, jax.ShapeDtypeStruct((n // tk, d, tk), BF16)],
        compiler_params=_cparams(("parallel",)),
        name="proj_odd",
    )(h, g, w_qk, w_vt)


def _diff_kernel(lam_ref, g_ref, q_ref, k_ref, vt_ref, o_ref, m_ref, l_ref, acc_ref, *, lambda_init):
    i = pl.program_id(2)
    tk = vt_ref.shape[2]
    dv = DIFF_DV
    nsub = q_ref.shape[0] // tk
    base = i * nsub
    lane = lax.broadcasted_iota(jnp.int32, (tk, LANES), 1)
    causal = (lax.broadcasted_iota(jnp.int32, (tk, tk), 0) <= lax.broadcasted_iota(jnp.int32, (tk, tk), 1))

    def q_map(a, mi):
        q = q_ref[a * tk:(a + 1) * tk, :]
        keep = (lane < DIFF_DH) if mi == 0 else (lane >= DIFF_DH)
        return jnp.where(keep, q, jnp.zeros_like(q))

    def run_tiles(tiles, scores):
        pending = []
        for (a, mi, vtbs, first, masked), s in zip(tiles, scores):
            if masked:
                s = jnp.where(causal, s, -jnp.inf)
            m_blk = jnp.max(s, axis=0, keepdims=True)
            if first:
                alpha = None
                m_new = m_blk
            else:
                m_prev = m_ref[a, mi]
                m_new = jnp.maximum(m_prev, m_blk)
                alpha = jnp.exp2(m_prev - m_new)
            m_ref[a, mi] = m_new
            p = jnp.exp2(s - m_new).astype(BF16)
            vtb = vtbs[0] if len(vtbs) == 1 else jnp.concatenate(vtbs, axis=1)
            pending.append((a, mi, alpha, _dot(vtb, p)))
        for a, mi, alpha, pv in pending:
            if alpha is None:
                acc_ref[a, mi] = pv[:dv]
                l_ref[a, mi] = pv[dv:dv + 1]
            else:
                acc_ref[a, mi] = alpha * acc_ref[a, mi] + pv[:dv]
                l_ref[a, mi] = alpha * l_ref[a, mi] + pv[dv:dv + 1]

    def keys(j, blocks=1):
        return k_ref[pl.ds(pl.multiple_of(j * tk, tk), blocks * tk), :]

    band, band_scores = [], []
    for c in range(nsub):
        kb, vtb = keys(base + c), vt_ref[base + c]
        for a in range(c, nsub):
            for mi in range(2):
                band.append((a, mi, [vtb], c == 0, a == c))
                band_scores.append(_dot_nt(kb, q_map(a, mi)))
    run_tiles(band, band_scores)

    kv_unroll = DIFF_KV_UNROLL if nsub % DIFF_KV_UNROLL == 0 else 1
    group = DIFF_KV_GROUP if kv_unroll % DIFF_KV_GROUP == 0 else 1

    def body(step, carry):
        first_block = step * kv_unroll
        kb = keys(first_block, kv_unroll)
        chains = [(a, mi) for a in range(nsub) for mi in range(2)]
        full = [_dot_nt(kb, q_map(a, mi)) for a, mi in chains]
        tiles, scores = [], []
        for u in range(0, kv_unroll, group):
            vtbs = [vt_ref[first_block + u + w] for w in range(group)]
            for (a, mi), s in zip(chains, full):
                tiles.append((a, mi, vtbs, False, False))
                scores.append(s[u * tk:(u + group) * tk])
        run_tiles(tiles, scores)
        return carry

    lax.fori_loop(0, base // kv_unroll, body, 0)

    lp = lam_ref[...]
    lam = (jnp.exp(jnp.sum(lp[0:1] * lp[1:2], axis=-1, keepdims=True))
           - jnp.exp(jnp.sum(lp[2:3] * lp[3:4], axis=-1, keepdims=True)) + lambda_init)
    for a in range(nsub):
        o = acc_ref[a, 0] / l_ref[a, 0] - lam * (acc_ref[a, 1] / l_ref[a, 1])
        o = o * lax.rsqrt(jnp.mean(o * o, axis=0, keepdims=True) + EPS) * (1.0 - lambda_init)
        o_ref[a * tk:(a + 1) * tk, :] = (o.T * g_ref[...]).astype(o_ref.dtype)


def _diff_attention(dq, dk, dvt, lam_params, subln_g, lambda_init, batch, seq):
    n, wd = dq.shape
    lam_params, lam_spec = _picked(*lam_params)
    subln_g, g_spec = _picked(*subln_g)
    tk = dvt.shape[2]
    tq = min(ATTN_QSUB * tk, seq)
    nq = seq // tq
    nsub = tq // tk
    qspec = pl.BlockSpec((tq, LANES), lambda b, h, i: (b * nq + i, h))
    kspec = pl.BlockSpec((seq, LANES), lambda b, h, i: (b, h))
    vtspec = pl.BlockSpec((seq // tk, DIFF_DV + ONES_ROWS, tk), lambda b, h, i: (b, h, 0))
    return pl.pallas_call(
        functools.partial(_diff_kernel, lambda_init=lambda_init),
        grid=(batch, DIFF_HEADS, nq),
        in_specs=[lam_spec, g_spec, qspec, kspec, vtspec],
        out_specs=qspec,
        out_shape=jax.ShapeDtypeStruct((n, wd), BF16),
        scratch_shapes=[pltpu.VMEM((nsub, 2, 1, tk), F32), pltpu.VMEM((nsub, 2, 1, tk), F32),
                        pltpu.VMEM((nsub, 2, DIFF_DV, tk), F32)],
        compiler_params=_cparams(("parallel", "parallel", "arbitrary")),
        name="diff_attn",
    )(lam_params, subln_g, dq, dk, dvt)


def _gla_kernel(q_ref, k_ref, la_ref, v_ref, gg_ref, g_ref, o_ref, state_ref):
    c = GLA_CHUNK
    sub = GLA_SUB
    nsub = c // sub
    heads = state_ref.shape[0]
    pair_lane = lax.broadcasted_iota(jnp.int32, (c, LANES), 1)
    pair_lane3 = lax.broadcasted_iota(jnp.int32, (nsub, sub, LANES), 2)
    own = [pair_lane < GLA_DK, pair_lane >= GLA_DK]
    own3 = [pair_lane3 < GLA_DK, pair_lane3 >= GLA_DK]

    @pl.when(pl.program_id(1) == 0)
    def _():
        state_ref[...] = jnp.zeros(state_ref.shape, F32)

    r_i = lax.broadcasted_iota(jnp.int32, (c, c), 0)
    c_i = lax.broadcasted_iota(jnp.int32, (c, c), 1)
    tril = (c_i <= r_i).astype(BF16)
    levels = []
    s = c // 2
    while s >= sub:
        levels.append((s, (r_i // (2 * s) == c_i // (2 * s)) & (r_i % (2 * s) >= s) & (c_i % (2 * s) < s)))
        s //= 2
    diag_mask = (r_i // sub == c_i // sub) & (c_i <= r_i)
    lane3 = lax.broadcasted_iota(jnp.int32, (nsub, sub, c), 2)
    blk3 = lax.broadcasted_iota(jnp.int32, (nsub, sub, c), 0)
    g = g_ref[...]

    def step(si, carry):
        cols = [slice(h * LANES, (h + 1) * LANES) for h in range(heads)]
        rows = [pl.ds(pl.multiple_of((si * GLA_UNROLL + u) * c, c), c) for u in range(GLA_UNROLL)]

        pairs = [slice(p * LANES, (p + 1) * LANES) for p in range(heads // 2)]

        cums = {}
        for u in range(GLA_UNROLL):
            for p in range(heads // 2):
                la_hi, la_mid, la_lo = _split3(la_ref[rows[u], pairs[p]])
                cums[u, p] = _dot(tril, la_hi) + _dot(tril, la_mid) + _dot(tril, la_lo)

        states = [state_ref[h] for h in range(heads)]
        mids = {}
        for u in range(GLA_UNROLL):
            for p in range(heads // 2):
                q = q_ref[rows[u], pairs[p]]
                k = k_ref[rows[u], pairs[p]]
                cum = cums[u, p]
                last = cum[c - 1:c, :]
                q_dec = (q * jnp.exp(cum)).astype(BF16)
                k_dec = k * jnp.exp(last - cum)
                state_decay = jnp.exp(last)

                level_ops = []
                for s, _ in levels:
                    ref = jnp.concatenate([jnp.broadcast_to(cum[lo + s:lo + s + 1, :], (2 * s, LANES))
                                           for lo in range(0, c, 2 * s)], axis=0)
                    qs = q * jnp.exp(jnp.minimum(cum - ref, 0.0))
                    ks = k * jnp.exp(jnp.minimum(ref - cum, 0.0))
                    level_ops.append((qs, ks.astype(BF16)))

                q3 = q.reshape(nsub, sub, LANES)
                k3 = k.reshape(nsub, sub, LANES)
                c3 = cum.reshape(nsub, sub, LANES)
                a3 = [jnp.zeros((nsub, sub, c), F32) for _ in range(2)]
                for j in range(sub):
                    dec = jnp.exp(jnp.minimum(c3 - c3[:, j:j + 1, :], 0.0))
                    prod = q3 * k3[:, j:j + 1, :] * dec
                    for t in range(2):
                        col = jnp.sum(jnp.where(own3[t], prod, 0.0), axis=-1, keepdims=True)
                        a3[t] = jnp.where(lane3 == blk3 * sub + j, col, a3[t])

                for t in range(2):
                    h = 2 * p + t
                    v = v_ref[rows[u], cols[h]]
                    state = states[h]
                    o_inter = _dot_nt(q_dec, state.astype(BF16))
                    states[h] = state * state_decay + _dot_tn(v, jnp.where(own[t], k_dec, 0.0).astype(BF16))
                    attn = jnp.where(diag_mask, a3[t].reshape(c, c), 0.0)
                    for (_, mask), (qs, ks) in zip(levels, level_ops):
                        a_level = _dot_nt(jnp.where(own[t], qs, 0.0).astype(BF16), ks)
                        attn = jnp.where(mask, a_level, attn)
                    mids[u, h] = (o_inter, _dot(attn.astype(BF16), v))

        for h in range(heads):
            state_ref[h] = states[h]
        for u in range(GLA_UNROLL):
            for h in range(heads):
                o_inter, o_intra = mids[u, h]
                gate = gg_ref[rows[u], cols[h]]
                o_ref[rows[u], cols[h]] = (_rms(o_inter + o_intra, g)
                                           * (gate * jax.nn.sigmoid(gate))).astype(o_ref.dtype)
        return carry

    lax.fori_loop(0, q_ref.shape[0] // (c * GLA_UNROLL), step, 0)


def _gla(gq, gk, la, gv, gg, norm_g, batch, seq):
    n, wd = gv.shape
    norm_g, g_spec = _picked(*norm_g)
    t = min(GLA_TILE, seq)
    nt = seq // t
    spec = pl.BlockSpec((t, wd), lambda b, i: (b * nt + i, 0))
    kspec = pl.BlockSpec((t, gq.shape[1]), lambda b, i: (b * nt + i, 0))
    return pl.pallas_call(
        _gla_kernel,
        grid=(batch, nt),
        in_specs=[kspec, kspec, kspec, spec, spec, g_spec],
        out_specs=spec,
        out_shape=jax.ShapeDtypeStruct((n, wd), BF16),
        scratch_shapes=[pltpu.VMEM((GLA_HEADS, GLA_DV, LANES), F32)],
        compiler_params=_cparams(("parallel", "arbitrary")),
        name="gla",
    )(gq, gk, la, gv, gg, norm_g)


SUM_ROWS = 8


def _sb_kernel(q_ref, k_ref, vt_ref, o_ref, acc_ref):
    i = pl.program_id(2)
    tk = vt_ref.shape[2]
    nsub = q_ref.shape[0] // tk
    base = i * nsub
    r_s = lax.broadcasted_iota(jnp.int32, (tk + SUM_ROWS, tk), 0)
    c_s = lax.broadcasted_iota(jnp.int32, (tk + SUM_ROWS, tk), 1)
    suffix_mat = ((c_s > r_s) | (r_s >= tk)).astype(BF16)
    strict = (lax.broadcasted_iota(jnp.int32, (tk, tk), 0) < lax.broadcasted_iota(jnp.int32, (tk, tk), 1))

    def run_tiles(tiles, later):
        later = list(later)
        logits = [_dot_nt(kb, q_ref[a * tk:(a + 1) * tk, :]) for a, kb, _, _ in tiles]
        mids = []
        for (a, _, _, masked), z in zip(tiles, logits):
            log_term = jnp.log2(1.0 + jnp.exp2(_neg_abs(z)))
            log_beta = jnp.minimum(z, 0.0) - log_term
            log_keep = log_beta - z
            if masked:
                log_keep = jnp.where(strict, log_keep, 0.0)
            mids.append((log_beta, _dot(suffix_mat, log_keep.astype(BF16))))
        pending = []
        for (a, _, vtb, masked), (log_beta, sums) in zip(tiles, mids):
            arg = log_beta + sums[:tk]
            if later[a] is not None:
                arg = arg + later[a]
            att = jnp.exp2(arg)
            if masked:
                att = jnp.where(strict, att, 0.0)
            pending.append((a, later[a] is None, _dot(vtb, att.astype(BF16))))
            later[a] = sums[tk:tk + 1] if later[a] is None else later[a] + sums[tk:tk + 1]
        for a, first, contrib in pending:
            if first:
                acc_ref[a] = contrib
            else:
                acc_ref[a] += contrib
        return tuple(later)

    def key_block(j):
        return k_ref[pl.ds(pl.multiple_of(j * tk, tk), tk), :], vt_ref[j]

    def retire(later_a, still_valid):
        return jnp.where(still_valid, later_a, -jnp.inf)

    def lockstep_tiles(offset, later):
        tiles, kept = [], []
        for a in range(nsub):
            j = base + a - offset
            kb, vtb = key_block(jnp.maximum(j, 0))
            tiles.append((a, kb, vtb, False))
            kept.append(retire(later[a], j >= 0))
        return tiles, kept

    diag = []
    for a in range(nsub):
        kb, vtb = key_block(base + a)
        diag.append((a, kb, vtb, True))
    later = run_tiles(diag, [None] * nsub)
    later = run_tiles(*lockstep_tiles(1, later))

    def alive(later):
        return (jnp.max(functools.reduce(jnp.maximum, later)) > LOG2_FLUSH).astype(jnp.int32)

    def cond(carry):
        step, live, _ = carry
        return (step < base + nsub - 2) & (live > 0)

    def body(carry):
        step, _, later = carry
        later = run_tiles(*lockstep_tiles(step + 2, later))
        return step + 1, alive(later), later

    lax.while_loop(cond, body, (jnp.int32(0), alive(later), later))
    for a in range(nsub):
        o_ref[a * tk:(a + 1) * tk, :] = acc_ref[a].T.astype(o_ref.dtype)


def _stick_breaking(q, k, vt, batch, seq):
    n, d = q.shape
    tk = vt.shape[2]
    tq = min(SB_QSUB * tk, seq)
    nq = seq // tq
    qspec = pl.BlockSpec((tq, LANES), lambda b, h, i: (b * nq + i, h))
    kspec = pl.BlockSpec((seq, LANES), lambda b, h, i: (b, h))
    vtspec = pl.BlockSpec((seq // tk, LANES, tk), lambda b, h, i: (b, h, 0))
    return pl.pallas_call(
        _sb_kernel,
        grid=(batch, SB_HEADS, nq),
        in_specs=[qspec, kspec, vtspec],
        out_specs=qspec,
        out_shape=jax.ShapeDtypeStruct((n, d), BF16),
        scratch_shapes=[pltpu.VMEM((tq // tk, LANES, tk), F32)],
        compiler_params=_cparams(("parallel", "parallel", "arbitrary")),
        name="stick_breaking",
    )(q, k, vt)


def kernel(x, ffn_pre_g, ffn_post_g, ffn_w_gate, ffn_w_up, ffn_w_down, mix_pre_g, mix_post_g, hyb_w_in, hyb_w_out, diff_lambda, diff_subln_g, gla_w_a2, gla_b_a, gla_norm_g, sb_w_qkv, sb_w_out):
    batch, seq, d = x.shape
    depth = ffn_pre_g.shape[0]
    assert d // SB_HEADS == LANES and DIFF_DV == LANES and GLA_DV == LANES
    h = x.reshape(batch * seq, d)

    n_ffn = ffn_w_gate.shape[1]
    dff = ffn_w_gate.shape[-1]
    wgu_all = jnp.concatenate([ffn_w_gate, ffn_w_up], axis=-1).astype(BF16).reshape(depth * n_ffn, d, 2 * dff)
    wd_all = ffn_w_down.astype(BF16).reshape(depth * n_ffn, dff, d)

    ffn_pre = ffn_pre_g.reshape(depth * n_ffn, 1, d)
    ffn_post = ffn_post_g.reshape(depth * n_ffn, 1, d)
    mix_pre = mix_pre_g.reshape(depth, 1, d)
    mix_post = mix_post_g.reshape(depth, 1, d)
    subln = diff_subln_g.reshape(-1, 1, DIFF_DV)
    gla_g = gla_norm_g.reshape(-1, 1, GLA_DV)
    hyb_out = hyb_w_out.astype(BF16)
    sb_out = sb_w_out.astype(BF16)

    def ffn(h, layer, j, mix=None):
        idx = layer * n_ffn + j
        return _ffn(h, (ffn_pre, idx), wgu_all, wd_all, idx, (ffn_post, idx), mix)

    for layer in range(depth):
        h = ffn(h, layer, 0)
        if layer % 2 == 0:
            e = layer // 2
            lambda_init = 0.8 - 0.6 * math.exp(-0.3 * layer)
            dq, dk, dvt, gq, gk, la, gv, gg = _proj_even(h, (mix_pre, layer), hyb_w_in[e], gla_w_a2[e], gla_b_a[e])
            a_out = _diff_attention(dq, dk, dvt, (diff_lambda, e), (subln, e), lambda_init, batch, seq)
            b_out = _gla(gq, gk, la, gv, gg, (gla_g, e), batch, seq)
            mix = (a_out, 0, b_out, 0, (hyb_out, e), (mix_post, layer))
        else:
            o = layer // 2
            q, k, vt = _proj_odd(h, (mix_pre, layer), sb_w_qkv[o])
            att = _stick_breaking(q, k, vt, batch, seq)
            mix = (att, 0, att, 1, (sb_out, o), (mix_post, layer))
        h = ffn(h, layer, 1, mix)
    return h.reshape(batch, seq, d)
```

```python
import functools
import math

import jax
import jax.numpy as jnp
from jax import lax
from jax.experimental import pallas as pl
from jax.experimental.pallas import tpu as pltpu

F32 = jnp.float32
BF16 = jnp.bfloat16

EPS = 1e-6
DIFF_HEADS = 4
DIFF_DH = 64
DIFF_DV = 2 * DIFF_DH
GLA_HEADS = 4
GLA_DK = 64
GLA_DV = 128
GLA_RANK = 16
GLA_TAU = 16.0
GLA_CHUNK = 64
GLA_UNROLL = 8
GLA_SUB = 8
SB_HEADS = 8

LANES = 128
VMEM_LIMIT = 56 * 1024 * 1024

TOKEN_TILE = 512
ATTN_TILE = 256
ATTN_QSUB = 4
SB_QSUB = 16
DIFF_KV_UNROLL = 4
DIFF_KV_GROUP = 2
ONES_ROWS = 16
LOG2_FLUSH = -150.0
LOG2E = math.log2(math.e)
GLA_TILE = 512
FFN_TILE = 1024
ROW_PART = 256


def _cparams(sem):
    return pltpu.CompilerParams(dimension_semantics=sem, vmem_limit_bytes=VMEM_LIMIT)


def _picked(stacked, idx):
    return stacked, pl.BlockSpec((None,) + stacked.shape[1:], lambda *_: (idx,) + (0,) * (stacked.ndim - 1))


def _rms(x, g):
    return x * lax.rsqrt(jnp.mean(x * x, axis=-1, keepdims=True) + EPS) * g


def _dot(a, b):
    return jnp.dot(a, b, preferred_element_type=F32)


def _dot_nt(a, b):
    return lax.dot_general(a, b, (((1,), (1,)), ((), ())), preferred_element_type=F32)


def _dot_tn(a, b):
    return lax.dot_general(a, b, (((0,), (0,)), ((), ())), preferred_element_type=F32)


def _split3(x):
    hi = x.astype(BF16)
    r1 = x - hi.astype(F32)
    mid = r1.astype(BF16)
    lo = (r1 - mid.astype(F32)).astype(BF16)
    return hi, mid, lo


def _neg_abs(x):
    bits = lax.bitcast_convert_type(x, jnp.int32) | jnp.int32(-2 ** 31)
    return lax.bitcast_convert_type(bits, F32)


def _softplus_parts(z):
    return jnp.maximum(z, 0.0), jnp.log(1.0 + jnp.exp(-jnp.abs(z)))


def _ffn_body(hs, gpre_ref, wgu_ref, wd_ref, gpost_ref):
    dff = wd_ref.shape[0]
    xns = [_rms(h, gpre_ref[...]).astype(BF16) for h in hs]
    acts = []
    for xn in xns:
        gu = _dot(xn, wgu_ref[...])
        gate = gu[:, :dff]
        acts.append((gate * jax.nn.sigmoid(gate) * gu[:, dff:]).astype(BF16))
    fs = [_dot(act, wd_ref[...]) for act in acts]
    return [h + 0.5 * _rms(f, gpost_ref[...]) for h, f in zip(hs, fs)]


def _row_parts(tm):
    part = min(ROW_PART, tm)
    return [pl.ds(p * part, part) for p in range(tm // part)]


def _ffn_kernel(h_ref, gpre_ref, wgu_ref, wd_ref, gpost_ref, o_ref):
    parts = _row_parts(h_ref.shape[0])
    outs = _ffn_body([h_ref[r, :] for r in parts], gpre_ref, wgu_ref, wd_ref, gpost_ref)
    for r, o in zip(parts, outs):
        o_ref[r, :] = o


def _mix_ffn_kernel(h_ref, a_ref, b_ref, wo_ref, gmix_ref, gpre_ref, wgu_ref, wd_ref, gpost_ref, o_ref):
    parts = _row_parts(h_ref.shape[0])
    half = a_ref.shape[1]
    ys = [_dot(a_ref[r, :], wo_ref[0:half, :]) + _dot(b_ref[r, :], wo_ref[half:, :]) for r in parts]
    hs = [h_ref[r, :] + _rms(y, gmix_ref[...]) for r, y in zip(parts, ys)]
    outs = _ffn_body(hs, gpre_ref, wgu_ref, wd_ref, gpost_ref)
    for r, o in zip(parts, outs):
        o_ref[r, :] = o


def _ffn(h, g_pre, wgu_all, wd_all, idx, g_post, mix=None):
    n, d = h.shape
    dff = wd_all.shape[1]
    tm = min(FFN_TILE, n)
    row = lambda i: (i, 0)
    g_pre, g_pre_spec = _picked(*g_pre)
    g_post, g_post_spec = _picked(*g_post)
    wgu_all, wgu_spec = _picked(wgu_all, idx)
    wd_all, wd_spec = _picked(wd_all, idx)
    ffn_specs = [g_pre_spec, wgu_spec, wd_spec, g_post_spec]
    ffn_args = (g_pre, wgu_all, wd_all, g_post)
    if mix is None:
        body, specs, args, name = _ffn_kernel, [], (), "ffn"
    else:
        a, a_col, b, b_col, w_out, g_mix = mix
        w_out, w_out_spec = _picked(*w_out)
        g_mix, g_mix_spec = _picked(*g_mix)
        half = w_out.shape[1] // 2
        body, name = _mix_ffn_kernel, "mix_ffn"
        specs = [pl.BlockSpec((tm, half), lambda i: (i, a_col)), pl.BlockSpec((tm, half), lambda i: (i, b_col)),
                 w_out_spec, g_mix_spec]
        args = (a, b, w_out, g_mix)
    return pl.pallas_call(
        body,
        grid=(n // tm,),
        in_specs=[pl.BlockSpec((tm, d), row)] + specs + ffn_specs,
        out_specs=pl.BlockSpec((tm, d), row),
        out_shape=jax.ShapeDtypeStruct((n, d), F32),
        compiler_params=_cparams(("parallel",)),
        name=name,
    )(h, *args, *ffn_args)


def _store_key_blocks(vt_ref, vt):
    tk = vt_ref.shape[2]
    for c in range(vt_ref.shape[0]):
        vt_ref[c] = vt[:, c * tk:(c + 1) * tk].astype(vt_ref.dtype)


def _proj_even_kernel(h_ref, g_ref, w_ref, wvt_ref, ones_ref, wga_ref, wa2_ref, ba_ref,
                      dq_ref, dk_ref, dvt_ref, gq_ref, gk_ref, la_ref, gv_ref, gg_ref):
    tk = dvt_ref.shape[2]
    w_hi, w_lo, _ = _split3(wa2_ref[...])
    widths = [r.shape[1] for r in (dq_ref, dk_ref, gq_ref, gk_ref, gv_ref, gg_ref)]
    starts = [sum(widths[:i]) for i in range(len(widths))]
    parts = _row_parts(h_ref.shape[0])
    ms = [_rms(h_ref[r, :], g_ref[...]).astype(BF16) for r in parts]
    for p, (r, m) in enumerate(zip(parts, ms)):
        def seg(i):
            return _dot(m, w_ref[:, starts[i]:starts[i] + widths[i]])

        dq_ref[r, :] = (seg(0) * (DIFF_DH ** -0.5 * LOG2E)).astype(BF16)
        dk_ref[r, :] = seg(1).astype(BF16)
        vt = _dot_nt(wvt_ref[...], m) + ones_ref[...]
        blocks = vt.shape[1] // tk
        for c in range(blocks):
            dvt_ref[p * blocks + c] = vt[:, c * tk:(c + 1) * tk].astype(dvt_ref.dtype)
        gq_ref[r, :] = seg(2) * GLA_DK ** -0.5
        gk_ref[r, :] = seg(3)
        gv_ref[r, :] = seg(4).astype(BF16)
        gg_ref[r, :] = seg(5)
        ga_hi, ga_lo, _ = _split3(_dot(m, wga_ref[...]))
        x = _dot(ga_hi, w_hi) + _dot(ga_lo, w_hi) + _dot(ga_hi, w_lo) + ba_ref[...]
        relu_neg, log_term = _softplus_parts(-x)
        la_ref[r, :] = -(relu_neg + log_term) * (1.0 / GLA_TAU)


def _proj_even(h, g, w_in, w_a2, b_a):
    n, d = h.shape
    g, g_spec = _picked(*g)
    tm = min(TOKEN_TILE, n)
    wd = DIFF_HEADS * DIFF_DV
    wk = GLA_HEADS * GLA_DK
    c = 0
    parts = []
    for width in (wd, wd, wd, wk, wk, GLA_HEADS * GLA_DV, GLA_HEADS * GLA_DV):
        parts.append(w_in[:, c:c + width])
        c += width
    w_vt = parts.pop(2).T.reshape(DIFF_HEADS, DIFF_DV, d)
    w_vt = jnp.pad(w_vt, ((0, 0), (0, ONES_ROWS), (0, 0))).reshape(-1, d).astype(BF16)
    ones_col = jnp.pad(jnp.zeros((DIFF_HEADS, DIFF_DV, 1), F32), ((0, 0), (0, ONES_ROWS), (0, 0)),
                       constant_values=1.0).reshape(-1, 1)
    vt_rows = w_vt.shape[0]
    w_main = jnp.concatenate(parts, axis=1).astype(BF16)
    tk = min(ATTN_TILE, tm)
    assert min(ROW_PART, tm) % tk == 0
    w_ga = jnp.pad(w_in[:, c:c + GLA_RANK], ((0, 0), (0, LANES - GLA_RANK))).astype(BF16)
    wa2 = jnp.pad(w_a2, ((0, LANES - GLA_RANK), (0, 0)))
    ba = b_a.reshape(1, wk)
    row = lambda i: (i, 0)
    fixed = lambda i: (0, 0)
    out_bf = jax.ShapeDtypeStruct((n, wd), BF16)
    out_f = jax.ShapeDtypeStruct((n, wd), F32)
    out_k = jax.ShapeDtypeStruct((n, wk), F32)
    out_vt = jax.ShapeDtypeStruct((n // tk, vt_rows, tk), BF16)
    rowspec = pl.BlockSpec((tm, wd), row)
    kspec = pl.BlockSpec((tm, wk), row)
    vtspec = pl.BlockSpec((tm // tk, vt_rows, tk), lambda i: (i, 0, 0))
    return pl.pallas_call(
        _proj_even_kernel,
        grid=(n // tm,),
        in_specs=[pl.BlockSpec((tm, d), row), g_spec,
                  pl.BlockSpec(w_main.shape, fixed), pl.BlockSpec(w_vt.shape, fixed),
                  pl.BlockSpec(ones_col.shape, fixed), pl.BlockSpec(w_ga.shape, fixed),
                  pl.BlockSpec(wa2.shape, fixed), pl.BlockSpec((1, wk), fixed)],
        out_specs=[rowspec, rowspec, vtspec, kspec, kspec, kspec, rowspec, rowspec],
        out_shape=[out_bf, out_bf, out_vt, out_k, out_k, out_k, out_bf, out_f],
        compiler_params=_cparams(("parallel",)),
        name="proj_even",
    )(h, g, w_main, w_vt, ones_col, w_ga, wa2, ba)


def _proj_odd_kernel(h_ref, g_ref, w_ref, wvt_ref, q_ref, k_ref, vt_ref):
    m = _rms(h_ref[...], g_ref[...]).astype(BF16)
    d = q_ref.shape[1]
    dh = d // SB_HEADS
    q_ref[...] = (_dot(m, w_ref[:, 0:d]) * (dh ** -0.5 * LOG2E)).astype(BF16)
    k_ref[...] = _dot(m, w_ref[:, d:2 * d]).astype(BF16)
    _store_key_blocks(vt_ref, _dot_nt(wvt_ref[...], m))


def _proj_odd(h, g, w_qkv):
    n, d = h.shape
    g, g_spec = _picked(*g)
    tm = min(TOKEN_TILE, n)
    tk = min(ATTN_TILE, tm)
    row = lambda i: (i, 0)
    fixed = lambda i: (0, 0)
    out = jax.ShapeDtypeStruct((n, d), BF16)
    w_qk = w_qkv[:, :2 * d].astype(BF16)
    w_vt = w_qkv[:, 2 * d:].T.astype(BF16)
    return pl.pallas_call(
        _proj_odd_kernel,
        grid=(n // tm,),
        in_specs=[pl.BlockSpec((tm, d), row), g_spec,
                  pl.BlockSpec(w_qk.shape, fixed), pl.BlockSpec(w_vt.shape, fixed)],
        out_specs=[pl.BlockSpec((tm, d), row), pl.BlockSpec((tm, d), row),
                   pl.BlockSpec((tm // tk, d, tk), lambda i: (i, 0, 0))],
        out_shape=[out, out, jax.ShapeDtypeStruct((n // tk, d, tk), BF16)],
        compiler_params=_cparams(("parallel",)),
        name="proj_odd",
    )(h, g, w_qk, w_vt)


def _diff_kernel(lam_ref, g_ref, q_ref, k_ref, vt_ref, o_ref, m_ref, l_ref, acc_ref, *, lambda_init):
    i = pl.program_id(2)
    tk = vt_ref.shape[2]
    dv = DIFF_DV
    nsub = q_ref.shape[0] // tk
    base = i * nsub
    lane = lax.broadcasted_iota(jnp.int32, (tk, LANES), 1)
    causal = (lax.broadcasted_iota(jnp.int32, (tk, tk), 0) <= lax.broadcasted_iota(jnp.int32, (tk, tk), 1))

    def q_map(a, mi):
        q = q_ref[a * tk:(a + 1) * tk, :]
        keep = (lane < DIFF_DH) if mi == 0 else (lane >= DIFF_DH)
        return jnp.where(keep, q, jnp.zeros_like(q))

    def run_tiles(tiles, scores):
        pending = []
        for (a, mi, vtbs, first, masked), ss in zip(tiles, scores):
            if masked:
                ss = [jnp.where(causal, s, -jnp.inf) for s in ss]
            m_blk = functools.reduce(jnp.maximum, [jnp.max(s, axis=0, keepdims=True) for s in ss])
            if first:
                alpha = None
                m_new = m_blk
            else:
                m_prev = m_ref[a, mi]
                m_new = jnp.maximum(m_prev, m_blk)
                alpha = jnp.exp2(m_prev - m_new)
            m_ref[a, mi] = m_new
            p = jnp.concatenate([jnp.exp2(s - m_new).astype(BF16) for s in ss], axis=0)
            vtb = jnp.concatenate(vtbs, axis=1)
            pending.append((a, mi, alpha, _dot(vtb, p)))
        for a, mi, alpha, pv in pending:
            if alpha is None:
                acc_ref[a, mi] = pv[:dv]
                l_ref[a, mi] = pv[dv:dv + 1]
            else:
                acc_ref[a, mi] = alpha * acc_ref[a, mi] + pv[:dv]
                l_ref[a, mi] = alpha * l_ref[a, mi] + pv[dv:dv + 1]

    def keys(j):
        return k_ref[pl.ds(pl.multiple_of(j * tk, tk), tk), :]

    band, band_scores = [], []
    for c in range(nsub):
        kb, vtb = keys(base + c), vt_ref[base + c]
        for a in range(c, nsub):
            for mi in range(2):
                band.append((a, mi, [vtb], c == 0, a == c))
                band_scores.append([_dot_nt(kb, q_map(a, mi))])
    run_tiles(band, band_scores)

    kv_unroll = DIFF_KV_UNROLL if nsub % DIFF_KV_UNROLL == 0 else 1
    group = DIFF_KV_GROUP if kv_unroll % DIFF_KV_GROUP == 0 else 1

    def body(step, carry):
        tiles, scores = [], []
        for u in range(0, kv_unroll, group):
            blocks = [step * kv_unroll + u + w for w in range(group)]
            kbs = [keys(j) for j in blocks]
            vtbs = [vt_ref[j] for j in blocks]
            for a in range(nsub):
                for mi in range(2):
                    tiles.append((a, mi, vtbs, False, False))
                    scores.append([_dot_nt(kb, q_map(a, mi)) for kb in kbs])
        run_tiles(tiles, scores)
        return carry

    lax.fori_loop(0, base // kv_unroll, body, 0)

    lp = lam_ref[...]
    lam = (jnp.exp(jnp.sum(lp[0:1] * lp[1:2], axis=-1, keepdims=True))
           - jnp.exp(jnp.sum(lp[2:3] * lp[3:4], axis=-1, keepdims=True)) + lambda_init)
    for a in range(nsub):
        o = acc_ref[a, 0] / l_ref[a, 0] - lam * (acc_ref[a, 1] / l_ref[a, 1])
        o = o * lax.rsqrt(jnp.mean(o * o, axis=0, keepdims=True) + EPS) * (1.0 - lambda_init)
        o_ref[a * tk:(a + 1) * tk, :] = (o.T * g_ref[...]).astype(o_ref.dtype)


def _diff_attention(dq, dk, dvt, lam_params, subln_g, lambda_init, batch, seq):
    n, wd = dq.shape
    lam_params, lam_spec = _picked(*lam_params)
    subln_g, g_spec = _picked(*subln_g)
    tk = dvt.shape[2]
    tq = min(ATTN_QSUB * tk, seq)
    nq = seq // tq
    nsub = tq // tk
    qspec = pl.BlockSpec((tq, LANES), lambda b, h, i: (b * nq + i, h))
    kspec = pl.BlockSpec((seq, LANES), lambda b, h, i: (b, h))
    vtspec = pl.BlockSpec((seq // tk, DIFF_DV + ONES_ROWS, tk), lambda b, h, i: (b, h, 0))
    return pl.pallas_call(
        functools.partial(_diff_kernel, lambda_init=lambda_init),
        grid=(batch, DIFF_HEADS, nq),
        in_specs=[lam_spec, g_spec, qspec, kspec, vtspec],
        out_specs=qspec,
        out_shape=jax.ShapeDtypeStruct((n, wd), BF16),
        scratch_shapes=[pltpu.VMEM((nsub, 2, 1, tk), F32), pltpu.VMEM((nsub, 2, 1, tk), F32),
                        pltpu.VMEM((nsub, 2, DIFF_DV, tk), F32)],
        compiler_params=_cparams(("parallel", "parallel", "arbitrary")),
        name="diff_attn",
    )(lam_params, subln_g, dq, dk, dvt)


def _gla_kernel(q_ref, k_ref, la_ref, v_ref, gg_ref, g_ref, o_ref, state_ref):
    c = GLA_CHUNK
    sub = GLA_SUB
    nsub = c // sub
    heads = state_ref.shape[0]
    pair_lane = lax.broadcasted_iota(jnp.int32, (c, LANES), 1)
    pair_lane3 = lax.broadcasted_iota(jnp.int32, (nsub, sub, LANES), 2)
    own = [pair_lane < GLA_DK, pair_lane >= GLA_DK]
    own3 = [pair_lane3 < GLA_DK, pair_lane3 >= GLA_DK]

    @pl.when(pl.program_id(1) == 0)
    def _():
        state_ref[...] = jnp.zeros(state_ref.shape, F32)

    r_i = lax.broadcasted_iota(jnp.int32, (c, c), 0)
    c_i = lax.broadcasted_iota(jnp.int32, (c, c), 1)
    tril = (c_i <= r_i).astype(BF16)
    levels = []
    s = c // 2
    while s >= sub:
        levels.append((s, (r_i // (2 * s) == c_i // (2 * s)) & (r_i % (2 * s) >= s) & (c_i % (2 * s) < s)))
        s //= 2
    diag_mask = (r_i // sub == c_i // sub) & (c_i <= r_i)
    lane3 = lax.broadcasted_iota(jnp.int32, (nsub, sub, c), 2)
    blk3 = lax.broadcasted_iota(jnp.int32, (nsub, sub, c), 0)
    g = g_ref[...]

    def step(si, carry):
        cols = [slice(h * LANES, (h + 1) * LANES) for h in range(heads)]
        rows = [pl.ds(pl.multiple_of((si * GLA_UNROLL + u) * c, c), c) for u in range(GLA_UNROLL)]

        pairs = [slice(p * LANES, (p + 1) * LANES) for p in range(heads // 2)]

        cums = {}
        for u in range(GLA_UNROLL):
            for p in range(heads // 2):
                la_hi, la_mid, la_lo = _split3(la_ref[rows[u], pairs[p]])
                cums[u, p] = _dot(tril, la_hi) + _dot(tril, la_mid) + _dot(tril, la_lo)

        states = [state_ref[h] for h in range(heads)]
        mids = {}
        for u in range(GLA_UNROLL):
            for p in range(heads // 2):
                q = q_ref[rows[u], pairs[p]]
                k = k_ref[rows[u], pairs[p]]
                cum = cums[u, p]
                last = cum[c - 1:c, :]
                q_dec = (q * jnp.exp(cum)).astype(BF16)
                k_dec = k * jnp.exp(last - cum)
                state_decay = jnp.exp(last)

                level_ops = []
                for s, _ in levels:
                    ref = jnp.concatenate([jnp.broadcast_to(cum[lo + s:lo + s + 1, :], (2 * s, LANES))
                                           for lo in range(0, c, 2 * s)], axis=0)
                    qs = q * jnp.exp(jnp.minimum(cum - ref, 0.0))
                    ks = k * jnp.exp(jnp.minimum(ref - cum, 0.0))
                    level_ops.append((qs, ks.astype(BF16)))

                q3 = q.reshape(nsub, sub, LANES)
                k3 = k.reshape(nsub, sub, LANES)
                c3 = cum.reshape(nsub, sub, LANES)
                a3 = [jnp.zeros((nsub, sub, c), F32) for _ in range(2)]
                for j in range(sub):
                    dec = jnp.exp(jnp.minimum(c3 - c3[:, j:j + 1, :], 0.0))
                    prod = q3 * k3[:, j:j + 1, :] * dec
                    for t in range(2):
                        col = jnp.sum(jnp.where(own3[t], prod, 0.0), axis=-1, keepdims=True)
                        a3[t] = jnp.where(lane3 == blk3 * sub + j, col, a3[t])

                for t in range(2):
                    h = 2 * p + t
                    v = v_ref[rows[u], cols[h]]
                    state = states[h]
                    o_inter = _dot_nt(q_dec, state.astype(BF16))
                    states[h] = state * state_decay + _dot_tn(v, jnp.where(own[t], k_dec, 0.0).astype(BF16))
                    attn = jnp.where(diag_mask, a3[t].reshape(c, c), 0.0)
                    for (_, mask), (qs, ks) in zip(levels, level_ops):
                        a_level = _dot_nt(jnp.where(own[t], qs, 0.0).astype(BF16), ks)
                        attn = jnp.where(mask, a_level, attn)
                    mids[u, h] = (o_inter, _dot(attn.astype(BF16), v))

        for h in range(heads):
            state_ref[h] = states[h]
        for u in range(GLA_UNROLL):
            for h in range(heads):
                o_inter, o_intra = mids[u, h]
                gate = gg_ref[rows[u], cols[h]]
                o_ref[rows[u], cols[h]] = (_rms(o_inter + o_intra, g)
                                           * (gate * jax.nn.sigmoid(gate))).astype(o_ref.dtype)
        return carry

    lax.fori_loop(0, q_ref.shape[0] // (c * GLA_UNROLL), step, 0)


def _gla(gq, gk, la, gv, gg, norm_g, batch, seq):
    n, wd = gv.shape
    norm_g, g_spec = _picked(*norm_g)
    t = min(GLA_TILE, seq)
    nt = seq // t
    spec = pl.BlockSpec((t, wd), lambda b, i: (b * nt + i, 0))
    kspec = pl.BlockSpec((t, gq.shape[1]), lambda b, i: (b * nt + i, 0))
    return pl.pallas_call(
        _gla_kernel,
        grid=(batch, nt),
        in_specs=[kspec, kspec, kspec, spec, spec, g_spec],
        out_specs=spec,
        out_shape=jax.ShapeDtypeStruct((n, wd), BF16),
        scratch_shapes=[pltpu.VMEM((GLA_HEADS, GLA_DV, LANES), F32)],
        compiler_params=_cparams(("parallel", "arbitrary")),
        name="gla",
    )(gq, gk, la, gv, gg, norm_g)


SUM_ROWS = 8


def _sb_kernel(q_ref, k_ref, vt_ref, o_ref, acc_ref):
    i = pl.program_id(2)
    tk = vt_ref.shape[2]
    nsub = q_ref.shape[0] // tk
    base = i * nsub
    r_s = lax.broadcasted_iota(jnp.int32, (tk + SUM_ROWS, tk), 0)
    c_s = lax.broadcasted_iota(jnp.int32, (tk + SUM_ROWS, tk), 1)
    suffix_mat = ((c_s > r_s) | (r_s >= tk)).astype(BF16)
    strict = (lax.broadcasted_iota(jnp.int32, (tk, tk), 0) < lax.broadcasted_iota(jnp.int32, (tk, tk), 1))

    def run_tiles(tiles, later):
        later = list(later)
        logits = [_dot_nt(kb, q_ref[a * tk:(a + 1) * tk, :]) for a, kb, _, _ in tiles]
        mids = []
        for (a, _, _, masked), z in zip(tiles, logits):
            log_term = jnp.log2(1.0 + jnp.exp2(_neg_abs(z)))
            log_beta = jnp.minimum(z, 0.0) - log_term
            log_keep = log_beta - z
            if masked:
                log_keep = jnp.where(strict, log_keep, 0.0)
            mids.append((log_beta, _dot(suffix_mat, log_keep.astype(BF16))))
        pending = []
        for (a, _, vtb, masked), (log_beta, sums) in zip(tiles, mids):
            arg = log_beta + sums[:tk]
            if later[a] is not None:
                arg = arg + later[a]
            att = jnp.exp2(arg)
            if masked:
                att = jnp.where(strict, att, 0.0)
            pending.append((a, later[a] is None, _dot(vtb, att.astype(BF16))))
            later[a] = sums[tk:tk + 1] if later[a] is None else later[a] + sums[tk:tk + 1]
        for a, first, contrib in pending:
            if first:
                acc_ref[a] = contrib
            else:
                acc_ref[a] += contrib
        return tuple(later)

    def key_block(j):
        return k_ref[pl.ds(pl.multiple_of(j * tk, tk), tk), :], vt_ref[j]

    def retire(later_a, still_valid):
        return jnp.where(still_valid, later_a, -jnp.inf)

    def lockstep_tiles(offset, later):
        tiles, kept = [], []
        for a in range(nsub):
            j = base + a - offset
            kb, vtb = key_block(jnp.maximum(j, 0))
            tiles.append((a, kb, vtb, False))
            kept.append(retire(later[a], j >= 0))
        return tiles, kept

    diag = []
    for a in range(nsub):
        kb, vtb = key_block(base + a)
        diag.append((a, kb, vtb, True))
    later = run_tiles(diag, [None] * nsub)
    later = run_tiles(*lockstep_tiles(1, later))

    def alive(later):
        return (jnp.max(functools.reduce(jnp.maximum, later)) > LOG2_FLUSH).astype(jnp.int32)

    def cond(carry):
        step, live, _ = carry
        return (step < base + nsub - 2) & (live > 0)

    def body(carry):
        step, _, later = carry
        later = run_tiles(*lockstep_tiles(step + 2, later))
        return step + 1, alive(later), later

    lax.while_loop(cond, body, (jnp.int32(0), alive(later), later))
    for a in range(nsub):
        o_ref[a * tk:(a + 1) * tk, :] = acc_ref[a].T.astype(o_ref.dtype)


def _stick_breaking(q, k, vt, batch, seq):
    n, d = q.shape
    tk = vt.shape[2]
    tq = min(SB_QSUB * tk, seq)
    nq = seq // tq
    qspec = pl.BlockSpec((tq, LANES), lambda b, h, i: (b * nq + i, h))
    kspec = pl.BlockSpec((seq, LANES), lambda b, h, i: (b, h))
    vtspec = pl.BlockSpec((seq // tk, LANES, tk), lambda b, h, i: (b, h, 0))
    return pl.pallas_call(
        _sb_kernel,
        grid=(batch, SB_HEADS, nq),
        in_specs=[qspec, kspec, vtspec],
        out_specs=qspec,
        out_shape=jax.ShapeDtypeStruct((n, d), BF16),
        scratch_shapes=[pltpu.VMEM((tq // tk, LANES, tk), F32)],
        compiler_params=_cparams(("parallel", "parallel", "arbitrary")),
        name="stick_breaking",
    )(q, k, vt)


def kernel(x, ffn_pre_g, ffn_post_g, ffn_w_gate, ffn_w_up, ffn_w_down, mix_pre_g, mix_post_g, hyb_w_in, hyb_w_out, diff_lambda, diff_subln_g, gla_w_a2, gla_b_a, gla_norm_g, sb_w_qkv, sb_w_out):
    batch, seq, d = x.shape
    depth = ffn_pre_g.shape[0]
    assert d // SB_HEADS == LANES and DIFF_DV == LANES and GLA_DV == LANES
    h = x.reshape(batch * seq, d)

    n_ffn = ffn_w_gate.shape[1]
    dff = ffn_w_gate.shape[-1]
    wgu_all = jnp.concatenate([ffn_w_gate, ffn_w_up], axis=-1).astype(BF16).reshape(depth * n_ffn, d, 2 * dff)
    wd_all = ffn_w_down.astype(BF16).reshape(depth * n_ffn, dff, d)

    ffn_pre = ffn_pre_g.reshape(depth * n_ffn, 1, d)
    ffn_post = ffn_post_g.reshape(depth * n_ffn, 1, d)
    mix_pre = mix_pre_g.reshape(depth, 1, d)
    mix_post = mix_post_g.reshape(depth, 1, d)
    subln = diff_subln_g.reshape(-1, 1, DIFF_DV)
    gla_g = gla_norm_g.reshape(-1, 1, GLA_DV)
    hyb_out = hyb_w_out.astype(BF16)
    sb_out = sb_w_out.astype(BF16)

    def ffn(h, layer, j, mix=None):
        idx = layer * n_ffn + j
        return _ffn(h, (ffn_pre, idx), wgu_all, wd_all, idx, (ffn_post, idx), mix)

    for layer in range(depth):
        h = ffn(h, layer, 0)
        if layer % 2 == 0:
            e = layer // 2
            lambda_init = 0.8 - 0.6 * math.exp(-0.3 * layer)
            dq, dk, dvt, gq, gk, la, gv, gg = _proj_even(h, (mix_pre, layer), hyb_w_in[e], gla_w_a2[e], gla_b_a[e])
            a_out = _diff_attention(dq, dk, dvt, (diff_lambda, e), (subln, e), lambda_init, batch, seq)
            b_out = _gla(gq, gk, la, gv, gg, (gla_g, e), batch, seq)
            mix = (a_out, 0, b_out, 0, (hyb_out, e), (mix_post, layer))
        else:
            o = layer // 2
            q, k, vt = _proj_odd(h, (mix_pre, layer), sb_w_qkv[o])
            att = _stick_breaking(q, k, vt, batch, seq)
            mix = (att, 0, att, 1, (sb_out, o), (mix_post, layer))
        h = ffn(h, layer, 1, mix)
    return h.reshape(batch, seq, d)
```

```python
import functools
import math

import jax
import jax.numpy as jnp
from jax import lax
from jax.experimental import pallas as pl
from jax.experimental.pallas import tpu as pltpu

F32 = jnp.float32
BF16 = jnp.bfloat16

EPS = 1e-6
DIFF_HEADS = 4
DIFF_DH = 64
DIFF_DV = 2 * DIFF_DH
GLA_HEADS = 4
GLA_DK = 64
GLA_DV = 128
GLA_RANK = 16
GLA_TAU = 16.0
GLA_CHUNK = 64
GLA_UNROLL = 8
GLA_SUB = 8
SB_HEADS = 8

LANES = 128
VMEM_LIMIT = 56 * 1024 * 1024

TOKEN_TILE = 512
FFN_TILE = 1024
ROW_PART = 256
GLA_TILE = 1024
ATTN_TILE = 256
ATTN_QSUB = 8
SB_QSUB = 16
DIFF_KV_UNROLL = 4
DIFF_KV_GROUP = 2
ONES_ROWS = 16
LOG2_FLUSH = -150.0
LOG2E = math.log2(math.e)


def _cparams(sem):
    return pltpu.CompilerParams(dimension_semantics=sem, vmem_limit_bytes=VMEM_LIMIT)


def _picked(stacked, idx):
    return stacked, pl.BlockSpec((None,) + stacked.shape[1:], lambda *_: (idx,) + (0,) * (stacked.ndim - 1))


def _rms(x, g):
    return x * lax.rsqrt(jnp.mean(x * x, axis=-1, keepdims=True) + EPS) * g


def _dot(a, b):
    return jnp.dot(a, b, preferred_element_type=F32)


def _dot_nt(a, b):
    return lax.dot_general(a, b, (((1,), (1,)), ((), ())), preferred_element_type=F32)


def _dot_tn(a, b):
    return lax.dot_general(a, b, (((0,), (0,)), ((), ())), preferred_element_type=F32)


def _split3(x):
    hi = x.astype(BF16)
    r1 = x - hi.astype(F32)
    mid = r1.astype(BF16)
    lo = (r1 - mid.astype(F32)).astype(BF16)
    return hi, mid, lo


def _neg_abs(x):
    bits = lax.bitcast_convert_type(x, jnp.int32) | jnp.int32(-2 ** 31)
    return lax.bitcast_convert_type(bits, F32)


def _softplus_parts(z):
    return jnp.maximum(z, 0.0), jnp.log(1.0 + jnp.exp(-jnp.abs(z)))


def _ffn_body(hs, gpre_ref, wgu_ref, wd_ref, gpost_ref):
    dff = wd_ref.shape[0]
    xns = [_rms(h, gpre_ref[...]).astype(BF16) for h in hs]
    acts = []
    for xn in xns:
        gu = _dot(xn, wgu_ref[...])
        gate = gu[:, :dff]
        acts.append((gate * jax.nn.sigmoid(gate) * gu[:, dff:]).astype(BF16))
    fs = [_dot(act, wd_ref[...]) for act in acts]
    return [h + 0.5 * _rms(f, gpost_ref[...]) for h, f in zip(hs, fs)]


def _row_parts(tm):
    part = min(ROW_PART, tm)
    return [pl.ds(p * part, part) for p in range(tm // part)]


def _ffn_kernel(h_ref, gpre_ref, wgu_ref, wd_ref, gpost_ref, o_ref):
    parts = _row_parts(h_ref.shape[0])
    outs = _ffn_body([h_ref[r, :] for r in parts], gpre_ref, wgu_ref, wd_ref, gpost_ref)
    for r, o in zip(parts, outs):
        o_ref[r, :] = o


def _mix_ffn_kernel(h_ref, a_ref, b_ref, wo_ref, gmix_ref, gpre_ref, wgu_ref, wd_ref, gpost_ref, o_ref):
    parts = _row_parts(h_ref.shape[0])
    half = a_ref.shape[1]
    ys = [_dot(a_ref[r, :], wo_ref[0:half, :]) + _dot(b_ref[r, :], wo_ref[half:, :]) for r in parts]
    hs = [h_ref[r, :] + _rms(y, gmix_ref[...]) for r, y in zip(parts, ys)]
    outs = _ffn_body(hs, gpre_ref, wgu_ref, wd_ref, gpost_ref)
    for r, o in zip(parts, outs):
        o_ref[r, :] = o


def _ffn(h, g_pre, wgu_all, wd_all, idx, g_post, mix=None):
    n, d = h.shape
    dff = wd_all.shape[1]
    tm = min(FFN_TILE, n)
    row = lambda i: (i, 0)
    g_pre, g_pre_spec = _picked(*g_pre)
    g_post, g_post_spec = _picked(*g_post)
    wgu_all, wgu_spec = _picked(wgu_all, idx)
    wd_all, wd_spec = _picked(wd_all, idx)
    ffn_specs = [g_pre_spec, wgu_spec, wd_spec, g_post_spec]
    ffn_args = (g_pre, wgu_all, wd_all, g_post)
    if mix is None:
        body, specs, args, name = _ffn_kernel, [], (), "ffn"
    else:
        a, a_col, b, b_col, w_out, g_mix = mix
        w_out, w_out_spec = _picked(*w_out)
        g_mix, g_mix_spec = _picked(*g_mix)
        half = w_out.shape[1] // 2
        body, name = _mix_ffn_kernel, "mix_ffn"
        specs = [pl.BlockSpec((tm, half), lambda i: (i, a_col)), pl.BlockSpec((tm, half), lambda i: (i, b_col)),
                 w_out_spec, g_mix_spec]
        args = (a, b, w_out, g_mix)
    return pl.pallas_call(
        body,
        grid=(n // tm,),
        in_specs=[pl.BlockSpec((tm, d), row)] + specs + ffn_specs,
        out_specs=pl.BlockSpec((tm, d), row),
        out_shape=jax.ShapeDtypeStruct((n, d), F32),
        compiler_params=_cparams(("parallel",)),
        name=name,
    )(h, *args, *ffn_args)


def _store_key_blocks(vt_ref, vt):
    tk = vt_ref.shape[2]
    for c in range(vt_ref.shape[0]):
        vt_ref[c] = vt[:, c * tk:(c + 1) * tk].astype(vt_ref.dtype)


def _proj_even_kernel(h_ref, g_ref, w_ref, wvt_ref, ones_ref, wga_ref, wa2_ref, ba_ref,
                      dq_ref, dk_ref, dvt_ref, gq_ref, gk_ref, la_ref, gv_ref, gg_ref):
    tk = dvt_ref.shape[2]
    w_hi, w_lo, _ = _split3(wa2_ref[...])
    widths = [r.shape[1] for r in (dq_ref, dk_ref, gq_ref, gk_ref, gv_ref, gg_ref)]
    starts = [sum(widths[:i]) for i in range(len(widths))]
    parts = _row_parts(h_ref.shape[0])
    ms = [_rms(h_ref[r, :], g_ref[...]).astype(BF16) for r in parts]
    for p, (r, m) in enumerate(zip(parts, ms)):
        def seg(i):
            return _dot(m, w_ref[:, starts[i]:starts[i] + widths[i]])

        dq_ref[r, :] = (seg(0) * (DIFF_DH ** -0.5 * LOG2E)).astype(BF16)
        dk_ref[r, :] = seg(1).astype(BF16)
        vt = _dot_nt(wvt_ref[...], m) + ones_ref[...]
        blocks = vt.shape[1] // tk
        for c in range(blocks):
            dvt_ref[p * blocks + c] = vt[:, c * tk:(c + 1) * tk].astype(dvt_ref.dtype)
        gq_ref[r, :] = seg(2) * GLA_DK ** -0.5
        gk_ref[r, :] = seg(3)
        gv_ref[r, :] = seg(4).astype(BF16)
        gg_ref[r, :] = seg(5)
        ga_hi, ga_lo, _ = _split3(_dot(m, wga_ref[...]))
        x = _dot(ga_hi, w_hi) + _dot(ga_lo, w_hi) + _dot(ga_hi, w_lo) + ba_ref[...]
        relu_neg, log_term = _softplus_parts(-x)
        la_ref[r, :] = -(relu_neg + log_term) * (1.0 / GLA_TAU)


def _proj_even(h, g, w_in, w_a2, b_a):
    n, d = h.shape
    g, g_spec = _picked(*g)
    tm = min(TOKEN_TILE, n)
    wd = DIFF_HEADS * DIFF_DV
    wk = GLA_HEADS * GLA_DK
    c = 0
    parts = []
    for width in (wd, wd, wd, wk, wk, GLA_HEADS * GLA_DV, GLA_HEADS * GLA_DV):
        parts.append(w_in[:, c:c + width])
        c += width
    w_vt = parts.pop(2).T.reshape(DIFF_HEADS, DIFF_DV, d)
    w_vt = jnp.pad(w_vt, ((0, 0), (0, ONES_ROWS), (0, 0))).reshape(-1, d).astype(BF16)
    ones_col = jnp.pad(jnp.zeros((DIFF_HEADS, DIFF_DV, 1), F32), ((0, 0), (0, ONES_ROWS), (0, 0)),
                       constant_values=1.0).reshape(-1, 1)
    vt_rows = w_vt.shape[0]
    w_main = jnp.concatenate(parts, axis=1).astype(BF16)
    tk = min(ATTN_TILE, tm)
    assert min(ROW_PART, tm) % tk == 0
    w_ga = jnp.pad(w_in[:, c:c + GLA_RANK], ((0, 0), (0, LANES - GLA_RANK))).astype(BF16)
    wa2 = jnp.pad(w_a2, ((0, LANES - GLA_RANK), (0, 0)))
    ba = b_a.reshape(1, wk)
    row = lambda i: (i, 0)
    fixed = lambda i: (0, 0)
    out_bf = jax.ShapeDtypeStruct((n, wd), BF16)
    out_f = jax.ShapeDtypeStruct((n, wd), F32)
    out_k = jax.ShapeDtypeStruct((n, wk), F32)
    out_vt = jax.ShapeDtypeStruct((n // tk, vt_rows, tk), BF16)
    rowspec = pl.BlockSpec((tm, wd), row)
    kspec = pl.BlockSpec((tm, wk), row)
    vtspec = pl.BlockSpec((tm // tk, vt_rows, tk), lambda i: (i, 0, 0))
    return pl.pallas_call(
        _proj_even_kernel,
        grid=(n // tm,),
        in_specs=[pl.BlockSpec((tm, d), row), g_spec,
                  pl.BlockSpec(w_main.shape, fixed), pl.BlockSpec(w_vt.shape, fixed),
                  pl.BlockSpec(ones_col.shape, fixed), pl.BlockSpec(w_ga.shape, fixed),
                  pl.BlockSpec(wa2.shape, fixed), pl.BlockSpec((1, wk), fixed)],
        out_specs=[rowspec, rowspec, vtspec, kspec, kspec, kspec, rowspec, rowspec],
        out_shape=[out_bf, out_bf, out_vt, out_k, out_k, out_k, out_bf, out_f],
        compiler_params=_cparams(("parallel",)),
        name="proj_even",
    )(h, g, w_main, w_vt, ones_col, w_ga, wa2, ba)


def _proj_odd_kernel(h_ref, g_ref, w_ref, wvt_ref, q_ref, k_ref, vt_ref):
    m = _rms(h_ref[...], g_ref[...]).astype(BF16)
    d = q_ref.shape[1]
    dh = d // SB_HEADS
    q_ref[...] = (_dot(m, w_ref[:, 0:d]) * (dh ** -0.5 * LOG2E)).astype(BF16)
    k_ref[...] = _dot(m, w_ref[:, d:2 * d]).astype(BF16)
    _store_key_blocks(vt_ref, _dot_nt(wvt_ref[...], m))


def _proj_odd(h, g, w_qkv):
    n, d = h.shape
    g, g_spec = _picked(*g)
    tm = min(TOKEN_TILE, n)
    tk = min(ATTN_TILE, tm)
    row = lambda i: (i, 0)
    fixed = lambda i: (0, 0)
    out = jax.ShapeDtypeStruct((n, d), BF16)
    w_qk = w_qkv[:, :2 * d].astype(BF16)
    w_vt = w_qkv[:, 2 * d:].T.astype(BF16)
    return pl.pallas_call(
        _proj_odd_kernel,
        grid=(n // tm,),
        in_specs=[pl.BlockSpec((tm, d), row), g_spec,
                  pl.BlockSpec(w_qk.shape, fixed), pl.BlockSpec(w_vt.shape, fixed)],
        out_specs=[pl.BlockSpec((tm, d), row), pl.BlockSpec((tm, d), row),
                   pl.BlockSpec((tm // tk, d, tk), lambda i: (i, 0, 0))],
        out_shape=[out, out, jax.ShapeDtypeStruct((n // tk, d, tk), BF16)],
        compiler_params=_cparams(("parallel",)),
        name="proj_odd",
    )(h, g, w_qk, w_vt)


def _diff_kernel(lam_ref, g_ref, q_ref, k_ref, vt_ref, o_ref, m_ref, l_ref, acc_ref, *, lambda_init):
    i = pl.program_id(2)
    tk = vt_ref.shape[2]
    dv = DIFF_DV
    nsub = q_ref.shape[0] // tk
    base = i * nsub
    lane = lax.broadcasted_iota(jnp.int32, (tk, LANES), 1)
    causal = (lax.broadcasted_iota(jnp.int32, (tk, tk), 0) <= lax.broadcasted_iota(jnp.int32, (tk, tk), 1))

    def q_map(a, mi):
        q = q_ref[a * tk:(a + 1) * tk, :]
        keep = (lane < DIFF_DH) if mi == 0 else (lane >= DIFF_DH)
        return jnp.where(keep, q, jnp.zeros_like(q))

    def run_tiles(tiles, scores):
        pending = []
        for (a, mi, vtbs, first, masked), ss in zip(tiles, scores):
            if masked:
                ss = [jnp.where(causal, s, -jnp.inf) for s in ss]
            m_blk = functools.reduce(jnp.maximum, [jnp.max(s, axis=0, keepdims=True) for s in ss])
            if first:
                alpha = None
                m_new = m_blk
            else:
                m_prev = m_ref[a, mi]
                m_new = jnp.maximum(m_prev, m_blk)
                alpha = jnp.exp2(m_prev - m_new)
            m_ref[a, mi] = m_new
            p = jnp.concatenate([jnp.exp2(s - m_new).astype(BF16) for s in ss], axis=0)
            vtb = jnp.concatenate(vtbs, axis=1)
            pending.append((a, mi, alpha, _dot(vtb, p)))
        for a, mi, alpha, pv in pending:
            if alpha is None:
                acc_ref[a, mi] = pv[:dv]
                l_ref[a, mi] = pv[dv:dv + 1]
            else:
                acc_ref[a, mi] = alpha * acc_ref[a, mi] + pv[:dv]
                l_ref[a, mi] = alpha * l_ref[a, mi] + pv[dv:dv + 1]

    def keys(j):
        return k_ref[pl.ds(pl.multiple_of(j * tk, tk), tk), :]

    band, band_scores = [], []
    for c in range(nsub):
        kb, vtb = keys(base + c), vt_ref[base + c]
        for a in range(c, nsub):
            for mi in range(2):
                band.append((a, mi, [vtb], c == 0, a == c))
                band_scores.append([_dot_nt(kb, q_map(a, mi))])
    run_tiles(band, band_scores)

    kv_unroll = DIFF_KV_UNROLL if nsub % DIFF_KV_UNROLL == 0 else 1
    group = DIFF_KV_GROUP if kv_unroll % DIFF_KV_GROUP == 0 else 1

    def body(step, carry):
        tiles, scores = [], []
        for u in range(0, kv_unroll, group):
            blocks = [step * kv_unroll + u + w for w in range(group)]
            kbs = [keys(j) for j in blocks]
            vtbs = [vt_ref[j] for j in blocks]
            for a in range(nsub):
                for mi in range(2):
                    tiles.append((a, mi, vtbs, False, False))
                    scores.append([_dot_nt(kb, q_map(a, mi)) for kb in kbs])
        run_tiles(tiles, scores)
        return carry

    lax.fori_loop(0, base // kv_unroll, body, 0)

    lp = lam_ref[...]
    lam = (jnp.exp(jnp.sum(lp[0:1] * lp[1:2], axis=-1, keepdims=True))
           - jnp.exp(jnp.sum(lp[2:3] * lp[3:4], axis=-1, keepdims=True)) + lambda_init)
    for a in range(nsub):
        o = acc_ref[a, 0] / l_ref[a, 0] - lam * (acc_ref[a, 1] / l_ref[a, 1])
        o = o * lax.rsqrt(jnp.mean(o * o, axis=0, keepdims=True) + EPS) * (1.0 - lambda_init)
        o_ref[a * tk:(a + 1) * tk, :] = (o.T * g_ref[...]).astype(o_ref.dtype)


def _diff_attention(dq, dk, dvt, lam_params, subln_g, lambda_init, batch, seq):
    n, wd = dq.shape
    lam_params, lam_spec = _picked(*lam_params)
    subln_g, g_spec = _picked(*subln_g)
    tk = dvt.shape[2]
    tq = min(ATTN_QSUB * tk, seq)
    nq = seq // tq
    nsub = tq // tk
    qspec = pl.BlockSpec((tq, LANES), lambda b, h, i: (b * nq + i, h))
    kspec = pl.BlockSpec((seq, LANES), lambda b, h, i: (b, h))
    vtspec = pl.BlockSpec((seq // tk, DIFF_DV + ONES_ROWS, tk), lambda b, h, i: (b, h, 0))
    return pl.pallas_call(
        functools.partial(_diff_kernel, lambda_init=lambda_init),
        grid=(batch, DIFF_HEADS, nq),
        in_specs=[lam_spec, g_spec, qspec, kspec, vtspec],
        out_specs=qspec,
        out_shape=jax.ShapeDtypeStruct((n, wd), BF16),
        scratch_shapes=[pltpu.VMEM((nsub, 2, 1, tk), F32), pltpu.VMEM((nsub, 2, 1, tk), F32),
                        pltpu.VMEM((nsub, 2, DIFF_DV, tk), F32)],
        compiler_params=_cparams(("parallel", "parallel", "arbitrary")),
        name="diff_attn",
    )(lam_params, subln_g, dq, dk, dvt)


def _gla_kernel(q_ref, k_ref, la_ref, v_ref, gg_ref, g_ref, o_ref, state_ref):
    c = GLA_CHUNK
    sub = GLA_SUB
    nsub = c // sub
    heads = state_ref.shape[0]
    pair_lane = lax.broadcasted_iota(jnp.int32, (c, LANES), 1)
    pair_lane3 = lax.broadcasted_iota(jnp.int32, (nsub, sub, LANES), 2)
    own = [pair_lane < GLA_DK, pair_lane >= GLA_DK]
    own3 = [pair_lane3 < GLA_DK, pair_lane3 >= GLA_DK]

    @pl.when(pl.program_id(1) == 0)
    def _():
        state_ref[...] = jnp.zeros(state_ref.shape, F32)

    r_i = lax.broadcasted_iota(jnp.int32, (c, c), 0)
    c_i = lax.broadcasted_iota(jnp.int32, (c, c), 1)
    tril = (c_i <= r_i).astype(BF16)
    levels = []
    s = c // 2
    while s >= sub:
        levels.append((s, (r_i // (2 * s) == c_i // (2 * s)) & (r_i % (2 * s) >= s) & (c_i % (2 * s) < s)))
        s //= 2
    diag_mask = (r_i // sub == c_i // sub) & (c_i <= r_i)
    lane3 = lax.broadcasted_iota(jnp.int32, (nsub, sub, c), 2)
    blk3 = lax.broadcasted_iota(jnp.int32, (nsub, sub, c), 0)
    g = g_ref[...]

    def step(si, carry):
        cols = [slice(h * LANES, (h + 1) * LANES) for h in range(heads)]
        rows = [pl.ds(pl.multiple_of((si * GLA_UNROLL + u) * c, c), c) for u in range(GLA_UNROLL)]

        pairs = [slice(p * LANES, (p + 1) * LANES) for p in range(heads // 2)]

        cums = {}
        for u in range(GLA_UNROLL):
            for p in range(heads // 2):
                la_hi, la_mid, la_lo = _split3(la_ref[rows[u], pairs[p]])
                cums[u, p] = _dot(tril, la_hi) + _dot(tril, la_mid) + _dot(tril, la_lo)

        states = [state_ref[h] for h in range(heads)]
        mids = {}
        for u in range(GLA_UNROLL):
            for p in range(heads // 2):
                q = q_ref[rows[u], pairs[p]]
                k = k_ref[rows[u], pairs[p]]
                cum = cums[u, p]
                last = cum[c - 1:c, :]
                q_dec = (q * jnp.exp(cum)).astype(BF16)
                k_dec = k * jnp.exp(last - cum)
                state_decay = jnp.exp(last)

                level_ops = []
                for s, _ in levels:
                    ref = jnp.concatenate([jnp.broadcast_to(cum[lo + s:lo + s + 1, :], (2 * s, LANES))
                                           for lo in range(0, c, 2 * s)], axis=0)
                    qs = q * jnp.exp(jnp.minimum(cum - ref, 0.0))
                    ks = k * jnp.exp(jnp.minimum(ref - cum, 0.0))
                    level_ops.append((qs, ks.astype(BF16)))

                q3 = q.reshape(nsub, sub, LANES)
                k3 = k.reshape(nsub, sub, LANES)
                c3 = cum.reshape(nsub, sub, LANES)
                a3 = [jnp.zeros((nsub, sub, c), F32) for _ in range(2)]
                for j in range(sub):
                    dec = jnp.exp(jnp.minimum(c3 - c3[:, j:j + 1, :], 0.0))
                    prod = q3 * k3[:, j:j + 1, :] * dec
                    for t in range(2):
                        col = jnp.sum(jnp.where(own3[t], prod, 0.0), axis=-1, keepdims=True)
                        a3[t] = jnp.where(lane3 == blk3 * sub + j, col, a3[t])

                for t in range(2):
                    h = 2 * p + t
                    v = v_ref[rows[u], cols[h]]
                    state = states[h]
                    o_inter = _dot_nt(q_dec, state.astype(BF16))
                    states[h] = state * state_decay + _dot_tn(v, jnp.where(own[t], k_dec, 0.0).astype(BF16))
                    attn = jnp.where(diag_mask, a3[t].reshape(c, c), 0.0)
                    for (_, mask), (qs, ks) in zip(levels, level_ops):
                        a_level = _dot_nt(jnp.where(own[t], qs, 0.0).astype(BF16), ks)
                        attn = jnp.where(mask, a_level, attn)
                    mids[u, h] = (o_inter, _dot(attn.astype(BF16), v))

        for h in range(heads):
            state_ref[h] = states[h]
        for u in range(GLA_UNROLL):
            for h in range(heads):
                o_inter, o_intra = mids[u, h]
                gate = gg_ref[rows[u], cols[h]]
                o_ref[rows[u], cols[h]] = (_rms(o_inter + o_intra, g)
                                           * (gate * jax.nn.sigmoid(gate))).astype(o_ref.dtype)
        return carry

    lax.fori_loop(0, q_ref.shape[0] // (c * GLA_UNROLL), step, 0)


def _gla(gq, gk, la, gv, gg, norm_g, batch, seq):
    n, wd = gv.shape
    norm_g, g_spec = _picked(*norm_g)
    t = min(GLA_TILE, seq)
    nt = seq // t
    spec = pl.BlockSpec((t, wd), lambda b, i: (b * nt + i, 0))
    kspec = pl.BlockSpec((t, gq.shape[1]), lambda b, i: (b * nt + i, 0))
    return pl.pallas_call(
        _gla_kernel,
        grid=(batch, nt),
        in_specs=[kspec, kspec, kspec, spec, spec, g_spec],
        out_specs=spec,
        out_shape=jax.ShapeDtypeStruct((n, wd), BF16),
        scratch_shapes=[pltpu.VMEM((GLA_HEADS, GLA_DV, LANES), F32)],
        compiler_params=_cparams(("parallel", "arbitrary")),
        name="gla",
    )(gq, gk, la, gv, gg, norm_g)


SUM_ROWS = 8


def _sb_kernel(q_ref, k_ref, vt_ref, o_ref, acc_ref):
    i = pl.program_id(2)
    tk = vt_ref.shape[2]
    nsub = q_ref.shape[0] // tk
    base = i * nsub
    r_s = lax.broadcasted_iota(jnp.int32, (tk + SUM_ROWS, tk), 0)
    c_s = lax.broadcasted_iota(jnp.int32, (tk + SUM_ROWS, tk), 1)
    suffix_mat = ((c_s > r_s) | (r_s >= tk)).astype(BF16)
    strict = (lax.broadcasted_iota(jnp.int32, (tk, tk), 0) < lax.broadcasted_iota(jnp.int32, (tk, tk), 1))

    def run_tiles(tiles, later):
        later = list(later)
        logits = [_dot_nt(kb, q_ref[a * tk:(a + 1) * tk, :]) for a, kb, _, _ in tiles]
        mids = []
        for (a, _, _, masked), z in zip(tiles, logits):
            log_term = jnp.log2(1.0 + jnp.exp2(_neg_abs(z)))
            log_beta = jnp.minimum(z, 0.0) - log_term
            log_keep = log_beta - z
            if masked:
                log_keep = jnp.where(strict, log_keep, 0.0)
            mids.append((log_beta, _dot(suffix_mat, log_keep.astype(BF16))))
        pending = []
        for (a, _, vtb, masked), (log_beta, sums) in zip(tiles, mids):
            arg = log_beta + sums[:tk]
            if later[a] is not None:
                arg = arg + later[a]
            att = jnp.exp2(arg)
            if masked:
                att = jnp.where(strict, att, 0.0)
            pending.append((a, later[a] is None, _dot(vtb, att.astype(BF16))))
            later[a] = sums[tk:tk + 1] if later[a] is None else later[a] + sums[tk:tk + 1]
        for a, first, contrib in pending:
            if first:
                acc_ref[a] = contrib
            else:
                acc_ref[a] += contrib
        return tuple(later)

    def key_block(j):
        return k_ref[pl.ds(pl.multiple_of(j * tk, tk), tk), :], vt_ref[j]

    def retire(later_a, still_valid):
        return jnp.where(still_valid, later_a, -jnp.inf)

    def lockstep_tiles(offset, later):
        tiles, kept = [], []
        for a in range(nsub):
            j = base + a - offset
            kb, vtb = key_block(jnp.maximum(j, 0))
            tiles.append((a, kb, vtb, False))
            kept.append(retire(later[a], j >= 0))
        return tiles, kept

    diag = []
    for a in range(nsub):
        kb, vtb = key_block(base + a)
        diag.append((a, kb, vtb, True))
    later = run_tiles(diag, [None] * nsub)
    later = run_tiles(*lockstep_tiles(1, later))

    def alive(later):
        return (jnp.max(functools.reduce(jnp.maximum, later)) > LOG2_FLUSH).astype(jnp.int32)

    def cond(carry):
        step, live, _ = carry
        return (step < base + nsub - 2) & (live > 0)

    def body(carry):
        step, _, later = carry
        later = run_tiles(*lockstep_tiles(step + 2, later))
        return step + 1, alive(later), later

    lax.while_loop(cond, body, (jnp.int32(0), alive(later), later))
    for a in range(nsub):
        o_ref[a * tk:(a + 1) * tk, :] = acc_ref[a].T.astype(o_ref.dtype)


def _stick_breaking(q, k, vt, batch, seq):
    n, d = q.shape
    tk = vt.shape[2]
    tq = min(SB_QSUB * tk, seq)
    nq = seq // tq
    qspec = pl.BlockSpec((tq, LANES), lambda b, h, i: (b * nq + i, h))
    kspec = pl.BlockSpec((seq, LANES), lambda b, h, i: (b, h))
    vtspec = pl.BlockSpec((seq // tk, LANES, tk), lambda b, h, i: (b, h, 0))
    return pl.pallas_call(
        _sb_kernel,
        grid=(batch, SB_HEADS, nq),
        in_specs=[qspec, kspec, vtspec],
        out_specs=qspec,
        out_shape=jax.ShapeDtypeStruct((n, d), BF16),
        scratch_shapes=[pltpu.VMEM((tq // tk, LANES, tk), F32)],
        compiler_params=_cparams(("parallel", "parallel", "arbitrary")),
        name="stick_breaking",
    )(q, k, vt)


def kernel(x, ffn_pre_g, ffn_post_g, ffn_w_gate, ffn_w_up, ffn_w_down, mix_pre_g, mix_post_g, hyb_w_in, hyb_w_out, diff_lambda, diff_subln_g, gla_w_a2, gla_b_a, gla_norm_g, sb_w_qkv, sb_w_out):
    batch, seq, d = x.shape
    depth = ffn_pre_g.shape[0]
    assert d // SB_HEADS == LANES and DIFF_DV == LANES and GLA_DV == LANES
    h = x.reshape(batch * seq, d)

    n_ffn = ffn_w_gate.shape[1]
    dff = ffn_w_gate.shape[-1]
    wgu_all = jnp.concatenate([ffn_w_gate, ffn_w_up], axis=-1).astype(BF16).reshape(depth * n_ffn, d, 2 * dff)
    wd_all = ffn_w_down.astype(BF16).reshape(depth * n_ffn, dff, d)

    ffn_pre = ffn_pre_g.reshape(depth * n_ffn, 1, d)
    ffn_post = ffn_post_g.reshape(depth * n_ffn, 1, d)
    mix_pre = mix_pre_g.reshape(depth, 1, d)
    mix_post = mix_post_g.reshape(depth, 1, d)
    subln = diff_subln_g.reshape(-1, 1, DIFF_DV)
    gla_g = gla_norm_g.reshape(-1, 1, GLA_DV)
    hyb_out = hyb_w_out.astype(BF16)
    sb_out = sb_w_out.astype(BF16)

    def ffn(h, layer, j, mix=None):
        idx = layer * n_ffn + j
        return _ffn(h, (ffn_pre, idx), wgu_all, wd_all, idx, (ffn_post, idx), mix)

    for layer in range(depth):
        h = ffn(h, layer, 0)
        if layer % 2 == 0:
            e = layer // 2
            lambda_init = 0.8 - 0.6 * math.exp(-0.3 * layer)
            dq, dk, dvt, gq, gk, la, gv, gg = _proj_even(h, (mix_pre, layer), hyb_w_in[e], gla_w_a2[e], gla_b_a[e])
            a_out = _diff_attention(dq, dk, dvt, (diff_lambda, e), (subln, e), lambda_init, batch, seq)
            b_out = _gla(gq, gk, la, gv, gg, (gla_g, e), batch, seq)
            mix = (a_out, 0, b_out, 0, (hyb_out, e), (mix_post, layer))
        else:
            o = layer // 2
            q, k, vt = _proj_odd(h, (mix_pre, layer), sb_w_qkv[o])
            att = _stick_breaking(q, k, vt, batch, seq)
            mix = (att, 0, att, 1, (sb_out, o), (mix_post, layer))
        h = ffn(h, layer, 1, mix)
    return h.reshape(batch, seq, d)
```

```python
import functools
import math

import jax
import jax.numpy as jnp
from jax import lax
from jax.experimental import pallas as pl
from jax.experimental.pallas import tpu as pltpu

F32 = jnp.float32
BF16 = jnp.bfloat16

EPS = 1e-6
DIFF_HEADS = 4
DIFF_DH = 64
DIFF_DV = 2 * DIFF_DH
GLA_HEADS = 4
GLA_DK = 64
GLA_DV = 128
GLA_RANK = 16
GLA_TAU = 16.0
GLA_CHUNK = 64
GLA_UNROLL = 8
GLA_SUB = 8
SB_HEADS = 8

LANES = 128
VMEM_LIMIT = 56 * 1024 * 1024

TOKEN_TILE = 1024
FFN_TILE = 1024
ROW_PART = 256
GLA_TILE = 1024
ATTN_TILE = 256
ATTN_QSUB = 8
SB_QSUB = 16
DIFF_KV_UNROLL = 8
DIFF_RUN_BLOCKS = 4
DIFF_KV_GROUP = 2
ONES_ROWS = 16
LOG2_FLUSH = -150.0
LOG2E = math.log2(math.e)


def _cparams(sem):
    return pltpu.CompilerParams(dimension_semantics=sem, vmem_limit_bytes=VMEM_LIMIT)


def _picked(stacked, idx):
    return stacked, pl.BlockSpec((None,) + stacked.shape[1:], lambda *_: (idx,) + (0,) * (stacked.ndim - 1))


def _rms(x, g):
    return x * lax.rsqrt(jnp.mean(x * x, axis=-1, keepdims=True) + EPS) * g


def _dot(a, b):
    return jnp.dot(a, b, preferred_element_type=F32)


def _dot_nt(a, b):
    return lax.dot_general(a, b, (((1,), (1,)), ((), ())), preferred_element_type=F32)


def _dot_tn(a, b):
    return lax.dot_general(a, b, (((0,), (0,)), ((), ())), preferred_element_type=F32)


def _split3(x):
    hi = x.astype(BF16)
    r1 = x - hi.astype(F32)
    mid = r1.astype(BF16)
    lo = (r1 - mid.astype(F32)).astype(BF16)
    return hi, mid, lo


def _neg_abs(x):
    bits = lax.bitcast_convert_type(x, jnp.int32) | jnp.int32(-2 ** 31)
    return lax.bitcast_convert_type(bits, F32)


def _softplus_parts(z):
    return jnp.maximum(z, 0.0), jnp.log(1.0 + jnp.exp(-jnp.abs(z)))


def _ffn_body(hs, gpre_ref, wgu_ref, wd_ref, gpost_ref):
    dff = wd_ref.shape[0]
    xns = [_rms(h, gpre_ref[...]).astype(BF16) for h in hs]
    acts = []
    for xn in xns:
        gu = _dot(xn, wgu_ref[...])
        gate = gu[:, :dff]
        acts.append((gate * jax.nn.sigmoid(gate) * gu[:, dff:]).astype(BF16))
    fs = [_dot(act, wd_ref[...]) for act in acts]
    return [h + 0.5 * _rms(f, gpost_ref[...]) for h, f in zip(hs, fs)]


def _row_parts(tm):
    part = min(ROW_PART, tm)
    return [pl.ds(p * part, part) for p in range(tm // part)]


def _ffn_kernel(h_ref, gpre_ref, wgu_ref, wd_ref, gpost_ref, o_ref):
    parts = _row_parts(h_ref.shape[0])
    outs = _ffn_body([h_ref[r, :] for r in parts], gpre_ref, wgu_ref, wd_ref, gpost_ref)
    for r, o in zip(parts, outs):
        o_ref[r, :] = o


def _mix_ffn_kernel(h_ref, a_ref, b_ref, wo_ref, gmix_ref, gpre_ref, wgu_ref, wd_ref, gpost_ref, o_ref):
    parts = _row_parts(h_ref.shape[0])
    half = a_ref.shape[1]
    ys = [_dot(a_ref[r, :], wo_ref[0:half, :]) + _dot(b_ref[r, :], wo_ref[half:, :]) for r in parts]
    hs = [h_ref[r, :] + _rms(y, gmix_ref[...]) for r, y in zip(parts, ys)]
    outs = _ffn_body(hs, gpre_ref, wgu_ref, wd_ref, gpost_ref)
    for r, o in zip(parts, outs):
        o_ref[r, :] = o


def _ffn(h, g_pre, wgu_all, wd_all, idx, g_post, mix=None):
    n, d = h.shape
    dff = wd_all.shape[1]
    tm = min(FFN_TILE, n)
    row = lambda i: (i, 0)
    g_pre, g_pre_spec = _picked(*g_pre)
    g_post, g_post_spec = _picked(*g_post)
    wgu_all, wgu_spec = _picked(wgu_all, idx)
    wd_all, wd_spec = _picked(wd_all, idx)
    ffn_specs = [g_pre_spec, wgu_spec, wd_spec, g_post_spec]
    ffn_args = (g_pre, wgu_all, wd_all, g_post)
    if mix is None:
        body, specs, args, name = _ffn_kernel, [], (), "ffn"
    else:
        a, a_col, b, b_col, w_out, g_mix = mix
        w_out, w_out_spec = _picked(*w_out)
        g_mix, g_mix_spec = _picked(*g_mix)
        half = w_out.shape[1] // 2
        body, name = _mix_ffn_kernel, "mix_ffn"
        specs = [pl.BlockSpec((tm, half), lambda i: (i, a_col)), pl.BlockSpec((tm, half), lambda i: (i, b_col)),
                 w_out_spec, g_mix_spec]
        args = (a, b, w_out, g_mix)
    return pl.pallas_call(
        body,
        grid=(n // tm,),
        in_specs=[pl.BlockSpec((tm, d), row)] + specs + ffn_specs,
        out_specs=pl.BlockSpec((tm, d), row),
        out_shape=jax.ShapeDtypeStruct((n, d), F32),
        compiler_params=_cparams(("parallel",)),
        name=name,
    )(h, *args, *ffn_args)


def _store_key_blocks(vt_ref, vt, part):
    tk = vt_ref.shape[2]
    blocks = vt.shape[1] // tk
    for c in range(blocks):
        vt_ref[part * blocks + c] = vt[:, c * tk:(c + 1) * tk].astype(vt_ref.dtype)


def _proj_even_kernel(h_ref, g_ref, w_ref, wvt_ref, ones_ref, wga_ref, wa2_ref, ba_ref,
                      dq_ref, dk_ref, dvt_ref, gq_ref, gk_ref, la_ref, gv_ref, gg_ref):
    w_hi, w_lo, _ = _split3(wa2_ref[...])
    widths = [r.shape[1] for r in (dq_ref, dk_ref, gq_ref, gk_ref, gv_ref, gg_ref)]
    starts = [sum(widths[:i]) for i in range(len(widths))]
    parts = _row_parts(h_ref.shape[0])
    ms = [_rms(h_ref[r, :], g_ref[...]).astype(BF16) for r in parts]
    for p, (r, m) in enumerate(zip(parts, ms)):
        def seg(i):
            return _dot(m, w_ref[:, starts[i]:starts[i] + widths[i]])

        dq_ref[r, :] = (seg(0) * (DIFF_DH ** -0.5 * LOG2E)).astype(BF16)
        dk_ref[r, :] = seg(1).astype(BF16)
        _store_key_blocks(dvt_ref, _dot_nt(wvt_ref[...], m) + ones_ref[...], p)
        gq_ref[r, :] = seg(2) * GLA_DK ** -0.5
        gk_ref[r, :] = seg(3)
        gv_ref[r, :] = seg(4).astype(BF16)
        gg_ref[r, :] = seg(5)
        ga_hi, ga_lo, _ = _split3(_dot(m, wga_ref[...]))
        x = _dot(ga_hi, w_hi) + _dot(ga_lo, w_hi) + _dot(ga_hi, w_lo) + ba_ref[...]
        relu_neg, log_term = _softplus_parts(-x)
        la_ref[r, :] = -(relu_neg + log_term) * (1.0 / GLA_TAU)


def _proj_even(h, g, w_in, w_a2, b_a):
    n, d = h.shape
    g, g_spec = _picked(*g)
    tm = min(TOKEN_TILE, n)
    wd = DIFF_HEADS * DIFF_DV
    wk = GLA_HEADS * GLA_DK
    c = 0
    parts = []
    for width in (wd, wd, wd, wk, wk, GLA_HEADS * GLA_DV, GLA_HEADS * GLA_DV):
        parts.append(w_in[:, c:c + width])
        c += width
    w_vt = parts.pop(2).T.reshape(DIFF_HEADS, DIFF_DV, d)
    w_vt = jnp.pad(w_vt, ((0, 0), (0, ONES_ROWS), (0, 0))).reshape(-1, d).astype(BF16)
    ones_col = jnp.pad(jnp.zeros((DIFF_HEADS, DIFF_DV, 1), F32), ((0, 0), (0, ONES_ROWS), (0, 0)),
                       constant_values=1.0).reshape(-1, 1)
    vt_rows = w_vt.shape[0]
    w_main = jnp.concatenate(parts, axis=1).astype(BF16)
    tk = min(ATTN_TILE, tm)
    assert min(ROW_PART, tm) % tk == 0
    w_ga = jnp.pad(w_in[:, c:c + GLA_RANK], ((0, 0), (0, LANES - GLA_RANK))).astype(BF16)
    wa2 = jnp.pad(w_a2, ((0, LANES - GLA_RANK), (0, 0)))
    ba = b_a.reshape(1, wk)
    row = lambda i: (i, 0)
    fixed = lambda i: (0, 0)
    out_bf = jax.ShapeDtypeStruct((n, wd), BF16)
    out_f = jax.ShapeDtypeStruct((n, wd), F32)
    out_k = jax.ShapeDtypeStruct((n, wk), F32)
    out_vt = jax.ShapeDtypeStruct((n // tk, vt_rows, tk), BF16)
    rowspec = pl.BlockSpec((tm, wd), row)
    kspec = pl.BlockSpec((tm, wk), row)
    vtspec = pl.BlockSpec((tm // tk, vt_rows, tk), lambda i: (i, 0, 0))
    return pl.pallas_call(
        _proj_even_kernel,
        grid=(n // tm,),
        in_specs=[pl.BlockSpec((tm, d), row), g_spec,
                  pl.BlockSpec(w_main.shape, fixed), pl.BlockSpec(w_vt.shape, fixed),
                  pl.BlockSpec(ones_col.shape, fixed), pl.BlockSpec(w_ga.shape, fixed),
                  pl.BlockSpec(wa2.shape, fixed), pl.BlockSpec((1, wk), fixed)],
        out_specs=[rowspec, rowspec, vtspec, kspec, kspec, kspec, rowspec, rowspec],
        out_shape=[out_bf, out_bf, out_vt, out_k, out_k, out_k, out_bf, out_f],
        compiler_params=_cparams(("parallel",)),
        name="proj_even",
    )(h, g, w_main, w_vt, ones_col, w_ga, wa2, ba)


def _proj_odd_kernel(h_ref, g_ref, w_ref, wvt_ref, q_ref, k_ref, vt_ref):
    d = q_ref.shape[1]
    dh = d // SB_HEADS
    parts = _row_parts(h_ref.shape[0])
    ms = [_rms(h_ref[r, :], g_ref[...]).astype(BF16) for r in parts]
    for p, (r, m) in enumerate(zip(parts, ms)):
        q_ref[r, :] = (_dot(m, w_ref[:, 0:d]) * (dh ** -0.5 * LOG2E)).astype(BF16)
        k_ref[r, :] = _dot(m, w_ref[:, d:2 * d]).astype(BF16)
        _store_key_blocks(vt_ref, _dot_nt(wvt_ref[...], m), p)


def _proj_odd(h, g, w_qkv):
    n, d = h.shape
    g, g_spec = _picked(*g)
    tm = min(TOKEN_TILE, n)
    tk = min(ATTN_TILE, tm)
    row = lambda i: (i, 0)
    fixed = lambda i: (0, 0)
    out = jax.ShapeDtypeStruct((n, d), BF16)
    w_qk = w_qkv[:, :2 * d].astype(BF16)
    w_vt = w_qkv[:, 2 * d:].T.astype(BF16)
    return pl.pallas_call(
        _proj_odd_kernel,
        grid=(n // tm,),
        in_specs=[pl.BlockSpec((tm, d), row), g_spec,
                  pl.BlockSpec(w_qk.shape, fixed), pl.BlockSpec(w_vt.shape, fixed)],
        out_specs=[pl.BlockSpec((tm, d), row), pl.BlockSpec((tm, d), row),
                   pl.BlockSpec((tm // tk, d, tk), lambda i: (i, 0, 0))],
        out_shape=[out, out, jax.ShapeDtypeStruct((n // tk, d, tk), BF16)],
        compiler_params=_cparams(("parallel",)),
        name="proj_odd",
    )(h, g, w_qk, w_vt)


def _diff_kernel(lam_ref, g_ref, q_ref, k_ref, vt_ref, o_ref, m_ref, l_ref, acc_ref, *, lambda_init):
    i = pl.program_id(2)
    tk = vt_ref.shape[2]
    dv = DIFF_DV
    nsub = q_ref.shape[0] // tk
    base = i * nsub
    lane = lax.broadcasted_iota(jnp.int32, (tk, LANES), 1)
    causal = (lax.broadcasted_iota(jnp.int32, (tk, tk), 0) <= lax.broadcasted_iota(jnp.int32, (tk, tk), 1))

    def q_map(a, mi):
        q = q_ref[a * tk:(a + 1) * tk, :]
        keep = (lane < DIFF_DH) if mi == 0 else (lane >= DIFF_DH)
        return jnp.where(keep, q, jnp.zeros_like(q))

    def run_tiles(tiles, scores):
        pending = []
        for (a, mi, vtbs, first, masked), ss in zip(tiles, scores):
            if masked:
                ss = [jnp.where(causal, s, -jnp.inf) for s in ss]
            m_blk = functools.reduce(jnp.maximum, [jnp.max(s, axis=0, keepdims=True) for s in ss])
            if first:
                alpha = None
                m_new = m_blk
            else:
                m_prev = m_ref[a, mi]
                m_new = jnp.maximum(m_prev, m_blk)
                alpha = jnp.exp2(m_prev - m_new)
            m_ref[a, mi] = m_new
            p = jnp.concatenate([jnp.exp2(s - m_new).astype(BF16) for s in ss], axis=0)
            vtb = jnp.concatenate(vtbs, axis=1)
            pending.append((a, mi, alpha, _dot(vtb, p)))
        for a, mi, alpha, pv in pending:
            if alpha is None:
                acc_ref[a, mi] = pv[:dv]
                l_ref[a, mi] = pv[dv:dv + 1]
            else:
                acc_ref[a, mi] = alpha * acc_ref[a, mi] + pv[:dv]
                l_ref[a, mi] = alpha * l_ref[a, mi] + pv[dv:dv + 1]

    def keys(j):
        return k_ref[pl.ds(pl.multiple_of(j * tk, tk), tk), :]

    band, band_scores = [], []
    for c in range(nsub):
        kb, vtb = keys(base + c), vt_ref[base + c]
        for a in range(c, nsub):
            for mi in range(2):
                band.append((a, mi, [vtb], c == 0, a == c))
                band_scores.append([_dot_nt(kb, q_map(a, mi))])
    run_tiles(band, band_scores)

    kv_unroll = DIFF_KV_UNROLL if nsub % DIFF_KV_UNROLL == 0 else 1
    group = DIFF_KV_GROUP if kv_unroll % DIFF_KV_GROUP == 0 else 1

    def body(step, carry):
        for r in range(0, kv_unroll, DIFF_RUN_BLOCKS):
            tiles, scores = [], []
            for u in range(r, min(r + DIFF_RUN_BLOCKS, kv_unroll), group):
                blocks = [step * kv_unroll + u + w for w in range(group)]
                kbs = [keys(j) for j in blocks]
                vtbs = [vt_ref[j] for j in blocks]
                for a in range(nsub):
                    for mi in range(2):
                        tiles.append((a, mi, vtbs, False, False))
                        scores.append([_dot_nt(kb, q_map(a, mi)) for kb in kbs])
            run_tiles(tiles, scores)
        return carry

    lax.fori_loop(0, base // kv_unroll, body, 0)

    lp = lam_ref[...]
    lam = (jnp.exp(jnp.sum(lp[0:1] * lp[1:2], axis=-1, keepdims=True))
           - jnp.exp(jnp.sum(lp[2:3] * lp[3:4], axis=-1, keepdims=True)) + lambda_init)
    for a in range(nsub):
        o = acc_ref[a, 0] / l_ref[a, 0] - lam * (acc_ref[a, 1] / l_ref[a, 1])
        o = o * lax.rsqrt(jnp.mean(o * o, axis=0, keepdims=True) + EPS) * (1.0 - lambda_init)
        o_ref[a * tk:(a + 1) * tk, :] = (o.T * g_ref[...]).astype(o_ref.dtype)


def _diff_attention(dq, dk, dvt, lam_params, subln_g, lambda_init, batch, seq):
    n, wd = dq.shape
    lam_params, lam_spec = _picked(*lam_params)
    subln_g, g_spec = _picked(*subln_g)
    tk = dvt.shape[2]
    tq = min(ATTN_QSUB * tk, seq)
    nq = seq // tq
    nsub = tq // tk
    qspec = pl.BlockSpec((tq, LANES), lambda b, h, i: (b * nq + i, h))
    kspec = pl.BlockSpec((seq, LANES), lambda b, h, i: (b, h))
    vtspec = pl.BlockSpec((seq // tk, DIFF_DV + ONES_ROWS, tk), lambda b, h, i: (b, h, 0))
    return pl.pallas_call(
        functools.partial(_diff_kernel, lambda_init=lambda_init),
        grid=(batch, DIFF_HEADS, nq),
        in_specs=[lam_spec, g_spec, qspec, kspec, vtspec],
        out_specs=qspec,
        out_shape=jax.ShapeDtypeStruct((n, wd), BF16),
        scratch_shapes=[pltpu.VMEM((nsub, 2, 1, tk), F32), pltpu.VMEM((nsub, 2, 1, tk), F32),
                        pltpu.VMEM((nsub, 2, DIFF_DV, tk), F32)],
        compiler_params=_cparams(("parallel", "parallel", "arbitrary")),
        name="diff_attn",
    )(lam_params, subln_g, dq, dk, dvt)


def _gla_kernel(q_ref, k_ref, la_ref, v_ref, gg_ref, g_ref, o_ref, state_ref):
    c = GLA_CHUNK
    sub = GLA_SUB
    nsub = c // sub
    heads = state_ref.shape[0]
    pair_lane = lax.broadcasted_iota(jnp.int32, (c, LANES), 1)
    pair_lane3 = lax.broadcasted_iota(jnp.int32, (nsub, sub, LANES), 2)
    own = [pair_lane < GLA_DK, pair_lane >= GLA_DK]
    own3 = [pair_lane3 < GLA_DK, pair_lane3 >= GLA_DK]

    @pl.when(pl.program_id(1) == 0)
    def _():
        state_ref[...] = jnp.zeros(state_ref.shape, F32)

    r_i = lax.broadcasted_iota(jnp.int32, (c, c), 0)
    c_i = lax.broadcasted_iota(jnp.int32, (c, c), 1)
    tril = (c_i <= r_i).astype(BF16)
    levels = []
    s = c // 2
    while s >= sub:
        levels.append((s, (r_i // (2 * s) == c_i // (2 * s)) & (r_i % (2 * s) >= s) & (c_i % (2 * s) < s)))
        s //= 2
    diag_mask = (r_i // sub == c_i // sub) & (c_i <= r_i)
    lane3 = lax.broadcasted_iota(jnp.int32, (nsub, sub, c), 2)
    blk3 = lax.broadcasted_iota(jnp.int32, (nsub, sub, c), 0)
    g = g_ref[...]

    def step(si, carry):
        cols = [slice(h * LANES, (h + 1) * LANES) for h in range(heads)]
        rows = [pl.ds(pl.multiple_of((si * GLA_UNROLL + u) * c, c), c) for u in range(GLA_UNROLL)]

        pairs = [slice(p * LANES, (p + 1) * LANES) for p in range(heads // 2)]

        cums = {}
        for u in range(GLA_UNROLL):
            for p in range(heads // 2):
                la_hi, la_mid, la_lo = _split3(la_ref[rows[u], pairs[p]])
                cums[u, p] = _dot(tril, la_hi) + _dot(tril, la_mid) + _dot(tril, la_lo)

        states = [state_ref[h] for h in range(heads)]
        mids = {}
        for u in range(GLA_UNROLL):
            for p in range(heads // 2):
                q = q_ref[rows[u], pairs[p]]
                k = k_ref[rows[u], pairs[p]]
                cum = cums[u, p]
                last = cum[c - 1:c, :]
                q_dec = (q * jnp.exp(cum)).astype(BF16)
                k_dec = k * jnp.exp(last - cum)
                state_decay = jnp.exp(last)

                level_ops = []
                for s, _ in levels:
                    ref = jnp.concatenate([jnp.broadcast_to(cum[lo + s:lo + s + 1, :], (2 * s, LANES))
                                           for lo in range(0, c, 2 * s)], axis=0)
                    qs = q * jnp.exp(jnp.minimum(cum - ref, 0.0))
                    ks = k * jnp.exp(jnp.minimum(ref - cum, 0.0))
                    level_ops.append((qs, ks.astype(BF16)))

                q3 = q.reshape(nsub, sub, LANES)
                k3 = k.reshape(nsub, sub, LANES)
                c3 = cum.reshape(nsub, sub, LANES)
                a3 = [jnp.zeros((nsub, sub, c), F32) for _ in range(2)]
                for j in range(sub):
                    dec = jnp.exp(jnp.minimum(c3 - c3[:, j:j + 1, :], 0.0))
                    prod = q3 * k3[:, j:j + 1, :] * dec
                    for t in range(2):
                        col = jnp.sum(jnp.where(own3[t], prod, 0.0), axis=-1, keepdims=True)
                        a3[t] = jnp.where(lane3 == blk3 * sub + j, col, a3[t])

                for t in range(2):
                    h = 2 * p + t
                    v = v_ref[rows[u], cols[h]]
                    state = states[h]
                    o_inter = _dot_nt(q_dec, state.astype(BF16))
                    states[h] = state * state_decay + _dot_tn(v, jnp.where(own[t], k_dec, 0.0).astype(BF16))
                    attn = jnp.where(diag_mask, a3[t].reshape(c, c), 0.0)
                    for (_, mask), (qs, ks) in zip(levels, level_ops):
                        a_level = _dot_nt(jnp.where(own[t], qs, 0.0).astype(BF16), ks)
                        attn = jnp.where(mask, a_level, attn)
                    mids[u, h] = (o_inter, _dot(attn.astype(BF16), v))

        for h in range(heads):
            state_ref[h] = states[h]
        for u in range(GLA_UNROLL):
            for h in range(heads):
                o_inter, o_intra = mids[u, h]
                gate = gg_ref[rows[u], cols[h]]
                o_ref[rows[u], cols[h]] = (_rms(o_inter + o_intra, g)
                                           * (gate * jax.nn.sigmoid(gate))).astype(o_ref.dtype)
        return carry

    lax.fori_loop(0, q_ref.shape[0] // (c * GLA_UNROLL), step, 0)


def _gla(gq, gk, la, gv, gg, norm_g, batch, seq):
    n, wd = gv.shape
    norm_g, g_spec = _picked(*norm_g)
    t = min(GLA_TILE, seq)
    nt = seq // t
    spec = pl.BlockSpec((t, wd), lambda b, i: (b * nt + i, 0))
    kspec = pl.BlockSpec((t, gq.shape[1]), lambda b, i: (b * nt + i, 0))
    return pl.pallas_call(
        _gla_kernel,
        grid=(batch, nt),
        in_specs=[kspec, kspec, kspec, spec, spec, g_spec],
        out_specs=spec,
        out_shape=jax.ShapeDtypeStruct((n, wd), BF16),
        scratch_shapes=[pltpu.VMEM((GLA_HEADS, GLA_DV, LANES), F32)],
        compiler_params=_cparams(("parallel", "arbitrary")),
        name="gla",
    )(gq, gk, la, gv, gg, norm_g)


SUM_ROWS = 8


def _sb_kernel(q_ref, k_ref, vt_ref, o_ref, acc_ref):
    i = pl.program_id(2)
    tk = vt_ref.shape[2]
    nsub = q_ref.shape[0] // tk
    base = i * nsub
    r_s = lax.broadcasted_iota(jnp.int32, (tk + SUM_ROWS, tk), 0)
    c_s = lax.broadcasted_iota(jnp.int32, (tk + SUM_ROWS, tk), 1)
    suffix_mat = ((c_s > r_s) | (r_s >= tk)).astype(BF16)
    strict = (lax.broadcasted_iota(jnp.int32, (tk, tk), 0) < lax.broadcasted_iota(jnp.int32, (tk, tk), 1))

    def run_tiles(tiles, later):
        later = list(later)
        logits = [_dot_nt(kb, q_ref[a * tk:(a + 1) * tk, :]) for a, kb, _, _ in tiles]
        mids = []
        for (a, _, _, masked), z in zip(tiles, logits):
            log_term = jnp.log2(1.0 + jnp.exp2(_neg_abs(z)))
            log_beta = jnp.minimum(z, 0.0) - log_term
            log_keep = log_beta - z
            if masked:
                log_keep = jnp.where(strict, log_keep, 0.0)
            mids.append((log_beta, _dot(suffix_mat, log_keep.astype(BF16))))
        pending = []
        for (a, _, vtb, masked), (log_beta, sums) in zip(tiles, mids):
            arg = log_beta + sums[:tk]
            if later[a] is not None:
                arg = arg + later[a]
            att = jnp.exp2(arg)
            if masked:
                att = jnp.where(strict, att, 0.0)
            pending.append((a, later[a] is None, _dot(vtb, att.astype(BF16))))
            later[a] = sums[tk:tk + 1] if later[a] is None else later[a] + sums[tk:tk + 1]
        for a, first, contrib in pending:
            if first:
                acc_ref[a] = contrib
            else:
                acc_ref[a] += contrib
        return tuple(later)

    def key_block(j):
        return k_ref[pl.ds(pl.multiple_of(j * tk, tk), tk), :], vt_ref[j]

    def retire(later_a, still_valid):
        return jnp.where(still_valid, later_a, -jnp.inf)

    def lockstep_tiles(offset, later):
        tiles, kept = [], []
        for a in range(nsub):
            j = base + a - offset
            kb, vtb = key_block(jnp.maximum(j, 0))
            tiles.append((a, kb, vtb, False))
            kept.append(retire(later[a], j >= 0))
        return tiles, kept

    diag = []
    for a in range(nsub):
        kb, vtb = key_block(base + a)
        diag.append((a, kb, vtb, True))
    later = run_tiles(diag, [None] * nsub)
    later = run_tiles(*lockstep_tiles(1, later))

    def alive(later):
        return (jnp.max(functools.reduce(jnp.maximum, later)) > LOG2_FLUSH).astype(jnp.int32)

    def cond(carry):
        step, live, _ = carry
        return (step < base + nsub - 2) & (live > 0)

    def body(carry):
        step, _, later = carry
        later = run_tiles(*lockstep_tiles(step + 2, later))
        return step + 1, alive(later), later

    lax.while_loop(cond, body, (jnp.int32(0), alive(later), later))
    for a in range(nsub):
        o_ref[a * tk:(a + 1) * tk, :] = acc_ref[a].T.astype(o_ref.dtype)


def _stick_breaking(q, k, vt, batch, seq):
    n, d = q.shape
    tk = vt.shape[2]
    tq = min(SB_QSUB * tk, seq)
    nq = seq // tq
    qspec = pl.BlockSpec((tq, LANES), lambda b, h, i: (b * nq + i, h))
    kspec = pl.BlockSpec((seq, LANES), lambda b, h, i: (b, h))
    vtspec = pl.BlockSpec((seq // tk, LANES, tk), lambda b, h, i: (b, h, 0))
    return pl.pallas_call(
        _sb_kernel,
        grid=(batch, SB_HEADS, nq),
        in_specs=[qspec, kspec, vtspec],
        out_specs=qspec,
        out_shape=jax.ShapeDtypeStruct((n, d), BF16),
        scratch_shapes=[pltpu.VMEM((tq // tk, LANES, tk), F32)],
        compiler_params=_cparams(("parallel", "parallel", "arbitrary")),
        name="stick_breaking",
    )(q, k, vt)


def kernel(x, ffn_pre_g, ffn_post_g, ffn_w_gate, ffn_w_up, ffn_w_down, mix_pre_g, mix_post_g, hyb_w_in, hyb_w_out, diff_lambda, diff_subln_g, gla_w_a2, gla_b_a, gla_norm_g, sb_w_qkv, sb_w_out):
    batch, seq, d = x.shape
    depth = ffn_pre_g.shape[0]
    assert d // SB_HEADS == LANES and DIFF_DV == LANES and GLA_DV == LANES
    h = x.reshape(batch * seq, d)

    n_ffn = ffn_w_gate.shape[1]
    dff = ffn_w_gate.shape[-1]
    wgu_all = jnp.concatenate([ffn_w_gate, ffn_w_up], axis=-1).astype(BF16).reshape(depth * n_ffn, d, 2 * dff)
    wd_all = ffn_w_down.astype(BF16).reshape(depth * n_ffn, dff, d)

    ffn_pre = ffn_pre_g.reshape(depth * n_ffn, 1, d)
    ffn_post = ffn_post_g.reshape(depth * n_ffn, 1, d)
    mix_pre = mix_pre_g.reshape(depth, 1, d)
    mix_post = mix_post_g.reshape(depth, 1, d)
    subln = diff_subln_g.reshape(-1, 1, DIFF_DV)
    gla_g = gla_norm_g.reshape(-1, 1, GLA_DV)
    hyb_out = hyb_w_out.astype(BF16)
    sb_out = sb_w_out.astype(BF16)

    def ffn(h, layer, j, mix=None):
        idx = layer * n_ffn + j
        return _ffn(h, (ffn_pre, idx), wgu_all, wd_all, idx, (ffn_post, idx), mix)

    for layer in range(depth):
        h = ffn(h, layer, 0)
        if layer % 2 == 0:
            e = layer // 2
            lambda_init = 0.8 - 0.6 * math.exp(-0.3 * layer)
            dq, dk, dvt, gq, gk, la, gv, gg = _proj_even(h, (mix_pre, layer), hyb_w_in[e], gla_w_a2[e], gla_b_a[e])
            a_out = _diff_attention(dq, dk, dvt, (diff_lambda, e), (subln, e), lambda_init, batch, seq)
            b_out = _gla(gq, gk, la, gv, gg, (gla_g, e), batch, seq)
            mix = (a_out, 0, b_out, 0, (hyb_out, e), (mix_post, layer))
        else:
            o = layer // 2
            q, k, vt = _proj_odd(h, (mix_pre, layer), sb_w_qkv[o])
            att = _stick_breaking(q, k, vt, batch, seq)
            mix = (att, 0, att, 1, (sb_out, o), (mix_post, layer))
        h = ffn(h, layer, 1, mix)
    return h.reshape(batch, seq, d)
```

```python
import functools
import math

import jax
import jax.numpy as jnp
from jax import lax
from jax.experimental import pallas as pl
from jax.experimental.pallas import tpu as pltpu

F32 = jnp.float32
BF16 = jnp.bfloat16

EPS = 1e-6
DIFF_HEADS = 4
DIFF_DH = 64
DIFF_DV = 2 * DIFF_DH
GLA_HEADS = 4
GLA_DK = 64
GLA_DV = 128
GLA_RANK = 16
GLA_TAU = 16.0
GLA_CHUNK = 64
GLA_UNROLL = 8
GLA_SUB = 8
SB_HEADS = 8

LANES = 128
VMEM_LIMIT = 56 * 1024 * 1024
DIFF_VMEM_LIMIT = 62 * 1024 * 1024

TOKEN_TILE = 1024
FFN_TILE = 1024
ROW_PART = 256
GLA_TILE = 1024
ATTN_TILE = 256
ATTN_QSUB = 8
SB_QSUB = 16
DIFF_KV_UNROLL = 8
DIFF_RUN_BLOCKS = 8
DIFF_KV_GROUP = 2
ONES_ROWS = 16
LOG2_FLUSH = -150.0
LOG2E = math.log2(math.e)


def _cparams(sem, vmem_limit=VMEM_LIMIT):
    return pltpu.CompilerParams(dimension_semantics=sem, vmem_limit_bytes=vmem_limit)


def _picked(stacked, idx):
    return stacked, pl.BlockSpec((None,) + stacked.shape[1:], lambda *_: (idx,) + (0,) * (stacked.ndim - 1))


def _rms(x, g):
    return x * lax.rsqrt(jnp.mean(x * x, axis=-1, keepdims=True) + EPS) * g


def _dot(a, b):
    return jnp.dot(a, b, preferred_element_type=F32)


def _dot_nt(a, b):
    return lax.dot_general(a, b, (((1,), (1,)), ((), ())), preferred_element_type=F32)


def _dot_tn(a, b):
    return lax.dot_general(a, b, (((0,), (0,)), ((), ())), preferred_element_type=F32)


def _split3(x):
    hi = x.astype(BF16)
    r1 = x - hi.astype(F32)
    mid = r1.astype(BF16)
    lo = (r1 - mid.astype(F32)).astype(BF16)
    return hi, mid, lo


def _neg_abs(x):
    bits = lax.bitcast_convert_type(x, jnp.int32) | jnp.int32(-2 ** 31)
    return lax.bitcast_convert_type(bits, F32)


def _softplus_parts(z):
    return jnp.maximum(z, 0.0), jnp.log(1.0 + jnp.exp(-jnp.abs(z)))


def _ffn_body(hs, gpre_ref, wgu_ref, wd_ref, gpost_ref):
    dff = wd_ref.shape[0]
    xns = [_rms(h, gpre_ref[...]).astype(BF16) for h in hs]
    acts = []
    for xn in xns:
        gu = _dot(xn, wgu_ref[...])
        gate = gu[:, :dff]
        acts.append((gate * jax.nn.sigmoid(gate) * gu[:, dff:]).astype(BF16))
    fs = [_dot(act, wd_ref[...]) for act in acts]
    return [h + 0.5 * _rms(f, gpost_ref[...]) for h, f in zip(hs, fs)]


def _row_parts(tm):
    part = min(ROW_PART, tm)
    return [pl.ds(p * part, part) for p in range(tm // part)]


def _ffn_kernel(h_ref, gpre_ref, wgu_ref, wd_ref, gpost_ref, o_ref):
    parts = _row_parts(h_ref.shape[0])
    outs = _ffn_body([h_ref[r, :] for r in parts], gpre_ref, wgu_ref, wd_ref, gpost_ref)
    for r, o in zip(parts, outs):
        o_ref[r, :] = o


def _mix_ffn_kernel(h_ref, a_ref, b_ref, wo_ref, gmix_ref, gpre_ref, wgu_ref, wd_ref, gpost_ref, o_ref):
    parts = _row_parts(h_ref.shape[0])
    half = a_ref.shape[1]
    ys = [_dot(a_ref[r, :], wo_ref[0:half, :]) + _dot(b_ref[r, :], wo_ref[half:, :]) for r in parts]
    hs = [h_ref[r, :] + _rms(y, gmix_ref[...]) for r, y in zip(parts, ys)]
    outs = _ffn_body(hs, gpre_ref, wgu_ref, wd_ref, gpost_ref)
    for r, o in zip(parts, outs):
        o_ref[r, :] = o


def _ffn(h, g_pre, wgu_all, wd_all, idx, g_post, mix=None):
    n, d = h.shape
    dff = wd_all.shape[1]
    tm = min(FFN_TILE, n)
    row = lambda i: (i, 0)
    g_pre, g_pre_spec = _picked(*g_pre)
    g_post, g_post_spec = _picked(*g_post)
    wgu_all, wgu_spec = _picked(wgu_all, idx)
    wd_all, wd_spec = _picked(wd_all, idx)
    ffn_specs = [g_pre_spec, wgu_spec, wd_spec, g_post_spec]
    ffn_args = (g_pre, wgu_all, wd_all, g_post)
    if mix is None:
        body, specs, args, name = _ffn_kernel, [], (), "ffn"
    else:
        a, a_col, b, b_col, w_out, g_mix = mix
        w_out, w_out_spec = _picked(*w_out)
        g_mix, g_mix_spec = _picked(*g_mix)
        half = w_out.shape[1] // 2
        body, name = _mix_ffn_kernel, "mix_ffn"
        specs = [pl.BlockSpec((tm, half), lambda i: (i, a_col)), pl.BlockSpec((tm, half), lambda i: (i, b_col)),
                 w_out_spec, g_mix_spec]
        args = (a, b, w_out, g_mix)
    return pl.pallas_call(
        body,
        grid=(n // tm,),
        in_specs=[pl.BlockSpec((tm, d), row)] + specs + ffn_specs,
        out_specs=pl.BlockSpec((tm, d), row),
        out_shape=jax.ShapeDtypeStruct((n, d), F32),
        compiler_params=_cparams(("parallel",)),
        name=name,
    )(h, *args, *ffn_args)


def _store_key_blocks(vt_ref, vt, part):
    tk = vt_ref.shape[2]
    blocks = vt.shape[1] // tk
    for c in range(blocks):
        vt_ref[part * blocks + c] = vt[:, c * tk:(c + 1) * tk].astype(vt_ref.dtype)


def _proj_even_kernel(h_ref, g_ref, w_ref, wvt_ref, ones_ref, wga_ref, wa2_ref, ba_ref,
                      dq_ref, dk_ref, dvt_ref, gq_ref, gk_ref, la_ref, gv_ref, gg_ref):
    w_hi, w_lo, _ = _split3(wa2_ref[...])
    widths = [r.shape[1] for r in (dq_ref, dk_ref, gq_ref, gk_ref, gv_ref, gg_ref)]
    starts = [sum(widths[:i]) for i in range(len(widths))]
    parts = _row_parts(h_ref.shape[0])
    ms = [_rms(h_ref[r, :], g_ref[...]).astype(BF16) for r in parts]
    for p, (r, m) in enumerate(zip(parts, ms)):
        def seg(i):
            return _dot(m, w_ref[:, starts[i]:starts[i] + widths[i]])

        dq_ref[r, :] = (seg(0) * (DIFF_DH ** -0.5 * LOG2E)).astype(BF16)
        dk_ref[r, :] = seg(1).astype(BF16)
        _store_key_blocks(dvt_ref, _dot_nt(wvt_ref[...], m) + ones_ref[...], p)
        gq_ref[r, :] = seg(2) * GLA_DK ** -0.5
        gk_ref[r, :] = seg(3)
        gv_ref[r, :] = seg(4).astype(BF16)
        gg_ref[r, :] = seg(5)
        ga_hi, ga_lo, _ = _split3(_dot(m, wga_ref[...]))
        x = _dot(ga_hi, w_hi) + _dot(ga_lo, w_hi) + _dot(ga_hi, w_lo) + ba_ref[...]
        relu_neg, log_term = _softplus_parts(-x)
        la_ref[r, :] = -(relu_neg + log_term) * (1.0 / GLA_TAU)


def _proj_even(h, g, w_in, w_a2, b_a):
    n, d = h.shape
    g, g_spec = _picked(*g)
    tm = min(TOKEN_TILE, n)
    wd = DIFF_HEADS * DIFF_DV
    wk = GLA_HEADS * GLA_DK
    c = 0
    parts = []
    for width in (wd, wd, wd, wk, wk, GLA_HEADS * GLA_DV, GLA_HEADS * GLA_DV):
        parts.append(w_in[:, c:c + width])
        c += width
    w_vt = parts.pop(2).T.reshape(DIFF_HEADS, DIFF_DV, d)
    w_vt = jnp.pad(w_vt, ((0, 0), (0, ONES_ROWS), (0, 0))).reshape(-1, d).astype(BF16)
    ones_col = jnp.pad(jnp.zeros((DIFF_HEADS, DIFF_DV, 1), F32), ((0, 0), (0, ONES_ROWS), (0, 0)),
                       constant_values=1.0).reshape(-1, 1)
    vt_rows = w_vt.shape[0]
    w_main = jnp.concatenate(parts, axis=1).astype(BF16)
    tk = min(ATTN_TILE, tm)
    assert min(ROW_PART, tm) % tk == 0
    w_ga = jnp.pad(w_in[:, c:c + GLA_RANK], ((0, 0), (0, LANES - GLA_RANK))).astype(BF16)
    wa2 = jnp.pad(w_a2, ((0, LANES - GLA_RANK), (0, 0)))
    ba = b_a.reshape(1, wk)
    row = lambda i: (i, 0)
    fixed = lambda i: (0, 0)
    out_bf = jax.ShapeDtypeStruct((n, wd), BF16)
    out_f = jax.ShapeDtypeStruct((n, wd), F32)
    out_k = jax.ShapeDtypeStruct((n, wk), F32)
    out_vt = jax.ShapeDtypeStruct((n // tk, vt_rows, tk), BF16)
    rowspec = pl.BlockSpec((tm, wd), row)
    kspec = pl.BlockSpec((tm, wk), row)
    vtspec = pl.BlockSpec((tm // tk, vt_rows, tk), lambda i: (i, 0, 0))
    return pl.pallas_call(
        _proj_even_kernel,
        grid=(n // tm,),
        in_specs=[pl.BlockSpec((tm, d), row), g_spec,
                  pl.BlockSpec(w_main.shape, fixed), pl.BlockSpec(w_vt.shape, fixed),
                  pl.BlockSpec(ones_col.shape, fixed), pl.BlockSpec(w_ga.shape, fixed),
                  pl.BlockSpec(wa2.shape, fixed), pl.BlockSpec((1, wk), fixed)],
        out_specs=[rowspec, rowspec, vtspec, kspec, kspec, kspec, rowspec, rowspec],
        out_shape=[out_bf, out_bf, out_vt, out_k, out_k, out_k, out_bf, out_f],
        compiler_params=_cparams(("parallel",)),
        name="proj_even",
    )(h, g, w_main, w_vt, ones_col, w_ga, wa2, ba)


def _proj_odd_kernel(h_ref, g_ref, w_ref, wvt_ref, q_ref, k_ref, vt_ref):
    d = q_ref.shape[1]
    dh = d // SB_HEADS
    parts = _row_parts(h_ref.shape[0])
    ms = [_rms(h_ref[r, :], g_ref[...]).astype(BF16) for r in parts]
    for p, (r, m) in enumerate(zip(parts, ms)):
        q_ref[r, :] = (_dot(m, w_ref[:, 0:d]) * (dh ** -0.5 * LOG2E)).astype(BF16)
        k_ref[r, :] = _dot(m, w_ref[:, d:2 * d]).astype(BF16)
        _store_key_blocks(vt_ref, _dot_nt(wvt_ref[...], m), p)


def _proj_odd(h, g, w_qkv):
    n, d = h.shape
    g, g_spec = _picked(*g)
    tm = min(TOKEN_TILE, n)
    tk = min(ATTN_TILE, tm)
    row = lambda i: (i, 0)
    fixed = lambda i: (0, 0)
    out = jax.ShapeDtypeStruct((n, d), BF16)
    w_qk = w_qkv[:, :2 * d].astype(BF16)
    w_vt = w_qkv[:, 2 * d:].T.astype(BF16)
    return pl.pallas_call(
        _proj_odd_kernel,
        grid=(n // tm,),
        in_specs=[pl.BlockSpec((tm, d), row), g_spec,
                  pl.BlockSpec(w_qk.shape, fixed), pl.BlockSpec(w_vt.shape, fixed)],
        out_specs=[pl.BlockSpec((tm, d), row), pl.BlockSpec((tm, d), row),
                   pl.BlockSpec((tm // tk, d, tk), lambda i: (i, 0, 0))],
        out_shape=[out, out, jax.ShapeDtypeStruct((n // tk, d, tk), BF16)],
        compiler_params=_cparams(("parallel",)),
        name="proj_odd",
    )(h, g, w_qk, w_vt)


def _diff_kernel(lam_ref, g_ref, q_ref, k_ref, vt_ref, o_ref, m_ref, l_ref, acc_ref, *, lambda_init):
    i = pl.program_id(2)
    tk = vt_ref.shape[2]
    dv = DIFF_DV
    nsub = q_ref.shape[0] // tk
    base = i * nsub
    lane = lax.broadcasted_iota(jnp.int32, (tk, LANES), 1)
    causal = (lax.broadcasted_iota(jnp.int32, (tk, tk), 0) <= lax.broadcasted_iota(jnp.int32, (tk, tk), 1))

    def q_map(a, mi):
        q = q_ref[a * tk:(a + 1) * tk, :]
        keep = (lane < DIFF_DH) if mi == 0 else (lane >= DIFF_DH)
        return jnp.where(keep, q, jnp.zeros_like(q))

    def run_tiles(tiles, scores):
        pending = []
        for (a, mi, vtbs, first, masked), ss in zip(tiles, scores):
            if masked:
                ss = [jnp.where(causal, s, -jnp.inf) for s in ss]
            m_blk = functools.reduce(jnp.maximum, [jnp.max(s, axis=0, keepdims=True) for s in ss])
            if first:
                alpha = None
                m_new = m_blk
            else:
                m_prev = m_ref[a, mi]
                m_new = jnp.maximum(m_prev, m_blk)
                alpha = jnp.exp2(m_prev - m_new)
            m_ref[a, mi] = m_new
            p = jnp.concatenate([jnp.exp2(s - m_new).astype(BF16) for s in ss], axis=0)
            vtb = jnp.concatenate(vtbs, axis=1)
            pending.append((a, mi, alpha, _dot(vtb, p)))
        for a, mi, alpha, pv in pending:
            if alpha is None:
                acc_ref[a, mi] = pv[:dv]
                l_ref[a, mi] = pv[dv:dv + 1]
            else:
                acc_ref[a, mi] = alpha * acc_ref[a, mi] + pv[:dv]
                l_ref[a, mi] = alpha * l_ref[a, mi] + pv[dv:dv + 1]

    def keys(j):
        return k_ref[pl.ds(pl.multiple_of(j * tk, tk), tk), :]

    band, band_scores = [], []
    for c in range(nsub):
        kb, vtb = keys(base + c), vt_ref[base + c]
        for a in range(c, nsub):
            for mi in range(2):
                band.append((a, mi, [vtb], c == 0, a == c))
                band_scores.append([_dot_nt(kb, q_map(a, mi))])
    run_tiles(band, band_scores)

    kv_unroll = DIFF_KV_UNROLL if nsub % DIFF_KV_UNROLL == 0 else 1
    group = DIFF_KV_GROUP if kv_unroll % DIFF_KV_GROUP == 0 else 1

    def body(step, carry):
        for r in range(0, kv_unroll, DIFF_RUN_BLOCKS):
            tiles, scores = [], []
            for u in range(r, min(r + DIFF_RUN_BLOCKS, kv_unroll), group):
                blocks = [step * kv_unroll + u + w for w in range(group)]
                kbs = [keys(j) for j in blocks]
                vtbs = [vt_ref[j] for j in blocks]
                for a in range(nsub):
                    for mi in range(2):
                        tiles.append((a, mi, vtbs, False, False))
                        scores.append([_dot_nt(kb, q_map(a, mi)) for kb in kbs])
            run_tiles(tiles, scores)
        return carry

    lax.fori_loop(0, base // kv_unroll, body, 0)

    lp = lam_ref[...]
    lam = (jnp.exp(jnp.sum(lp[0:1] * lp[1:2], axis=-1, keepdims=True))
           - jnp.exp(jnp.sum(lp[2:3] * lp[3:4], axis=-1, keepdims=True)) + lambda_init)
    for a in range(nsub):
        o = acc_ref[a, 0] / l_ref[a, 0] - lam * (acc_ref[a, 1] / l_ref[a, 1])
        o = o * lax.rsqrt(jnp.mean(o * o, axis=0, keepdims=True) + EPS) * (1.0 - lambda_init)
        o_ref[a * tk:(a + 1) * tk, :] = (o.T * g_ref[...]).astype(o_ref.dtype)


def _diff_attention(dq, dk, dvt, lam_params, subln_g, lambda_init, batch, seq):
    n, wd = dq.shape
    lam_params, lam_spec = _picked(*lam_params)
    subln_g, g_spec = _picked(*subln_g)
    tk = dvt.shape[2]
    tq = min(ATTN_QSUB * tk, seq)
    nq = seq // tq
    nsub = tq // tk
    qspec = pl.BlockSpec((tq, LANES), lambda b, h, i: (b * nq + i, h))
    kspec = pl.BlockSpec((seq, LANES), lambda b, h, i: (b, h), pipeline_mode=pl.Buffered(1))
    vtspec = pl.BlockSpec((seq // tk, DIFF_DV + ONES_ROWS, tk), lambda b, h, i: (b, h, 0),
                          pipeline_mode=pl.Buffered(1))
    return pl.pallas_call(
        functools.partial(_diff_kernel, lambda_init=lambda_init),
        grid=(batch, DIFF_HEADS, nq),
        in_specs=[lam_spec, g_spec, qspec, kspec, vtspec],
        out_specs=qspec,
        out_shape=jax.ShapeDtypeStruct((n, wd), BF16),
        scratch_shapes=[pltpu.VMEM((nsub, 2, 1, tk), F32), pltpu.VMEM((nsub, 2, 1, tk), F32),
                        pltpu.VMEM((nsub, 2, DIFF_DV, tk), F32)],
        compiler_params=_cparams(("parallel", "parallel", "arbitrary"), DIFF_VMEM_LIMIT),
        name="diff_attn",
    )(lam_params, subln_g, dq, dk, dvt)


def _gla_kernel(q_ref, k_ref, la_ref, v_ref, gg_ref, g_ref, o_ref, state_ref):
    c = GLA_CHUNK
    sub = GLA_SUB
    nsub = c // sub
    heads = state_ref.shape[0]
    pair_lane = lax.broadcasted_iota(jnp.int32, (c, LANES), 1)
    pair_lane3 = lax.broadcasted_iota(jnp.int32, (nsub, sub, LANES), 2)
    own = [pair_lane < GLA_DK, pair_lane >= GLA_DK]
    own3 = [pair_lane3 < GLA_DK, pair_lane3 >= GLA_DK]

    @pl.when(pl.program_id(1) == 0)
    def _():
        state_ref[...] = jnp.zeros(state_ref.shape, F32)

    r_i = lax.broadcasted_iota(jnp.int32, (c, c), 0)
    c_i = lax.broadcasted_iota(jnp.int32, (c, c), 1)
    tril = (c_i <= r_i).astype(BF16)
    levels = []
    s = c // 2
    while s >= sub:
        levels.append((s, (r_i // (2 * s) == c_i // (2 * s)) & (r_i % (2 * s) >= s) & (c_i % (2 * s) < s)))
        s //= 2
    diag_mask = (r_i // sub == c_i // sub) & (c_i <= r_i)
    lane3 = lax.broadcasted_iota(jnp.int32, (nsub, sub, c), 2)
    blk3 = lax.broadcasted_iota(jnp.int32, (nsub, sub, c), 0)
    g = g_ref[...]

    def step(si, carry):
        cols = [slice(h * LANES, (h + 1) * LANES) for h in range(heads)]
        rows = [pl.ds(pl.multiple_of((si * GLA_UNROLL + u) * c, c), c) for u in range(GLA_UNROLL)]

        pairs = [slice(p * LANES, (p + 1) * LANES) for p in range(heads // 2)]

        cums = {}
        for u in range(GLA_UNROLL):
            for p in range(heads // 2):
                la_hi, la_mid, la_lo = _split3(la_ref[rows[u], pairs[p]])
                cums[u, p] = _dot(tril, la_hi) + _dot(tril, la_mid) + _dot(tril, la_lo)

        states = [state_ref[h] for h in range(heads)]
        mids = {}
        for u in range(GLA_UNROLL):
            for p in range(heads // 2):
                q = q_ref[rows[u], pairs[p]]
                k = k_ref[rows[u], pairs[p]]
                cum = cums[u, p]
                last = cum[c - 1:c, :]
                q_dec = (q * jnp.exp(cum)).astype(BF16)
                k_dec = k * jnp.exp(last - cum)
                state_decay = jnp.exp(last)

                level_ops = []
                for s, _ in levels:
                    ref = jnp.concatenate([jnp.broadcast_to(cum[lo + s:lo + s + 1, :], (2 * s, LANES))
                                           for lo in range(0, c, 2 * s)], axis=0)
                    qs = q * jnp.exp(jnp.minimum(cum - ref, 0.0))
                    ks = k * jnp.exp(jnp.minimum(ref - cum, 0.0))
                    level_ops.append((qs, ks.astype(BF16)))

                q3 = q.reshape(nsub, sub, LANES)
                k3 = k.reshape(nsub, sub, LANES)
                c3 = cum.reshape(nsub, sub, LANES)
                a3 = [jnp.zeros((nsub, sub, c), F32) for _ in range(2)]
                for j in range(sub):
                    dec = jnp.exp(jnp.minimum(c3 - c3[:, j:j + 1, :], 0.0))
                    prod = q3 * k3[:, j:j + 1, :] * dec
                    for t in range(2):
                        col = jnp.sum(jnp.where(own3[t], prod, 0.0), axis=-1, keepdims=True)
                        a3[t] = jnp.where(lane3 == blk3 * sub + j, col, a3[t])

                for t in range(2):
                    h = 2 * p + t
                    v = v_ref[rows[u], cols[h]]
                    state = states[h]
                    o_inter = _dot_nt(q_dec, state.astype(BF16))
                    states[h] = state * state_decay + _dot_tn(v, jnp.where(own[t], k_dec, 0.0).astype(BF16))
                    attn = jnp.where(diag_mask, a3[t].reshape(c, c), 0.0)
                    for (_, mask), (qs, ks) in zip(levels, level_ops):
                        a_level = _dot_nt(jnp.where(own[t], qs, 0.0).astype(BF16), ks)
                        attn = jnp.where(mask, a_level, attn)
                    mids[u, h] = (o_inter, _dot(attn.astype(BF16), v))

        for h in range(heads):
            state_ref[h] = states[h]
        for u in range(GLA_UNROLL):
            for h in range(heads):
                o_inter, o_intra = mids[u, h]
                gate = gg_ref[rows[u], cols[h]]
                o_ref[rows[u], cols[h]] = (_rms(o_inter + o_intra, g)
                                           * (gate * jax.nn.sigmoid(gate))).astype(o_ref.dtype)
        return carry

    lax.fori_loop(0, q_ref.shape[0] // (c * GLA_UNROLL), step, 0)


def _gla(gq, gk, la, gv, gg, norm_g, batch, seq):
    n, wd = gv.shape
    norm_g, g_spec = _picked(*norm_g)
    t = min(GLA_TILE, seq)
    nt = seq // t
    spec = pl.BlockSpec((t, wd), lambda b, i: (b * nt + i, 0))
    kspec = pl.BlockSpec((t, gq.shape[1]), lambda b, i: (b * nt + i, 0))
    return pl.pallas_call(
        _gla_kernel,
        grid=(batch, nt),
        in_specs=[kspec, kspec, kspec, spec, spec, g_spec],
        out_specs=spec,
        out_shape=jax.ShapeDtypeStruct((n, wd), BF16),
        scratch_shapes=[pltpu.VMEM((GLA_HEADS, GLA_DV, LANES), F32)],
        compiler_params=_cparams(("parallel", "arbitrary")),
        name="gla",
    )(gq, gk, la, gv, gg, norm_g)


SUM_ROWS = 8


def _sb_kernel(q_ref, k_ref, vt_ref, o_ref, acc_ref):
    i = pl.program_id(2)
    tk = vt_ref.shape[2]
    nsub = q_ref.shape[0] // tk
    base = i * nsub
    r_s = lax.broadcasted_iota(jnp.int32, (tk + SUM_ROWS, tk), 0)
    c_s = lax.broadcasted_iota(jnp.int32, (tk + SUM_ROWS, tk), 1)
    suffix_mat = ((c_s > r_s) | (r_s >= tk)).astype(BF16)
    strict = (lax.broadcasted_iota(jnp.int32, (tk, tk), 0) < lax.broadcasted_iota(jnp.int32, (tk, tk), 1))

    def run_tiles(tiles, later):
        later = list(later)
        logits = [_dot_nt(kb, q_ref[a * tk:(a + 1) * tk, :]) for a, kb, _, _ in tiles]
        mids = []
        for (a, _, _, masked), z in zip(tiles, logits):
            log_term = jnp.log2(1.0 + jnp.exp2(_neg_abs(z)))
            log_beta = jnp.minimum(z, 0.0) - log_term
            log_keep = log_beta - z
            if masked:
                log_keep = jnp.where(strict, log_keep, 0.0)
            mids.append((log_beta, _dot(suffix_mat, log_keep.astype(BF16))))
        pending = []
        for (a, _, vtb, masked), (log_beta, sums) in zip(tiles, mids):
            arg = log_beta + sums[:tk]
            if later[a] is not None:
                arg = arg + later[a]
            att = jnp.exp2(arg)
            if masked:
                att = jnp.where(strict, att, 0.0)
            pending.append((a, later[a] is None, _dot(vtb, att.astype(BF16))))
            later[a] = sums[tk:tk + 1] if later[a] is None else later[a] + sums[tk:tk + 1]
        for a, first, contrib in pending:
            if first:
                acc_ref[a] = contrib
            else:
                acc_ref[a] += contrib
        return tuple(later)

    def key_block(j):
        return k_ref[pl.ds(pl.multiple_of(j * tk, tk), tk), :], vt_ref[j]

    def retire(later_a, still_valid):
        return jnp.where(still_valid, later_a, -jnp.inf)

    def lockstep_tiles(offset, later):
        tiles, kept = [], []
        for a in range(nsub):
            j = base + a - offset
            kb, vtb = key_block(jnp.maximum(j, 0))
            tiles.append((a, kb, vtb, False))
            kept.append(retire(later[a], j >= 0))
        return tiles, kept

    diag = []
    for a in range(nsub):
        kb, vtb = key_block(base + a)
        diag.append((a, kb, vtb, True))
    later = run_tiles(diag, [None] * nsub)
    later = run_tiles(*lockstep_tiles(1, later))

    def alive(later):
        return (jnp.max(functools.reduce(jnp.maximum, later)) > LOG2_FLUSH).astype(jnp.int32)

    def cond(carry):
        step, live, _ = carry
        return (step < base + nsub - 2) & (live > 0)

    def body(carry):
        step, _, later = carry
        later = run_tiles(*lockstep_tiles(step + 2, later))
        return step + 1, alive(later), later

    lax.while_loop(cond, body, (jnp.int32(0), alive(later), later))
    for a in range(nsub):
        o_ref[a * tk:(a + 1) * tk, :] = acc_ref[a].T.astype(o_ref.dtype)


def _stick_breaking(q, k, vt, batch, seq):
    n, d = q.shape
    tk = vt.shape[2]
    tq = min(SB_QSUB * tk, seq)
    nq = seq // tq
    qspec = pl.BlockSpec((tq, LANES), lambda b, h, i: (b * nq + i, h))
    kspec = pl.BlockSpec((seq, LANES), lambda b, h, i: (b, h))
    vtspec = pl.BlockSpec((seq // tk, LANES, tk), lambda b, h, i: (b, h, 0))
    return pl.pallas_call(
        _sb_kernel,
        grid=(batch, SB_HEADS, nq),
        in_specs=[qspec, kspec, vtspec],
        out_specs=qspec,
        out_shape=jax.ShapeDtypeStruct((n, d), BF16),
        scratch_shapes=[pltpu.VMEM((tq // tk, LANES, tk), F32)],
        compiler_params=_cparams(("parallel", "parallel", "arbitrary")),
        name="stick_breaking",
    )(q, k, vt)


def kernel(x, ffn_pre_g, ffn_post_g, ffn_w_gate, ffn_w_up, ffn_w_down, mix_pre_g, mix_post_g, hyb_w_in, hyb_w_out, diff_lambda, diff_subln_g, gla_w_a2, gla_b_a, gla_norm_g, sb_w_qkv, sb_w_out):
    batch, seq, d = x.shape
    depth = ffn_pre_g.shape[0]
    assert d // SB_HEADS == LANES and DIFF_DV == LANES and GLA_DV == LANES
    h = x.reshape(batch * seq, d)

    n_ffn = ffn_w_gate.shape[1]
    dff = ffn_w_gate.shape[-1]
    wgu_all = jnp.concatenate([ffn_w_gate, ffn_w_up], axis=-1).astype(BF16).reshape(depth * n_ffn, d, 2 * dff)
    wd_all = ffn_w_down.astype(BF16).reshape(depth * n_ffn, dff, d)

    ffn_pre = ffn_pre_g.reshape(depth * n_ffn, 1, d)
    ffn_post = ffn_post_g.reshape(depth * n_ffn, 1, d)
    mix_pre = mix_pre_g.reshape(depth, 1, d)
    mix_post = mix_post_g.reshape(depth, 1, d)
    subln = diff_subln_g.reshape(-1, 1, DIFF_DV)
    gla_g = gla_norm_g.reshape(-1, 1, GLA_DV)
    hyb_out = hyb_w_out.astype(BF16)
    sb_out = sb_w_out.astype(BF16)

    def ffn(h, layer, j, mix=None):
        idx = layer * n_ffn + j
        return _ffn(h, (ffn_pre, idx), wgu_all, wd_all, idx, (ffn_post, idx), mix)

    for layer in range(depth):
        h = ffn(h, layer, 0)
        if layer % 2 == 0:
            e = layer // 2
            lambda_init = 0.8 - 0.6 * math.exp(-0.3 * layer)
            dq, dk, dvt, gq, gk, la, gv, gg = _proj_even(h, (mix_pre, layer), hyb_w_in[e], gla_w_a2[e], gla_b_a[e])
            a_out = _diff_attention(dq, dk, dvt, (diff_lambda, e), (subln, e), lambda_init, batch, seq)
            b_out = _gla(gq, gk, la, gv, gg, (gla_g, e), batch, seq)
            mix = (a_out, 0, b_out, 0, (hyb_out, e), (mix_post, layer))
        else:
            o = layer // 2
            q, k, vt = _proj_odd(h, (mix_pre, layer), sb_w_qkv[o])
            att = _stick_breaking(q, k, vt, batch, seq)
            mix = (att, 0, att, 1, (sb_out, o), (mix_post, layer))
        h = ffn(h, layer, 1, mix)
    return h.reshape(batch, seq, d)
```

```python
import functools
import math

import jax
import jax.numpy as jnp
from jax import lax
from jax.experimental import pallas as pl
from jax.experimental.pallas import tpu as pltpu

F32 = jnp.float32
BF16 = jnp.bfloat16

EPS = 1e-6
DIFF_HEADS = 4
DIFF_DH = 64
DIFF_DV = 2 * DIFF_DH
GLA_HEADS = 4
GLA_DK = 64
GLA_DV = 128
GLA_RANK = 16
GLA_TAU = 16.0
GLA_CHUNK = 64
GLA_UNROLL = 8
GLA_SUB = 8
SB_HEADS = 8

LANES = 128
VMEM_LIMIT = 56 * 1024 * 1024
DIFF_VMEM_LIMIT = 62 * 1024 * 1024

TOKEN_TILE = 1024
FFN_TILE = 1024
ROW_PART = 256
GLA_TILE = 512
ATTN_TILE = 256
ATTN_QSUB = 8
SB_QSUB = 16
DIFF_KV_UNROLL = 8
DIFF_RUN_BLOCKS = 8
DIFF_KV_GROUP = 2
ONES_ROWS = 16
LOG2_FLUSH = -150.0
LOG2E = math.log2(math.e)


def _cparams(sem, vmem_limit=VMEM_LIMIT):
    return pltpu.CompilerParams(dimension_semantics=sem, vmem_limit_bytes=vmem_limit)


def _picked(stacked, idx):
    return stacked, pl.BlockSpec((None,) + stacked.shape[1:], lambda *_: (idx,) + (0,) * (stacked.ndim - 1))


def _rms(x, g):
    return x * lax.rsqrt(jnp.mean(x * x, axis=-1, keepdims=True) + EPS) * g


def _dot(a, b):
    return jnp.dot(a, b, preferred_element_type=F32)


def _dot_nt(a, b):
    return lax.dot_general(a, b, (((1,), (1,)), ((), ())), preferred_element_type=F32)


def _dot_tn(a, b):
    return lax.dot_general(a, b, (((0,), (0,)), ((), ())), preferred_element_type=F32)


def _split3(x):
    hi = x.astype(BF16)
    r1 = x - hi.astype(F32)
    mid = r1.astype(BF16)
    lo = (r1 - mid.astype(F32)).astype(BF16)
    return hi, mid, lo


def _neg_abs(x):
    bits = lax.bitcast_convert_type(x, jnp.int32) | jnp.int32(-2 ** 31)
    return lax.bitcast_convert_type(bits, F32)


def _softplus_parts(z):
    return jnp.maximum(z, 0.0), jnp.log(1.0 + jnp.exp(-jnp.abs(z)))


def _ffn_body(hs, gpre_ref, wgu_ref, wd_ref, gpost_ref):
    dff = wd_ref.shape[0]
    xns = [_rms(h, gpre_ref[...]).astype(BF16) for h in hs]
    acts = []
    for xn in xns:
        gu = _dot(xn, wgu_ref[...])
        gate = gu[:, :dff]
        acts.append((gate * jax.nn.sigmoid(gate) * gu[:, dff:]).astype(BF16))
    fs = [_dot(act, wd_ref[...]) for act in acts]
    return [h + 0.5 * _rms(f, gpost_ref[...]) for h, f in zip(hs, fs)]


def _row_parts(tm):
    part = min(ROW_PART, tm)
    return [pl.ds(p * part, part) for p in range(tm // part)]


def _ffn_kernel(h_ref, gpre_ref, wgu_ref, wd_ref, gpost_ref, o_ref):
    parts = _row_parts(h_ref.shape[0])
    outs = _ffn_body([h_ref[r, :] for r in parts], gpre_ref, wgu_ref, wd_ref, gpost_ref)
    for r, o in zip(parts, outs):
        o_ref[r, :] = o


def _mix_ffn_kernel(h_ref, a_ref, b_ref, wo_ref, gmix_ref, gpre_ref, wgu_ref, wd_ref, gpost_ref, o_ref):
    parts = _row_parts(h_ref.shape[0])
    half = a_ref.shape[1]
    ys = [_dot(a_ref[r, :], wo_ref[0:half, :]) + _dot(b_ref[r, :], wo_ref[half:, :]) for r in parts]
    hs = [h_ref[r, :] + _rms(y, gmix_ref[...]) for r, y in zip(parts, ys)]
    outs = _ffn_body(hs, gpre_ref, wgu_ref, wd_ref, gpost_ref)
    for r, o in zip(parts, outs):
        o_ref[r, :] = o


def _ffn(h, g_pre, wgu_all, wd_all, idx, g_post, mix=None):
    n, d = h.shape
    dff = wd_all.shape[1]
    tm = min(FFN_TILE, n)
    row = lambda i: (i, 0)
    g_pre, g_pre_spec = _picked(*g_pre)
    g_post, g_post_spec = _picked(*g_post)
    wgu_all, wgu_spec = _picked(wgu_all, idx)
    wd_all, wd_spec = _picked(wd_all, idx)
    ffn_specs = [g_pre_spec, wgu_spec, wd_spec, g_post_spec]
    ffn_args = (g_pre, wgu_all, wd_all, g_post)
    if mix is None:
        body, specs, args, name = _ffn_kernel, [], (), "ffn"
    else:
        a, a_col, b, b_col, w_out, g_mix = mix
        w_out, w_out_spec = _picked(*w_out)
        g_mix, g_mix_spec = _picked(*g_mix)
        half = w_out.shape[1] // 2
        body, name = _mix_ffn_kernel, "mix_ffn"
        specs = [pl.BlockSpec((tm, half), lambda i: (i, a_col)), pl.BlockSpec((tm, half), lambda i: (i, b_col)),
                 w_out_spec, g_mix_spec]
        args = (a, b, w_out, g_mix)
    return pl.pallas_call(
        body,
        grid=(n // tm,),
        in_specs=[pl.BlockSpec((tm, d), row)] + specs + ffn_specs,
        out_specs=pl.BlockSpec((tm, d), row),
        out_shape=jax.ShapeDtypeStruct((n, d), F32),
        compiler_params=_cparams(("parallel",)),
        name=name,
    )(h, *args, *ffn_args)


def _store_key_blocks(vt_ref, vt, part):
    tk = vt_ref.shape[2]
    blocks = vt.shape[1] // tk
    for c in range(blocks):
        vt_ref[part * blocks + c] = vt[:, c * tk:(c + 1) * tk].astype(vt_ref.dtype)


def _proj_even_kernel(h_ref, g_ref, w_ref, wvt_ref, ones_ref, wga_ref, wa2_ref, ba_ref,
                      dq_ref, dk_ref, dvt_ref, gq_ref, gk_ref, la_ref, gv_ref, gg_ref):
    w_hi, w_lo, _ = _split3(wa2_ref[...])
    widths = [r.shape[1] for r in (dq_ref, dk_ref, gq_ref, gk_ref, gv_ref, gg_ref)]
    starts = [sum(widths[:i]) for i in range(len(widths))]
    parts = _row_parts(h_ref.shape[0])
    ms = [_rms(h_ref[r, :], g_ref[...]).astype(BF16) for r in parts]
    for p, (r, m) in enumerate(zip(parts, ms)):
        def seg(i):
            return _dot(m, w_ref[:, starts[i]:starts[i] + widths[i]])

        dq_ref[r, :] = (seg(0) * (DIFF_DH ** -0.5 * LOG2E)).astype(BF16)
        dk_ref[r, :] = seg(1).astype(BF16)
        _store_key_blocks(dvt_ref, _dot_nt(wvt_ref[...], m) + ones_ref[...], p)
        gq_ref[r, :] = seg(2) * GLA_DK ** -0.5
        gk_ref[r, :] = seg(3)
        gv_ref[r, :] = seg(4).astype(BF16)
        gg_ref[r, :] = seg(5)
        ga_hi, ga_lo, _ = _split3(_dot(m, wga_ref[...]))
        x = _dot(ga_hi, w_hi) + _dot(ga_lo, w_hi) + _dot(ga_hi, w_lo) + ba_ref[...]
        relu_neg, log_term = _softplus_parts(-x)
        la_ref[r, :] = -(relu_neg + log_term) * (1.0 / GLA_TAU)


def _proj_even(h, g, w_in, w_a2, b_a):
    n, d = h.shape
    g, g_spec = _picked(*g)
    tm = min(TOKEN_TILE, n)
    wd = DIFF_HEADS * DIFF_DV
    wk = GLA_HEADS * GLA_DK
    c = 0
    parts = []
    for width in (wd, wd, wd, wk, wk, GLA_HEADS * GLA_DV, GLA_HEADS * GLA_DV):
        parts.append(w_in[:, c:c + width])
        c += width
    w_vt = parts.pop(2).T.reshape(DIFF_HEADS, DIFF_DV, d)
    w_vt = jnp.pad(w_vt, ((0, 0), (0, ONES_ROWS), (0, 0))).reshape(-1, d).astype(BF16)
    ones_col = jnp.pad(jnp.zeros((DIFF_HEADS, DIFF_DV, 1), F32), ((0, 0), (0, ONES_ROWS), (0, 0)),
                       constant_values=1.0).reshape(-1, 1)
    vt_rows = w_vt.shape[0]
    w_main = jnp.concatenate(parts, axis=1).astype(BF16)
    tk = min(ATTN_TILE, tm)
    assert min(ROW_PART, tm) % tk == 0
    w_ga = jnp.pad(w_in[:, c:c + GLA_RANK], ((0, 0), (0, LANES - GLA_RANK))).astype(BF16)
    wa2 = jnp.pad(w_a2, ((0, LANES - GLA_RANK), (0, 0)))
    ba = b_a.reshape(1, wk)
    row = lambda i: (i, 0)
    fixed = lambda i: (0, 0)
    out_bf = jax.ShapeDtypeStruct((n, wd), BF16)
    out_f = jax.ShapeDtypeStruct((n, wd), F32)
    out_k = jax.ShapeDtypeStruct((n, wk), F32)
    out_vt = jax.ShapeDtypeStruct((n // tk, vt_rows, tk), BF16)
    rowspec = pl.BlockSpec((tm, wd), row)
    kspec = pl.BlockSpec((tm, wk), row)
    vtspec = pl.BlockSpec((tm // tk, vt_rows, tk), lambda i: (i, 0, 0))
    return pl.pallas_call(
        _proj_even_kernel,
        grid=(n // tm,),
        in_specs=[pl.BlockSpec((tm, d), row), g_spec,
                  pl.BlockSpec(w_main.shape, fixed), pl.BlockSpec(w_vt.shape, fixed),
                  pl.BlockSpec(ones_col.shape, fixed), pl.BlockSpec(w_ga.shape, fixed),
                  pl.BlockSpec(wa2.shape, fixed), pl.BlockSpec((1, wk), fixed)],
        out_specs=[rowspec, rowspec, vtspec, kspec, kspec, kspec, rowspec, rowspec],
        out_shape=[out_bf, out_bf, out_vt, out_k, out_k, out_k, out_bf, out_f],
        compiler_params=_cparams(("parallel",)),
        name="proj_even",
    )(h, g, w_main, w_vt, ones_col, w_ga, wa2, ba)


def _proj_odd_kernel(h_ref, g_ref, w_ref, wvt_ref, q_ref, k_ref, vt_ref):
    d = q_ref.shape[1]
    dh = d // SB_HEADS
    parts = _row_parts(h_ref.shape[0])
    ms = [_rms(h_ref[r, :], g_ref[...]).astype(BF16) for r in parts]
    for p, (r, m) in enumerate(zip(parts, ms)):
        q_ref[r, :] = (_dot(m, w_ref[:, 0:d]) * (dh ** -0.5 * LOG2E)).astype(BF16)
        k_ref[r, :] = _dot(m, w_ref[:, d:2 * d]).astype(BF16)
        _store_key_blocks(vt_ref, _dot_nt(wvt_ref[...], m), p)


def _proj_odd(h, g, w_qkv):
    n, d = h.shape
    g, g_spec = _picked(*g)
    tm = min(TOKEN_TILE, n)
    tk = min(ATTN_TILE, tm)
    row = lambda i: (i, 0)
    fixed = lambda i: (0, 0)
    out = jax.ShapeDtypeStruct((n, d), BF16)
    w_qk = w_qkv[:, :2 * d].astype(BF16)
    w_vt = w_qkv[:, 2 * d:].T.astype(BF16)
    return pl.pallas_call(
        _proj_odd_kernel,
        grid=(n // tm,),
        in_specs=[pl.BlockSpec((tm, d), row), g_spec,
                  pl.BlockSpec(w_qk.shape, fixed), pl.BlockSpec(w_vt.shape, fixed)],
        out_specs=[pl.BlockSpec((tm, d), row), pl.BlockSpec((tm, d), row),
                   pl.BlockSpec((tm // tk, d, tk), lambda i: (i, 0, 0))],
        out_shape=[out, out, jax.ShapeDtypeStruct((n // tk, d, tk), BF16)],
        compiler_params=_cparams(("parallel",)),
        name="proj_odd",
    )(h, g, w_qk, w_vt)


def _diff_kernel(lam_ref, g_ref, q_ref, k_ref, vt_ref, o_ref, m_ref, l_ref, acc_ref, *, lambda_init):
    i = pl.program_id(2)
    tk = vt_ref.shape[2]
    dv = DIFF_DV
    nsub = q_ref.shape[0] // tk
    base = i * nsub
    lane = lax.broadcasted_iota(jnp.int32, (tk, LANES), 1)
    causal = (lax.broadcasted_iota(jnp.int32, (tk, tk), 0) <= lax.broadcasted_iota(jnp.int32, (tk, tk), 1))

    def q_map(a, mi):
        q = q_ref[a * tk:(a + 1) * tk, :]
        keep = (lane < DIFF_DH) if mi == 0 else (lane >= DIFF_DH)
        return jnp.where(keep, q, jnp.zeros_like(q))

    def run_tiles(tiles, scores):
        pending = []
        for (a, mi, vtbs, first, masked), ss in zip(tiles, scores):
            if masked:
                ss = [jnp.where(causal, s, -jnp.inf) for s in ss]
            m_blk = functools.reduce(jnp.maximum, [jnp.max(s, axis=0, keepdims=True) for s in ss])
            if first:
                alpha = None
                m_new = m_blk
            else:
                m_prev = m_ref[a, mi]
                m_new = jnp.maximum(m_prev, m_blk)
                alpha = jnp.exp2(m_prev - m_new)
            m_ref[a, mi] = m_new
            p = jnp.concatenate([jnp.exp2(s - m_new).astype(BF16) for s in ss], axis=0)
            vtb = jnp.concatenate(vtbs, axis=1)
            pending.append((a, mi, alpha, _dot(vtb, p)))
        for a, mi, alpha, pv in pending:
            if alpha is None:
                acc_ref[a, mi] = pv[:dv]
                l_ref[a, mi] = pv[dv:dv + 1]
            else:
                acc_ref[a, mi] = alpha * acc_ref[a, mi] + pv[:dv]
                l_ref[a, mi] = alpha * l_ref[a, mi] + pv[dv:dv + 1]

    def keys(j):
        return k_ref[pl.ds(pl.multiple_of(j * tk, tk), tk), :]

    band, band_scores = [], []
    for c in range(nsub):
        kb, vtb = keys(base + c), vt_ref[base + c]
        for a in range(c, nsub):
            for mi in range(2):
                band.append((a, mi, [vtb], c == 0, a == c))
                band_scores.append([_dot_nt(kb, q_map(a, mi))])
    run_tiles(band, band_scores)

    kv_unroll = DIFF_KV_UNROLL if nsub % DIFF_KV_UNROLL == 0 else 1
    group = DIFF_KV_GROUP if kv_unroll % DIFF_KV_GROUP == 0 else 1

    def body(step, carry):
        for r in range(0, kv_unroll, DIFF_RUN_BLOCKS):
            tiles, scores = [], []
            for u in range(r, min(r + DIFF_RUN_BLOCKS, kv_unroll), group):
                blocks = [step * kv_unroll + u + w for w in range(group)]
                kbs = [keys(j) for j in blocks]
                vtbs = [vt_ref[j] for j in blocks]
                for a in range(nsub):
                    for mi in range(2):
                        tiles.append((a, mi, vtbs, False, False))
                        scores.append([_dot_nt(kb, q_map(a, mi)) for kb in kbs])
            run_tiles(tiles, scores)
        return carry

    lax.fori_loop(0, base // kv_unroll, body, 0)

    lp = lam_ref[...]
    lam = (jnp.exp(jnp.sum(lp[0:1] * lp[1:2], axis=-1, keepdims=True))
           - jnp.exp(jnp.sum(lp[2:3] * lp[3:4], axis=-1, keepdims=True)) + lambda_init)
    for a in range(nsub):
        o = acc_ref[a, 0] / l_ref[a, 0] - lam * (acc_ref[a, 1] / l_ref[a, 1])
        o = o * lax.rsqrt(jnp.mean(o * o, axis=0, keepdims=True) + EPS) * (1.0 - lambda_init)
        o_ref[a * tk:(a + 1) * tk, :] = (o.T * g_ref[...]).astype(o_ref.dtype)


def _diff_attention(dq, dk, dvt, lam_params, subln_g, lambda_init, batch, seq):
    n, wd = dq.shape
    lam_params, lam_spec = _picked(*lam_params)
    subln_g, g_spec = _picked(*subln_g)
    tk = dvt.shape[2]
    tq = min(ATTN_QSUB * tk, seq)
    nq = seq // tq
    nsub = tq // tk
    qspec = pl.BlockSpec((tq, LANES), lambda b, h, i: (b * nq + i, h))
    kspec = pl.BlockSpec((seq, LANES), lambda b, h, i: (b, h), pipeline_mode=pl.Buffered(1))
    vtspec = pl.BlockSpec((seq // tk, DIFF_DV + ONES_ROWS, tk), lambda b, h, i: (b, h, 0),
                          pipeline_mode=pl.Buffered(1))
    return pl.pallas_call(
        functools.partial(_diff_kernel, lambda_init=lambda_init),
        grid=(batch, DIFF_HEADS, nq),
        in_specs=[lam_spec, g_spec, qspec, kspec, vtspec],
        out_specs=qspec,
        out_shape=jax.ShapeDtypeStruct((n, wd), BF16),
        scratch_shapes=[pltpu.VMEM((nsub, 2, 1, tk), F32), pltpu.VMEM((nsub, 2, 1, tk), F32),
                        pltpu.VMEM((nsub, 2, DIFF_DV, tk), F32)],
        compiler_params=_cparams(("parallel", "parallel", "arbitrary"), DIFF_VMEM_LIMIT),
        name="diff_attn",
    )(lam_params, subln_g, dq, dk, dvt)


def _gla_kernel(q_ref, k_ref, la_ref, v_ref, gg_ref, g_ref, o_ref, state_ref):
    c = GLA_CHUNK
    sub = GLA_SUB
    nsub = c // sub
    batch = q_ref.shape[0]
    heads = state_ref.shape[0] // batch
    pair_lane = lax.broadcasted_iota(jnp.int32, (c, LANES), 1)
    pair_lane3 = lax.broadcasted_iota(jnp.int32, (nsub, sub, LANES), 2)
    own = [pair_lane < GLA_DK, pair_lane >= GLA_DK]
    own3 = [pair_lane3 < GLA_DK, pair_lane3 >= GLA_DK]

    @pl.when(pl.program_id(0) == 0)
    def _():
        state_ref[...] = jnp.zeros(state_ref.shape, F32)

    r_i = lax.broadcasted_iota(jnp.int32, (c, c), 0)
    c_i = lax.broadcasted_iota(jnp.int32, (c, c), 1)
    tril = (c_i <= r_i).astype(BF16)
    levels = []
    s = c // 2
    while s >= sub:
        levels.append((s, (r_i // (2 * s) == c_i // (2 * s)) & (r_i % (2 * s) >= s) & (c_i % (2 * s) < s)))
        s //= 2
    diag_mask = (r_i // sub == c_i // sub) & (c_i <= r_i)
    lane3 = lax.broadcasted_iota(jnp.int32, (nsub, sub, c), 2)
    blk3 = lax.broadcasted_iota(jnp.int32, (nsub, sub, c), 0)
    g = g_ref[...]

    def step(si, carry):
        cols = [slice(h * LANES, (h + 1) * LANES) for h in range(heads)]
        rows = [pl.ds(pl.multiple_of((si * GLA_UNROLL + u) * c, c), c) for u in range(GLA_UNROLL)]

        pairs = [slice(p * LANES, (p + 1) * LANES) for p in range(heads // 2)]

        chains = [(b, u, p) for u in range(GLA_UNROLL) for b in range(batch) for p in range(heads // 2)]
        cums = {}
        for b, u, p in chains:
            la_hi, la_mid, la_lo = _split3(la_ref[b, rows[u], pairs[p]])
            cums[b, u, p] = _dot(tril, la_hi) + _dot(tril, la_mid) + _dot(tril, la_lo)

        states = [state_ref[i] for i in range(batch * heads)]
        mids = {}
        for b, u, p in chains:
            q = q_ref[b, rows[u], pairs[p]]
            k = k_ref[b, rows[u], pairs[p]]
            cum = cums[b, u, p]
            last = cum[c - 1:c, :]
            q_dec = (q * jnp.exp(cum)).astype(BF16)
            k_dec = k * jnp.exp(last - cum)
            state_decay = jnp.exp(last)

            level_ops = []
            for s, _ in levels:
                ref = jnp.concatenate([jnp.broadcast_to(cum[lo + s:lo + s + 1, :], (2 * s, LANES))
                                       for lo in range(0, c, 2 * s)], axis=0)
                qs = q * jnp.exp(jnp.minimum(cum - ref, 0.0))
                ks = k * jnp.exp(jnp.minimum(ref - cum, 0.0))
                level_ops.append((qs, ks.astype(BF16)))

            q3 = q.reshape(nsub, sub, LANES)
            k3 = k.reshape(nsub, sub, LANES)
            c3 = cum.reshape(nsub, sub, LANES)
            a3 = [jnp.zeros((nsub, sub, c), F32) for _ in range(2)]
            for j in range(sub):
                dec = jnp.exp(jnp.minimum(c3 - c3[:, j:j + 1, :], 0.0))
                prod = q3 * k3[:, j:j + 1, :] * dec
                for t in range(2):
                    col = jnp.sum(jnp.where(own3[t], prod, 0.0), axis=-1, keepdims=True)
                    a3[t] = jnp.where(lane3 == blk3 * sub + j, col, a3[t])

            for t in range(2):
                h = 2 * p + t
                v = v_ref[b, rows[u], cols[h]]
                state = states[b * heads + h]
                o_inter = _dot_nt(q_dec, state.astype(BF16))
                states[b * heads + h] = (state * state_decay
                                         + _dot_tn(v, jnp.where(own[t], k_dec, 0.0).astype(BF16)))
                attn = jnp.where(diag_mask, a3[t].reshape(c, c), 0.0)
                for (_, mask), (qs, ks) in zip(levels, level_ops):
                    a_level = _dot_nt(jnp.where(own[t], qs, 0.0).astype(BF16), ks)
                    attn = jnp.where(mask, a_level, attn)
                mids[b, u, h] = (o_inter, _dot(attn.astype(BF16), v))

        for i in range(batch * heads):
            state_ref[i] = states[i]
        for (b, u, h), (o_inter, o_intra) in mids.items():
            gate = gg_ref[b, rows[u], cols[h]]
            o_ref[b, rows[u], cols[h]] = (_rms(o_inter + o_intra, g)
                                          * (gate * jax.nn.sigmoid(gate))).astype(o_ref.dtype)
        return carry

    lax.fori_loop(0, q_ref.shape[1] // (c * GLA_UNROLL), step, 0)


def _gla(gq, gk, la, gv, gg, norm_g, batch, seq):
    n, wd = gv.shape
    norm_g, g_spec = _picked(*norm_g)
    t = min(GLA_TILE, seq)
    wk = gq.shape[1]
    spec = pl.BlockSpec((batch, t, wd), lambda i: (0, i, 0))
    kspec = pl.BlockSpec((batch, t, wk), lambda i: (0, i, 0))
    as_rows = lambda x: x.reshape(batch, seq, x.shape[1])
    out = pl.pallas_call(
        _gla_kernel,
        grid=(seq // t,),
        in_specs=[kspec, kspec, kspec, spec, spec, g_spec],
        out_specs=spec,
        out_shape=jax.ShapeDtypeStruct((batch, seq, wd), BF16),
        scratch_shapes=[pltpu.VMEM((batch * GLA_HEADS, GLA_DV, LANES), F32)],
        compiler_params=_cparams(("arbitrary",)),
        name="gla",
    )(as_rows(gq), as_rows(gk), as_rows(la), as_rows(gv), as_rows(gg), norm_g)
    return out.reshape(n, wd)


SUM_ROWS = 8


def _sb_kernel(q_ref, k_ref, vt_ref, o_ref, acc_ref):
    i = pl.program_id(2)
    tk = vt_ref.shape[2]
    nsub = q_ref.shape[0] // tk
    base = i * nsub
    r_s = lax.broadcasted_iota(jnp.int32, (tk + SUM_ROWS, tk), 0)
    c_s = lax.broadcasted_iota(jnp.int32, (tk + SUM_ROWS, tk), 1)
    suffix_mat = ((c_s > r_s) | (r_s >= tk)).astype(BF16)
    strict = (lax.broadcasted_iota(jnp.int32, (tk, tk), 0) < lax.broadcasted_iota(jnp.int32, (tk, tk), 1))

    def run_tiles(tiles, later):
        later = list(later)
        logits = [_dot_nt(kb, q_ref[a * tk:(a + 1) * tk, :]) for a, kb, _, _, _ in tiles]
        mids = []
        for (a, _, _, masked, _), z in zip(tiles, logits):
            log_term = jnp.log2(1.0 + jnp.exp2(_neg_abs(z)))
            log_beta = jnp.minimum(z, 0.0) - log_term
            log_keep = log_beta - z
            if masked:
                log_keep = jnp.where(strict, log_keep, 0.0)
            mids.append((log_beta, _dot(suffix_mat, log_keep.astype(BF16))))
        pending = []
        for (a, _, vtb, masked, valid), (log_beta, sums) in zip(tiles, mids):
            arg = log_beta + sums[:tk]
            if later[a] is not None:
                if valid is not None:
                    later[a] = jnp.where(valid, later[a], -jnp.inf)
                arg = arg + later[a]
            att = jnp.exp2(arg)
            if masked:
                att = jnp.where(strict, att, 0.0)
            pending.append((a, later[a] is None, _dot(vtb, att.astype(BF16))))
            later[a] = sums[tk:tk + 1] if later[a] is None else later[a] + sums[tk:tk + 1]
        for a, first, contrib in pending:
            if first:
                acc_ref[a] = contrib
            else:
                acc_ref[a] += contrib
        return tuple(later)

    def key_block(j):
        return k_ref[pl.ds(pl.multiple_of(j * tk, tk), tk), :], vt_ref[j]

    def lockstep_tiles(offset):
        tiles = []
        for a in range(nsub):
            j = base + a - offset
            kb, vtb = key_block(jnp.maximum(j, 0))
            tiles.append((a, kb, vtb, False, j >= 0))
        return tiles

    diag = []
    for a in range(nsub):
        kb, vtb = key_block(base + a)
        diag.append((a, kb, vtb, True, None))
    later = run_tiles(diag + lockstep_tiles(1), [None] * nsub)

    def alive(later):
        return (jnp.max(functools.reduce(jnp.maximum, later)) > LOG2_FLUSH).astype(jnp.int32)

    def cond(carry):
        step, live, _ = carry
        return (step < base + nsub - 2) & (live > 0)

    def body(carry):
        step, _, later = carry
        later = run_tiles(lockstep_tiles(step + 2), later)
        return step + 1, alive(later), later

    lax.while_loop(cond, body, (jnp.int32(0), alive(later), later))
    for a in range(nsub):
        o_ref[a * tk:(a + 1) * tk, :] = acc_ref[a].T.astype(o_ref.dtype)


def _stick_breaking(q, k, vt, batch, seq):
    n, d = q.shape
    tk = vt.shape[2]
    tq = min(SB_QSUB * tk, seq)
    nq = seq // tq
    qspec = pl.BlockSpec((tq, LANES), lambda b, h, i: (b * nq + i, h))
    kspec = pl.BlockSpec((seq, LANES), lambda b, h, i: (b, h))
    vtspec = pl.BlockSpec((seq // tk, LANES, tk), lambda b, h, i: (b, h, 0))
    return pl.pallas_call(
        _sb_kernel,
        grid=(batch, SB_HEADS, nq),
        in_specs=[qspec, kspec, vtspec],
        out_specs=qspec,
        out_shape=jax.ShapeDtypeStruct((n, d), BF16),
        scratch_shapes=[pltpu.VMEM((tq // tk, LANES, tk), F32)],
        compiler_params=_cparams(("parallel", "parallel", "arbitrary")),
        name="stick_breaking",
    )(q, k, vt)


def kernel(x, ffn_pre_g, ffn_post_g, ffn_w_gate, ffn_w_up, ffn_w_down, mix_pre_g, mix_post_g, hyb_w_in, hyb_w_out, diff_lambda, diff_subln_g, gla_w_a2, gla_b_a, gla_norm_g, sb_w_qkv, sb_w_out):
    batch, seq, d = x.shape
    depth = ffn_pre_g.shape[0]
    assert d // SB_HEADS == LANES and DIFF_DV == LANES and GLA_DV == LANES
    h = x.reshape(batch * seq, d)

    n_ffn = ffn_w_gate.shape[1]
    dff = ffn_w_gate.shape[-1]
    wgu_all = jnp.concatenate([ffn_w_gate, ffn_w_up], axis=-1).astype(BF16).reshape(depth * n_ffn, d, 2 * dff)
    wd_all = ffn_w_down.astype(BF16).reshape(depth * n_ffn, dff, d)

    ffn_pre = ffn_pre_g.reshape(depth * n_ffn, 1, d)
    ffn_post = ffn_post_g.reshape(depth * n_ffn, 1, d)
    mix_pre = mix_pre_g.reshape(depth, 1, d)
    mix_post = mix_post_g.reshape(depth, 1, d)
    subln = diff_subln_g.reshape(-1, 1, DIFF_DV)
    gla_g = gla_norm_g.reshape(-1, 1, GLA_DV)
    hyb_out = hyb_w_out.astype(BF16)
    sb_out = sb_w_out.astype(BF16)

    def ffn(h, layer, j, mix=None):
        idx = layer * n_ffn + j
        return _ffn(h, (ffn_pre, idx), wgu_all, wd_all, idx, (ffn_post, idx), mix)

    for layer in range(depth):
        h = ffn(h, layer, 0)
        if layer % 2 == 0:
            e = layer // 2
            lambda_init = 0.8 - 0.6 * math.exp(-0.3 * layer)
            dq, dk, dvt, gq, gk, la, gv, gg = _proj_even(h, (mix_pre, layer), hyb_w_in[e], gla_w_a2[e], gla_b_a[e])
            a_out = _diff_attention(dq, dk, dvt, (diff_lambda, e), (subln, e), lambda_init, batch, seq)
            b_out = _gla(gq, gk, la, gv, gg, (gla_g, e), batch, seq)
            mix = (a_out, 0, b_out, 0, (hyb_out, e), (mix_post, layer))
        else:
            o = layer // 2
            q, k, vt = _proj_odd(h, (mix_pre, layer), sb_w_qkv[o])
            att = _stick_breaking(q, k, vt, batch, seq)
            mix = (att, 0, att, 1, (sb_out, o), (mix_post, layer))
        h = ffn(h, layer, 1, mix)
    return h.reshape(batch, seq, d)
```

```python
import functools
import math

import jax
import jax.numpy as jnp
from jax import lax
from jax.experimental import pallas as pl
from jax.experimental.pallas import tpu as pltpu

F32 = jnp.float32
BF16 = jnp.bfloat16

EPS = 1e-6
DIFF_HEADS = 4
DIFF_DH = 64
DIFF_DV = 2 * DIFF_DH
GLA_HEADS = 4
GLA_DK = 64
GLA_DV = 128
GLA_RANK = 16
GLA_TAU = 16.0
GLA_CHUNK = 64
GLA_UNROLL = 8
GLA_SUB = 8
SB_HEADS = 8

LANES = 128
VMEM_LIMIT = 56 * 1024 * 1024
ATTN_VMEM_LIMIT = 62 * 1024 * 1024

TOKEN_TILE = 1024
FFN_TILE = 1024
ROW_PART = 256
GLA_TILE = 512
ATTN_TILE = 256
ATTN_QSUB = 8
SB_QSUB = 32
DIFF_KV_UNROLL = 8
DIFF_RUN_BLOCKS = 8
DIFF_KV_GROUP = 2
ONES_ROWS = 16
LOG2_FLUSH = -150.0
LOG2E = math.log2(math.e)


def _cparams(sem, vmem_limit=VMEM_LIMIT):
    return pltpu.CompilerParams(dimension_semantics=sem, vmem_limit_bytes=vmem_limit)


def _picked(stacked, idx):
    return stacked, pl.BlockSpec((None,) + stacked.shape[1:], lambda *_: (idx,) + (0,) * (stacked.ndim - 1))


def _rms(x, g):
    return x * lax.rsqrt(jnp.mean(x * x, axis=-1, keepdims=True) + EPS) * g


def _dot(a, b):
    return jnp.dot(a, b, preferred_element_type=F32)


def _dot_nt(a, b):
    return lax.dot_general(a, b, (((1,), (1,)), ((), ())), preferred_element_type=F32)


def _dot_tn(a, b):
    return lax.dot_general(a, b, (((0,), (0,)), ((), ())), preferred_element_type=F32)


def _split3(x):
    hi = x.astype(BF16)
    r1 = x - hi.astype(F32)
    mid = r1.astype(BF16)
    lo = (r1 - mid.astype(F32)).astype(BF16)
    return hi, mid, lo


def _neg_abs(x):
    bits = lax.bitcast_convert_type(x, jnp.int32) | jnp.int32(-2 ** 31)
    return lax.bitcast_convert_type(bits, F32)


def _softplus_parts(z):
    return jnp.maximum(z, 0.0), jnp.log(1.0 + jnp.exp(-jnp.abs(z)))


def _ffn_body(hs, gpre_ref, wgu_ref, wd_ref, gpost_ref):
    dff = wd_ref.shape[0]
    xns = [_rms(h, gpre_ref[...]).astype(BF16) for h in hs]
    acts = []
    for xn in xns:
        gu = _dot(xn, wgu_ref[...])
        gate = gu[:, :dff]
        acts.append((gate * jax.nn.sigmoid(gate) * gu[:, dff:]).astype(BF16))
    fs = [_dot(act, wd_ref[...]) for act in acts]
    return [h + 0.5 * _rms(f, gpost_ref[...]) for h, f in zip(hs, fs)]


def _row_parts(tm):
    part = min(ROW_PART, tm)
    return [pl.ds(p * part, part) for p in range(tm // part)]


def _ffn_kernel(h_ref, gpre_ref, wgu_ref, wd_ref, gpost_ref, o_ref):
    parts = _row_parts(h_ref.shape[0])
    outs = _ffn_body([h_ref[r, :] for r in parts], gpre_ref, wgu_ref, wd_ref, gpost_ref)
    for r, o in zip(parts, outs):
        o_ref[r, :] = o


def _mix_ffn_kernel(h_ref, a_ref, b_ref, wo_ref, gmix_ref, gpre_ref, wgu_ref, wd_ref, gpost_ref, o_ref):
    parts = _row_parts(h_ref.shape[0])
    half = a_ref.shape[1]
    ys = [_dot(a_ref[r, :], wo_ref[0:half, :]) + _dot(b_ref[r, :], wo_ref[half:, :]) for r in parts]
    hs = [h_ref[r, :] + _rms(y, gmix_ref[...]) for r, y in zip(parts, ys)]
    outs = _ffn_body(hs, gpre_ref, wgu_ref, wd_ref, gpost_ref)
    for r, o in zip(parts, outs):
        o_ref[r, :] = o


def _ffn(h, g_pre, wgu_all, wd_all, idx, g_post, mix=None):
    n, d = h.shape
    dff = wd_all.shape[1]
    tm = min(FFN_TILE, n)
    row = lambda i: (i, 0)
    g_pre, g_pre_spec = _picked(*g_pre)
    g_post, g_post_spec = _picked(*g_post)
    wgu_all, wgu_spec = _picked(wgu_all, idx)
    wd_all, wd_spec = _picked(wd_all, idx)
    ffn_specs = [g_pre_spec, wgu_spec, wd_spec, g_post_spec]
    ffn_args = (g_pre, wgu_all, wd_all, g_post)
    if mix is None:
        body, specs, args, name = _ffn_kernel, [], (), "ffn"
    else:
        a, a_col, b, b_col, w_out, g_mix = mix
        w_out, w_out_spec = _picked(*w_out)
        g_mix, g_mix_spec = _picked(*g_mix)
        half = w_out.shape[1] // 2
        body, name = _mix_ffn_kernel, "mix_ffn"
        specs = [pl.BlockSpec((tm, half), lambda i: (i, a_col)), pl.BlockSpec((tm, half), lambda i: (i, b_col)),
                 w_out_spec, g_mix_spec]
        args = (a, b, w_out, g_mix)
    return pl.pallas_call(
        body,
        grid=(n // tm,),
        in_specs=[pl.BlockSpec((tm, d), row)] + specs + ffn_specs,
        out_specs=pl.BlockSpec((tm, d), row),
        out_shape=jax.ShapeDtypeStruct((n, d), F32),
        compiler_params=_cparams(("parallel",)),
        name=name,
    )(h, *args, *ffn_args)


def _store_key_blocks(vt_ref, vt, part):
    tk = vt_ref.shape[2]
    blocks = vt.shape[1] // tk
    for c in range(blocks):
        vt_ref[part * blocks + c] = vt[:, c * tk:(c + 1) * tk].astype(vt_ref.dtype)


def _proj_even_kernel(h_ref, g_ref, w_ref, wvt_ref, ones_ref, wga_ref, wa2_ref, ba_ref,
                      dq_ref, dk_ref, dvt_ref, gq_ref, gk_ref, la_ref, gv_ref, gg_ref):
    w_hi, w_lo, _ = _split3(wa2_ref[...])
    widths = [r.shape[1] for r in (dq_ref, dk_ref, gq_ref, gk_ref, gv_ref, gg_ref)]
    starts = [sum(widths[:i]) for i in range(len(widths))]
    parts = _row_parts(h_ref.shape[0])
    ms = [_rms(h_ref[r, :], g_ref[...]).astype(BF16) for r in parts]
    for p, (r, m) in enumerate(zip(parts, ms)):
        def seg(i):
            return _dot(m, w_ref[:, starts[i]:starts[i] + widths[i]])

        dq_ref[r, :] = (seg(0) * (DIFF_DH ** -0.5 * LOG2E)).astype(BF16)
        dk_ref[r, :] = seg(1).astype(BF16)
        _store_key_blocks(dvt_ref, _dot_nt(wvt_ref[...], m) + ones_ref[...], p)
        gq_ref[r, :] = seg(2) * GLA_DK ** -0.5
        gk_ref[r, :] = seg(3)
        gv_ref[r, :] = seg(4).astype(BF16)
        gg_ref[r, :] = seg(5)
        ga_hi, ga_lo, _ = _split3(_dot(m, wga_ref[...]))
        x = _dot(ga_hi, w_hi) + _dot(ga_lo, w_hi) + _dot(ga_hi, w_lo) + ba_ref[...]
        relu_neg, log_term = _softplus_parts(-x)
        la_ref[r, :] = -(relu_neg + log_term) * (1.0 / GLA_TAU)


def _proj_even(h, g, w_in, w_a2, b_a):
    n, d = h.shape
    g, g_spec = _picked(*g)
    tm = min(TOKEN_TILE, n)
    wd = DIFF_HEADS * DIFF_DV
    wk = GLA_HEADS * GLA_DK
    c = 0
    parts = []
    for width in (wd, wd, wd, wk, wk, GLA_HEADS * GLA_DV, GLA_HEADS * GLA_DV):
        parts.append(w_in[:, c:c + width])
        c += width
    w_vt = parts.pop(2).T.reshape(DIFF_HEADS, DIFF_DV, d)
    w_vt = jnp.pad(w_vt, ((0, 0), (0, ONES_ROWS), (0, 0))).reshape(-1, d).astype(BF16)
    ones_col = jnp.pad(jnp.zeros((DIFF_HEADS, DIFF_DV, 1), F32), ((0, 0), (0, ONES_ROWS), (0, 0)),
                       constant_values=1.0).reshape(-1, 1)
    vt_rows = w_vt.shape[0]
    w_main = jnp.concatenate(parts, axis=1).astype(BF16)
    tk = min(ATTN_TILE, tm)
    assert min(ROW_PART, tm) % tk == 0
    w_ga = jnp.pad(w_in[:, c:c + GLA_RANK], ((0, 0), (0, LANES - GLA_RANK))).astype(BF16)
    wa2 = jnp.pad(w_a2, ((0, LANES - GLA_RANK), (0, 0)))
    ba = b_a.reshape(1, wk)
    row = lambda i: (i, 0)
    fixed = lambda i: (0, 0)
    out_bf = jax.ShapeDtypeStruct((n, wd), BF16)
    out_f = jax.ShapeDtypeStruct((n, wd), F32)
    out_k = jax.ShapeDtypeStruct((n, wk), F32)
    out_vt = jax.ShapeDtypeStruct((n // tk, vt_rows, tk), BF16)
    rowspec = pl.BlockSpec((tm, wd), row)
    kspec = pl.BlockSpec((tm, wk), row)
    vtspec = pl.BlockSpec((tm // tk, vt_rows, tk), lambda i: (i, 0, 0))
    return pl.pallas_call(
        _proj_even_kernel,
        grid=(n // tm,),
        in_specs=[pl.BlockSpec((tm, d), row), g_spec,
                  pl.BlockSpec(w_main.shape, fixed), pl.BlockSpec(w_vt.shape, fixed),
                  pl.BlockSpec(ones_col.shape, fixed), pl.BlockSpec(w_ga.shape, fixed),
                  pl.BlockSpec(wa2.shape, fixed), pl.BlockSpec((1, wk), fixed)],
        out_specs=[rowspec, rowspec, vtspec, kspec, kspec, kspec, rowspec, rowspec],
        out_shape=[out_bf, out_bf, out_vt, out_k, out_k, out_k, out_bf, out_f],
        compiler_params=_cparams(("parallel",)),
        name="proj_even",
    )(h, g, w_main, w_vt, ones_col, w_ga, wa2, ba)


def _proj_odd_kernel(h_ref, g_ref, w_ref, wvt_ref, q_ref, k_ref, vt_ref):
    d = q_ref.shape[1]
    dh = d // SB_HEADS
    parts = _row_parts(h_ref.shape[0])
    ms = [_rms(h_ref[r, :], g_ref[...]).astype(BF16) for r in parts]
    for p, (r, m) in enumerate(zip(parts, ms)):
        q_ref[r, :] = (_dot(m, w_ref[:, 0:d]) * (dh ** -0.5 * LOG2E)).astype(BF16)
        k_ref[r, :] = _dot(m, w_ref[:, d:2 * d]).astype(BF16)
        _store_key_blocks(vt_ref, _dot_nt(wvt_ref[...], m), p)


def _proj_odd(h, g, w_qkv):
    n, d = h.shape
    g, g_spec = _picked(*g)
    tm = min(TOKEN_TILE, n)
    tk = min(ATTN_TILE, tm)
    row = lambda i: (i, 0)
    fixed = lambda i: (0, 0)
    out = jax.ShapeDtypeStruct((n, d), BF16)
    w_qk = w_qkv[:, :2 * d].astype(BF16)
    w_vt = w_qkv[:, 2 * d:].T.astype(BF16)
    return pl.pallas_call(
        _proj_odd_kernel,
        grid=(n // tm,),
        in_specs=[pl.BlockSpec((tm, d), row), g_spec,
                  pl.BlockSpec(w_qk.shape, fixed), pl.BlockSpec(w_vt.shape, fixed)],
        out_specs=[pl.BlockSpec((tm, d), row), pl.BlockSpec((tm, d), row),
                   pl.BlockSpec((tm // tk, d, tk), lambda i: (i, 0, 0))],
        out_shape=[out, out, jax.ShapeDtypeStruct((n // tk, d, tk), BF16)],
        compiler_params=_cparams(("parallel",)),
        name="proj_odd",
    )(h, g, w_qk, w_vt)


def _diff_kernel(lam_ref, g_ref, q_ref, k_ref, vt_ref, o_ref, m_ref, l_ref, acc_ref, *, lambda_init):
    i = pl.program_id(2)
    tk = vt_ref.shape[2]
    dv = DIFF_DV
    nsub = q_ref.shape[0] // tk
    base = i * nsub
    lane = lax.broadcasted_iota(jnp.int32, (tk, LANES), 1)
    causal = (lax.broadcasted_iota(jnp.int32, (tk, tk), 0) <= lax.broadcasted_iota(jnp.int32, (tk, tk), 1))

    def q_map(a, mi):
        q = q_ref[a * tk:(a + 1) * tk, :]
        keep = (lane < DIFF_DH) if mi == 0 else (lane >= DIFF_DH)
        return jnp.where(keep, q, jnp.zeros_like(q))

    def run_tiles(tiles, scores):
        pending = []
        for (a, mi, vtbs, first, masked), ss in zip(tiles, scores):
            if masked:
                ss = [jnp.where(causal, s, -jnp.inf) for s in ss]
            m_blk = functools.reduce(jnp.maximum, [jnp.max(s, axis=0, keepdims=True) for s in ss])
            if first:
                alpha = None
                m_new = m_blk
            else:
                m_prev = m_ref[a, mi]
                m_new = jnp.maximum(m_prev, m_blk)
                alpha = jnp.exp2(m_prev - m_new)
            m_ref[a, mi] = m_new
            p = jnp.concatenate([jnp.exp2(s - m_new).astype(BF16) for s in ss], axis=0)
            vtb = jnp.concatenate(vtbs, axis=1)
            pending.append((a, mi, alpha, _dot(vtb, p)))
        for a, mi, alpha, pv in pending:
            if alpha is None:
                acc_ref[a, mi] = pv[:dv]
                l_ref[a, mi] = pv[dv:dv + 1]
            else:
                acc_ref[a, mi] = alpha * acc_ref[a, mi] + pv[:dv]
                l_ref[a, mi] = alpha * l_ref[a, mi] + pv[dv:dv + 1]

    def keys(j):
        return k_ref[pl.ds(pl.multiple_of(j * tk, tk), tk), :]

    band, band_scores = [], []
    for c in range(nsub):
        kb, vtb = keys(base + c), vt_ref[base + c]
        for a in range(c, nsub):
            for mi in range(2):
                band.append((a, mi, [vtb], c == 0, a == c))
                band_scores.append([_dot_nt(kb, q_map(a, mi))])
    run_tiles(band, band_scores)

    kv_unroll = DIFF_KV_UNROLL if nsub % DIFF_KV_UNROLL == 0 else 1
    group = DIFF_KV_GROUP if kv_unroll % DIFF_KV_GROUP == 0 else 1

    def body(step, carry):
        for r in range(0, kv_unroll, DIFF_RUN_BLOCKS):
            tiles, scores = [], []
            for u in range(r, min(r + DIFF_RUN_BLOCKS, kv_unroll), group):
                blocks = [step * kv_unroll + u + w for w in range(group)]
                kbs = [keys(j) for j in blocks]
                vtbs = [vt_ref[j] for j in blocks]
                for a in range(nsub):
                    for mi in range(2):
                        tiles.append((a, mi, vtbs, False, False))
                        scores.append([_dot_nt(kb, q_map(a, mi)) for kb in kbs])
            run_tiles(tiles, scores)
        return carry

    lax.fori_loop(0, base // kv_unroll, body, 0)

    lp = lam_ref[...]
    lam = (jnp.exp(jnp.sum(lp[0:1] * lp[1:2], axis=-1, keepdims=True))
           - jnp.exp(jnp.sum(lp[2:3] * lp[3:4], axis=-1, keepdims=True)) + lambda_init)
    for a in range(nsub):
        o = acc_ref[a, 0] / l_ref[a, 0] - lam * (acc_ref[a, 1] / l_ref[a, 1])
        o = o * lax.rsqrt(jnp.mean(o * o, axis=0, keepdims=True) + EPS) * (1.0 - lambda_init)
        o_ref[a * tk:(a + 1) * tk, :] = (o.T * g_ref[...]).astype(o_ref.dtype)


def _diff_attention(dq, dk, dvt, lam_params, subln_g, lambda_init, batch, seq):
    n, wd = dq.shape
    lam_params, lam_spec = _picked(*lam_params)
    subln_g, g_spec = _picked(*subln_g)
    tk = dvt.shape[2]
    tq = min(ATTN_QSUB * tk, seq)
    nq = seq // tq
    nsub = tq // tk
    qspec = pl.BlockSpec((tq, LANES), lambda b, h, i: (b * nq + i, h))
    kspec = pl.BlockSpec((seq, LANES), lambda b, h, i: (b, h), pipeline_mode=pl.Buffered(1))
    vtspec = pl.BlockSpec((seq // tk, DIFF_DV + ONES_ROWS, tk), lambda b, h, i: (b, h, 0),
                          pipeline_mode=pl.Buffered(1))
    return pl.pallas_call(
        functools.partial(_diff_kernel, lambda_init=lambda_init),
        grid=(batch, DIFF_HEADS, nq),
        in_specs=[lam_spec, g_spec, qspec, kspec, vtspec],
        out_specs=qspec,
        out_shape=jax.ShapeDtypeStruct((n, wd), BF16),
        scratch_shapes=[pltpu.VMEM((nsub, 2, 1, tk), F32), pltpu.VMEM((nsub, 2, 1, tk), F32),
                        pltpu.VMEM((nsub, 2, DIFF_DV, tk), F32)],
        compiler_params=_cparams(("parallel", "parallel", "arbitrary"), ATTN_VMEM_LIMIT),
        name="diff_attn",
    )(lam_params, subln_g, dq, dk, dvt)


def _gla_kernel(q_ref, k_ref, la_ref, v_ref, gg_ref, g_ref, o_ref, state_ref):
    c = GLA_CHUNK
    sub = GLA_SUB
    nsub = c // sub
    batch = q_ref.shape[0]
    heads = state_ref.shape[0] // batch
    pair_lane = lax.broadcasted_iota(jnp.int32, (c, LANES), 1)
    pair_lane3 = lax.broadcasted_iota(jnp.int32, (nsub, sub, LANES), 2)
    own = [pair_lane < GLA_DK, pair_lane >= GLA_DK]
    own3 = [pair_lane3 < GLA_DK, pair_lane3 >= GLA_DK]

    @pl.when(pl.program_id(0) == 0)
    def _():
        state_ref[...] = jnp.zeros(state_ref.shape, F32)

    r_i = lax.broadcasted_iota(jnp.int32, (c, c), 0)
    c_i = lax.broadcasted_iota(jnp.int32, (c, c), 1)
    tril = (c_i <= r_i).astype(BF16)
    levels = []
    s = c // 2
    while s >= sub:
        levels.append((s, (r_i // (2 * s) == c_i // (2 * s)) & (r_i % (2 * s) >= s) & (c_i % (2 * s) < s)))
        s //= 2
    diag_mask = (r_i // sub == c_i // sub) & (c_i <= r_i)
    lane3 = lax.broadcasted_iota(jnp.int32, (nsub, sub, c), 2)
    blk3 = lax.broadcasted_iota(jnp.int32, (nsub, sub, c), 0)
    g = g_ref[...]

    def step(si, carry):
        cols = [slice(h * LANES, (h + 1) * LANES) for h in range(heads)]
        rows = [pl.ds(pl.multiple_of((si * GLA_UNROLL + u) * c, c), c) for u in range(GLA_UNROLL)]

        pairs = [slice(p * LANES, (p + 1) * LANES) for p in range(heads // 2)]

        chains = [(b, u, p) for u in range(GLA_UNROLL) for b in range(batch) for p in range(heads // 2)]
        cums = {}
        for b, u, p in chains:
            la_hi, la_mid, la_lo = _split3(la_ref[b, rows[u], pairs[p]])
            cums[b, u, p] = _dot(tril, la_hi) + _dot(tril, la_mid) + _dot(tril, la_lo)

        states = [state_ref[i] for i in range(batch * heads)]
        mids = {}
        for b, u, p in chains:
            q = q_ref[b, rows[u], pairs[p]]
            k = k_ref[b, rows[u], pairs[p]]
            cum = cums[b, u, p]
            last = cum[c - 1:c, :]
            q_dec = (q * jnp.exp(cum)).astype(BF16)
            k_dec = k * jnp.exp(last - cum)
            state_decay = jnp.exp(last)

            level_ops = []
            for s, _ in levels:
                ref = jnp.concatenate([jnp.broadcast_to(cum[lo + s:lo + s + 1, :], (2 * s, LANES))
                                       for lo in range(0, c, 2 * s)], axis=0)
                qs = q * jnp.exp(jnp.minimum(cum - ref, 0.0))
                ks = k * jnp.exp(jnp.minimum(ref - cum, 0.0))
                level_ops.append((qs, ks.astype(BF16)))

            q3 = q.reshape(nsub, sub, LANES)
            k3 = k.reshape(nsub, sub, LANES)
            c3 = cum.reshape(nsub, sub, LANES)
            a3 = [jnp.zeros((nsub, sub, c), F32) for _ in range(2)]
            for j in range(sub):
                dec = jnp.exp(jnp.minimum(c3 - c3[:, j:j + 1, :], 0.0))
                prod = q3 * k3[:, j:j + 1, :] * dec
                for t in range(2):
                    col = jnp.sum(jnp.where(own3[t], prod, 0.0), axis=-1, keepdims=True)
                    a3[t] = jnp.where(lane3 == blk3 * sub + j, col, a3[t])

            for t in range(2):
                h = 2 * p + t
                v = v_ref[b, rows[u], cols[h]]
                state = states[b * heads + h]
                o_inter = _dot_nt(q_dec, state.astype(BF16))
                states[b * heads + h] = (state * state_decay
                                         + _dot_tn(v, jnp.where(own[t], k_dec, 0.0).astype(BF16)))
                attn = jnp.where(diag_mask, a3[t].reshape(c, c), 0.0)
                for (_, mask), (qs, ks) in zip(levels, level_ops):
                    a_level = _dot_nt(jnp.where(own[t], qs, 0.0).astype(BF16), ks)
                    attn = jnp.where(mask, a_level, attn)
                mids[b, u, h] = (o_inter, _dot(attn.astype(BF16), v))

        for i in range(batch * heads):
            state_ref[i] = states[i]
        for (b, u, h), (o_inter, o_intra) in mids.items():
            gate = gg_ref[b, rows[u], cols[h]]
            o_ref[b, rows[u], cols[h]] = (_rms(o_inter + o_intra, g)
                                          * (gate * jax.nn.sigmoid(gate))).astype(o_ref.dtype)
        return carry

    lax.fori_loop(0, q_ref.shape[1] // (c * GLA_UNROLL), step, 0)


def _gla(gq, gk, la, gv, gg, norm_g, batch, seq):
    n, wd = gv.shape
    norm_g, g_spec = _picked(*norm_g)
    t = min(GLA_TILE, seq)
    wk = gq.shape[1]
    spec = pl.BlockSpec((batch, t, wd), lambda i: (0, i, 0))
    kspec = pl.BlockSpec((batch, t, wk), lambda i: (0, i, 0))
    as_rows = lambda x: x.reshape(batch, seq, x.shape[1])
    out = pl.pallas_call(
        _gla_kernel,
        grid=(seq // t,),
        in_specs=[kspec, kspec, kspec, spec, spec, g_spec],
        out_specs=spec,
        out_shape=jax.ShapeDtypeStruct((batch, seq, wd), BF16),
        scratch_shapes=[pltpu.VMEM((batch * GLA_HEADS, GLA_DV, LANES), F32)],
        compiler_params=_cparams(("arbitrary",)),
        name="gla",
    )(as_rows(gq), as_rows(gk), as_rows(la), as_rows(gv), as_rows(gg), norm_g)
    return out.reshape(n, wd)


SUM_ROWS = 8


def _sb_kernel(q_ref, k_ref, vt_ref, o_ref, acc_ref):
    i = pl.program_id(2)
    tk = vt_ref.shape[2]
    nsub = q_ref.shape[0] // tk
    base = i * nsub
    r_s = lax.broadcasted_iota(jnp.int32, (tk + SUM_ROWS, tk), 0)
    c_s = lax.broadcasted_iota(jnp.int32, (tk + SUM_ROWS, tk), 1)
    suffix_mat = ((c_s > r_s) | (r_s >= tk)).astype(BF16)
    strict = (lax.broadcasted_iota(jnp.int32, (tk, tk), 0) < lax.broadcasted_iota(jnp.int32, (tk, tk), 1))

    def run_tiles(tiles, later):
        later = list(later)
        logits = [_dot_nt(kb, q_ref[a * tk:(a + 1) * tk, :]) for a, kb, _, _, _ in tiles]
        mids = []
        for (a, _, _, masked, _), z in zip(tiles, logits):
            log_term = jnp.log2(1.0 + jnp.exp2(_neg_abs(z)))
            log_beta = jnp.minimum(z, 0.0) - log_term
            log_keep = log_beta - z
            if masked:
                log_keep = jnp.where(strict, log_keep, 0.0)
            mids.append((log_beta, _dot(suffix_mat, log_keep.astype(BF16))))
        pending = []
        for (a, _, vtb, masked, valid), (log_beta, sums) in zip(tiles, mids):
            arg = log_beta + sums[:tk]
            if later[a] is not None:
                if valid is not None:
                    later[a] = jnp.where(valid, later[a], -jnp.inf)
                arg = arg + later[a]
            att = jnp.exp2(arg)
            if masked:
                att = jnp.where(strict, att, 0.0)
            pending.append((a, later[a] is None, _dot(vtb, att.astype(BF16))))
            later[a] = sums[tk:tk + 1] if later[a] is None else later[a] + sums[tk:tk + 1]
        for a, first, contrib in pending:
            if first:
                acc_ref[a] = contrib
            else:
                acc_ref[a] += contrib
        return tuple(later)

    def key_block(j):
        return k_ref[pl.ds(pl.multiple_of(j * tk, tk), tk), :], vt_ref[j]

    def lockstep_tiles(offset):
        tiles = []
        for a in range(nsub):
            j = base + a - offset
            kb, vtb = key_block(jnp.maximum(j, 0))
            tiles.append((a, kb, vtb, False, j >= 0))
        return tiles

    diag = []
    for a in range(nsub):
        kb, vtb = key_block(base + a)
        diag.append((a, kb, vtb, True, None))
    later = run_tiles(diag + lockstep_tiles(1), [None] * nsub)

    def alive(later):
        return (jnp.max(functools.reduce(jnp.maximum, later)) > LOG2_FLUSH).astype(jnp.int32)

    def cond(carry):
        step, live, _ = carry
        return (step < base + nsub - 2) & (live > 0)

    def body(carry):
        step, _, later = carry
        later = run_tiles(lockstep_tiles(step + 2), later)
        return step + 1, alive(later), later

    lax.while_loop(cond, body, (jnp.int32(0), alive(later), later))
    for a in range(nsub):
        o_ref[a * tk:(a + 1) * tk, :] = acc_ref[a].T.astype(o_ref.dtype)


def _stick_breaking(q, k, vt, batch, seq):
    n, d = q.shape
    tk = vt.shape[2]
    tq = min(SB_QSUB * tk, seq)
    nq = seq // tq
    qspec = pl.BlockSpec((tq, LANES), lambda b, h, i: (b * nq + i, h))
    kspec = pl.BlockSpec((seq, LANES), lambda b, h, i: (b, h), pipeline_mode=pl.Buffered(1))
    vtspec = pl.BlockSpec((seq // tk, LANES, tk), lambda b, h, i: (b, h, 0), pipeline_mode=pl.Buffered(1))
    return pl.pallas_call(
        _sb_kernel,
        grid=(batch, SB_HEADS, nq),
        in_specs=[qspec, kspec, vtspec],
        out_specs=qspec,
        out_shape=jax.ShapeDtypeStruct((n, d), BF16),
        scratch_shapes=[pltpu.VMEM((tq // tk, LANES, tk), F32)],
        compiler_params=_cparams(("parallel", "parallel", "arbitrary"), ATTN_VMEM_LIMIT),
        name="stick_breaking",
    )(q, k, vt)


def kernel(x, ffn_pre_g, ffn_post_g, ffn_w_gate, ffn_w_up, ffn_w_down, mix_pre_g, mix_post_g, hyb_w_in, hyb_w_out, diff_lambda, diff_subln_g, gla_w_a2, gla_b_a, gla_norm_g, sb_w_qkv, sb_w_out):
    batch, seq, d = x.shape
    depth = ffn_pre_g.shape[0]
    assert d // SB_HEADS == LANES and DIFF_DV == LANES and GLA_DV == LANES
    h = x.reshape(batch * seq, d)

    n_ffn = ffn_w_gate.shape[1]
    dff = ffn_w_gate.shape[-1]
    wgu_all = jnp.concatenate([ffn_w_gate, ffn_w_up], axis=-1).astype(BF16).reshape(depth * n_ffn, d, 2 * dff)
    wd_all = ffn_w_down.astype(BF16).reshape(depth * n_ffn, dff, d)

    ffn_pre = ffn_pre_g.reshape(depth * n_ffn, 1, d)
    ffn_post = ffn_post_g.reshape(depth * n_ffn, 1, d)
    mix_pre = mix_pre_g.reshape(depth, 1, d)
    mix_post = mix_post_g.reshape(depth, 1, d)
    subln = diff_subln_g.reshape(-1, 1, DIFF_DV)
    gla_g = gla_norm_g.reshape(-1, 1, GLA_DV)
    hyb_out = hyb_w_out.astype(BF16)
    sb_out = sb_w_out.astype(BF16)

    def ffn(h, layer, j, mix=None):
        idx = layer * n_ffn + j
        return _ffn(h, (ffn_pre, idx), wgu_all, wd_all, idx, (ffn_post, idx), mix)

    for layer in range(depth):
        h = ffn(h, layer, 0)
        if layer % 2 == 0:
            e = layer // 2
            lambda_init = 0.8 - 0.6 * math.exp(-0.3 * layer)
            dq, dk, dvt, gq, gk, la, gv, gg = _proj_even(h, (mix_pre, layer), hyb_w_in[e], gla_w_a2[e], gla_b_a[e])
            a_out = _diff_attention(dq, dk, dvt, (diff_lambda, e), (subln, e), lambda_init, batch, seq)
            b_out = _gla(gq, gk, la, gv, gg, (gla_g, e), batch, seq)
            mix = (a_out, 0, b_out, 0, (hyb_out, e), (mix_post, layer))
        else:
            o = layer // 2
            q, k, vt = _proj_odd(h, (mix_pre, layer), sb_w_qkv[o])
            att = _stick_breaking(q, k, vt, batch, seq)
            mix = (att, 0, att, 1, (sb_out, o), (mix_post, layer))
        h = ffn(h, layer, 1, mix)
    return h.reshape(batch, seq, d)
```

```python
import functools
import math

import jax
import jax.numpy as jnp
from jax import lax
from jax.experimental import pallas as pl
from jax.experimental.pallas import tpu as pltpu

F32 = jnp.float32
BF16 = jnp.bfloat16

EPS = 1e-6
DIFF_HEADS = 4
DIFF_DH = 64
DIFF_DV = 2 * DIFF_DH
GLA_HEADS = 4
GLA_DK = 64
GLA_DV = 128
GLA_RANK = 16
GLA_TAU = 16.0
GLA_CHUNK = 64
GLA_UNROLL = 8
GLA_SUB = 8
SB_HEADS = 8

LANES = 128
VMEM_LIMIT = 56 * 1024 * 1024
DIFF_VMEM_LIMIT = 62 * 1024 * 1024

TOKEN_TILE = 1024
FFN_TILE = 1024
ROW_PART = 256
GLA_TILE = 512
ATTN_TILE = 256
ATTN_QSUB = 8
SB_QSUB = 16
DIFF_KV_UNROLL = 8
DIFF_RUN_BLOCKS = 8
DIFF_KV_GROUP = 2
ONES_ROWS = 16
LOG2_FLUSH = -150.0
LOG2E = math.log2(math.e)


def _cparams(sem, vmem_limit=VMEM_LIMIT):
    return pltpu.CompilerParams(dimension_semantics=sem, vmem_limit_bytes=vmem_limit)


def _picked(stacked, idx):
    return stacked, pl.BlockSpec((None,) + stacked.shape[1:], lambda *_: (idx,) + (0,) * (stacked.ndim - 1))


def _rms(x, g):
    return x * lax.rsqrt(jnp.mean(x * x, axis=-1, keepdims=True) + EPS) * g


def _dot(a, b):
    return jnp.dot(a, b, preferred_element_type=F32)


def _dot_nt(a, b):
    return lax.dot_general(a, b, (((1,), (1,)), ((), ())), preferred_element_type=F32)


def _dot_tn(a, b):
    return lax.dot_general(a, b, (((0,), (0,)), ((), ())), preferred_element_type=F32)


def _split3(x):
    hi = x.astype(BF16)
    r1 = x - hi.astype(F32)
    mid = r1.astype(BF16)
    lo = (r1 - mid.astype(F32)).astype(BF16)
    return hi, mid, lo


def _neg_abs(x):
    bits = lax.bitcast_convert_type(x, jnp.int32) | jnp.int32(-2 ** 31)
    return lax.bitcast_convert_type(bits, F32)


def _softplus_parts(z):
    return jnp.maximum(z, 0.0), jnp.log(1.0 + jnp.exp(-jnp.abs(z)))


def _ffn_body(hs, gpre_ref, wgu_ref, wd_ref, gpost_ref):
    dff = wd_ref.shape[0]
    xns = [_rms(h, gpre_ref[...]).astype(BF16) for h in hs]
    acts = []
    for xn in xns:
        gu = _dot(xn, wgu_ref[...])
        gate = gu[:, :dff]
        acts.append((gate * jax.nn.sigmoid(gate) * gu[:, dff:]).astype(BF16))
    fs = [_dot(act, wd_ref[...]) for act in acts]
    return [h + 0.5 * _rms(f, gpost_ref[...]) for h, f in zip(hs, fs)]


def _row_parts(tm):
    part = min(ROW_PART, tm)
    return [pl.ds(p * part, part) for p in range(tm // part)]


def _ffn_kernel(h_ref, gpre_ref, wgu_ref, wd_ref, gpost_ref, o_ref):
    parts = _row_parts(h_ref.shape[0])
    outs = _ffn_body([h_ref[r, :] for r in parts], gpre_ref, wgu_ref, wd_ref, gpost_ref)
    for r, o in zip(parts, outs):
        o_ref[r, :] = o


def _mix_ffn_kernel(h_ref, a_ref, b_ref, wo_ref, gmix_ref, gpre_ref, wgu_ref, wd_ref, gpost_ref, o_ref):
    parts = _row_parts(h_ref.shape[0])
    half = a_ref.shape[1]
    ys = [_dot(a_ref[r, :], wo_ref[0:half, :]) + _dot(b_ref[r, :], wo_ref[half:, :]) for r in parts]
    hs = [h_ref[r, :] + _rms(y, gmix_ref[...]) for r, y in zip(parts, ys)]
    outs = _ffn_body(hs, gpre_ref, wgu_ref, wd_ref, gpost_ref)
    for r, o in zip(parts, outs):
        o_ref[r, :] = o


def _ffn(h, g_pre, wgu_all, wd_all, idx, g_post, mix=None):
    n, d = h.shape
    dff = wd_all.shape[1]
    tm = min(FFN_TILE, n)
    row = lambda i: (i, 0)
    g_pre, g_pre_spec = _picked(*g_pre)
    g_post, g_post_spec = _picked(*g_post)
    wgu_all, wgu_spec = _picked(wgu_all, idx)
    wd_all, wd_spec = _picked(wd_all, idx)
    ffn_specs = [g_pre_spec, wgu_spec, wd_spec, g_post_spec]
    ffn_args = (g_pre, wgu_all, wd_all, g_post)
    if mix is None:
        body, specs, args, name = _ffn_kernel, [], (), "ffn"
    else:
        a, a_col, b, b_col, w_out, g_mix = mix
        w_out, w_out_spec = _picked(*w_out)
        g_mix, g_mix_spec = _picked(*g_mix)
        half = w_out.shape[1] // 2
        body, name = _mix_ffn_kernel, "mix_ffn"
        specs = [pl.BlockSpec((tm, half), lambda i: (i, a_col)), pl.BlockSpec((tm, half), lambda i: (i, b_col)),
                 w_out_spec, g_mix_spec]
        args = (a, b, w_out, g_mix)
    return pl.pallas_call(
        body,
        grid=(n // tm,),
        in_specs=[pl.BlockSpec((tm, d), row)] + specs + ffn_specs,
        out_specs=pl.BlockSpec((tm, d), row),
        out_shape=jax.ShapeDtypeStruct((n, d), F32),
        compiler_params=_cparams(("parallel",)),
        name=name,
    )(h, *args, *ffn_args)


def _store_key_blocks(vt_ref, vt, part):
    tk = vt_ref.shape[2]
    blocks = vt.shape[1] // tk
    for c in range(blocks):
        vt_ref[part * blocks + c] = vt[:, c * tk:(c + 1) * tk].astype(vt_ref.dtype)


def _proj_even_kernel(h_ref, g_ref, w_ref, wvt_ref, ones_ref, wga_ref, wa2_ref, ba_ref,
                      dq_ref, dk_ref, dvt_ref, gq_ref, gk_ref, la_ref, gv_ref, gg_ref):
    w_hi, w_lo, _ = _split3(wa2_ref[...])
    widths = [r.shape[1] for r in (dq_ref, dk_ref, gq_ref, gk_ref, gv_ref, gg_ref)]
    starts = [sum(widths[:i]) for i in range(len(widths))]
    parts = _row_parts(h_ref.shape[0])
    ms = [_rms(h_ref[r, :], g_ref[...]).astype(BF16) for r in parts]
    for p, (r, m) in enumerate(zip(parts, ms)):
        def seg(i):
            return _dot(m, w_ref[:, starts[i]:starts[i] + widths[i]])

        dq_ref[r, :] = (seg(0) * (DIFF_DH ** -0.5 * LOG2E)).astype(BF16)
        dk_ref[r, :] = seg(1).astype(BF16)
        _store_key_blocks(dvt_ref, _dot_nt(wvt_ref[...], m) + ones_ref[...], p)
        gq_ref[r, :] = seg(2) * GLA_DK ** -0.5
        gk_ref[r, :] = seg(3)
        gv_ref[r, :] = seg(4).astype(BF16)
        gg_ref[r, :] = seg(5)
        ga_hi, ga_lo, _ = _split3(_dot(m, wga_ref[...]))
        x = _dot(ga_hi, w_hi) + _dot(ga_lo, w_hi) + _dot(ga_hi, w_lo) + ba_ref[...]
        relu_neg, log_term = _softplus_parts(-x)
        la_ref[r, :] = -(relu_neg + log_term) * (1.0 / GLA_TAU)


def _proj_even(h, g, w_in, w_a2, b_a):
    n, d = h.shape
    g, g_spec = _picked(*g)
    tm = min(TOKEN_TILE, n)
    wd = DIFF_HEADS * DIFF_DV
    wk = GLA_HEADS * GLA_DK
    c = 0
    parts = []
    for width in (wd, wd, wd, wk, wk, GLA_HEADS * GLA_DV, GLA_HEADS * GLA_DV):
        parts.append(w_in[:, c:c + width])
        c += width
    w_vt = parts.pop(2).T.reshape(DIFF_HEADS, DIFF_DV, d)
    w_vt = jnp.pad(w_vt, ((0, 0), (0, ONES_ROWS), (0, 0))).reshape(-1, d).astype(BF16)
    ones_col = jnp.pad(jnp.zeros((DIFF_HEADS, DIFF_DV, 1), F32), ((0, 0), (0, ONES_ROWS), (0, 0)),
                       constant_values=1.0).reshape(-1, 1)
    vt_rows = w_vt.shape[0]
    w_main = jnp.concatenate(parts, axis=1).astype(BF16)
    tk = min(ATTN_TILE, tm)
    assert min(ROW_PART, tm) % tk == 0
    w_ga = jnp.pad(w_in[:, c:c + GLA_RANK], ((0, 0), (0, LANES - GLA_RANK))).astype(BF16)
    wa2 = jnp.pad(w_a2, ((0, LANES - GLA_RANK), (0, 0)))
    ba = b_a.reshape(1, wk)
    row = lambda i: (i, 0)
    fixed = lambda i: (0, 0)
    out_bf = jax.ShapeDtypeStruct((n, wd), BF16)
    out_f = jax.ShapeDtypeStruct((n, wd), F32)
    out_k = jax.ShapeDtypeStruct((n, wk), F32)
    out_vt = jax.ShapeDtypeStruct((n // tk, vt_rows, tk), BF16)
    rowspec = pl.BlockSpec((tm, wd), row)
    kspec = pl.BlockSpec((tm, wk), row)
    vtspec = pl.BlockSpec((tm // tk, vt_rows, tk), lambda i: (i, 0, 0))
    return pl.pallas_call(
        _proj_even_kernel,
        grid=(n // tm,),
        in_specs=[pl.BlockSpec((tm, d), row), g_spec,
                  pl.BlockSpec(w_main.shape, fixed), pl.BlockSpec(w_vt.shape, fixed),
                  pl.BlockSpec(ones_col.shape, fixed), pl.BlockSpec(w_ga.shape, fixed),
                  pl.BlockSpec(wa2.shape, fixed), pl.BlockSpec((1, wk), fixed)],
        out_specs=[rowspec, rowspec, vtspec, kspec, kspec, kspec, rowspec, rowspec],
        out_shape=[out_bf, out_bf, out_vt, out_k, out_k, out_k, out_bf, out_f],
        compiler_params=_cparams(("parallel",)),
        name="proj_even",
    )(h, g, w_main, w_vt, ones_col, w_ga, wa2, ba)


def _proj_odd_kernel(h_ref, g_ref, w_ref, wvt_ref, q_ref, k_ref, vt_ref):
    d = q_ref.shape[1]
    dh = d // SB_HEADS
    parts = _row_parts(h_ref.shape[0])
    ms = [_rms(h_ref[r, :], g_ref[...]).astype(BF16) for r in parts]
    for p, (r, m) in enumerate(zip(parts, ms)):
        q_ref[r, :] = (_dot(m, w_ref[:, 0:d]) * (dh ** -0.5 * LOG2E)).astype(BF16)
        k_ref[r, :] = _dot(m, w_ref[:, d:2 * d]).astype(BF16)
        _store_key_blocks(vt_ref, _dot_nt(wvt_ref[...], m), p)


def _proj_odd(h, g, w_qkv):
    n, d = h.shape
    g, g_spec = _picked(*g)
    tm = min(TOKEN_TILE, n)
    tk = min(ATTN_TILE, tm)
    row = lambda i: (i, 0)
    fixed = lambda i: (0, 0)
    out = jax.ShapeDtypeStruct((n, d), BF16)
    w_qk = w_qkv[:, :2 * d].astype(BF16)
    w_vt = w_qkv[:, 2 * d:].T.astype(BF16)
    return pl.pallas_call(
        _proj_odd_kernel,
        grid=(n // tm,),
        in_specs=[pl.BlockSpec((tm, d), row), g_spec,
                  pl.BlockSpec(w_qk.shape, fixed), pl.BlockSpec(w_vt.shape, fixed)],
        out_specs=[pl.BlockSpec((tm, d), row), pl.BlockSpec((tm, d), row),
                   pl.BlockSpec((tm // tk, d, tk), lambda i: (i, 0, 0))],
        out_shape=[out, out, jax.ShapeDtypeStruct((n // tk, d, tk), BF16)],
        compiler_params=_cparams(("parallel",)),
        name="proj_odd",
    )(h, g, w_qk, w_vt)


def _diff_kernel(lam_ref, g_ref, q_ref, k_ref, vt_ref, o_ref, m_ref, l_ref, acc_ref, *, lambda_init):
    i = pl.program_id(2)
    tk = vt_ref.shape[2]
    dv = DIFF_DV
    nsub = q_ref.shape[0] // tk
    base = i * nsub
    lane = lax.broadcasted_iota(jnp.int32, (tk, LANES), 1)
    causal = (lax.broadcasted_iota(jnp.int32, (tk, tk), 0) <= lax.broadcasted_iota(jnp.int32, (tk, tk), 1))

    def q_map(a, mi):
        q = q_ref[a * tk:(a + 1) * tk, :]
        keep = (lane < DIFF_DH) if mi == 0 else (lane >= DIFF_DH)
        return jnp.where(keep, q, jnp.zeros_like(q))

    def run_tiles(tiles, scores):
        pending = []
        for (a, mi, vtbs, first, masked), ss in zip(tiles, scores):
            if masked:
                ss = [jnp.where(causal, s, -jnp.inf) for s in ss]
            m_blk = functools.reduce(jnp.maximum, [jnp.max(s, axis=0, keepdims=True) for s in ss])
            if first:
                alpha = None
                m_new = m_blk
            else:
                m_prev = m_ref[a, mi]
                m_new = jnp.maximum(m_prev, m_blk)
                alpha = jnp.exp2(m_prev - m_new)
            m_ref[a, mi] = m_new
            p = jnp.concatenate([jnp.exp2(s - m_new).astype(BF16) for s in ss], axis=0)
            vtb = jnp.concatenate(vtbs, axis=1)
            pending.append((a, mi, alpha, _dot(vtb, p)))
        for a, mi, alpha, pv in pending:
            if alpha is None:
                acc_ref[a, mi] = pv[:dv]
                l_ref[a, mi] = pv[dv:dv + 1]
            else:
                acc_ref[a, mi] = alpha * acc_ref[a, mi] + pv[:dv]
                l_ref[a, mi] = alpha * l_ref[a, mi] + pv[dv:dv + 1]

    def keys(j):
        return k_ref[pl.ds(pl.multiple_of(j * tk, tk), tk), :]

    band_kb = [keys(base + c) for c in range(nsub)]
    band_vt = [vt_ref[base + c] for c in range(nsub)]
    per_chain = []
    for a in range(nsub):
        groups = [list(range(c, min(c + DIFF_KV_GROUP, a))) for c in range(0, a, DIFF_KV_GROUP)] + [[a]]
        per_chain.append(groups)
    band, band_scores = [], []
    for g in range(max(len(groups) for groups in per_chain)):
        for a, groups in enumerate(per_chain):
            if g < len(groups):
                for mi in range(2):
                    band.append((a, mi, [band_vt[c] for c in groups[g]], g == 0, groups[g] == [a]))
                    band_scores.append([_dot_nt(band_kb[c], q_map(a, mi)) for c in groups[g]])
    run_tiles(band, band_scores)

    kv_unroll = DIFF_KV_UNROLL if nsub % DIFF_KV_UNROLL == 0 else 1
    group = DIFF_KV_GROUP if kv_unroll % DIFF_KV_GROUP == 0 else 1

    def body(step, carry):
        for r in range(0, kv_unroll, DIFF_RUN_BLOCKS):
            tiles, scores = [], []
            for u in range(r, min(r + DIFF_RUN_BLOCKS, kv_unroll), group):
                blocks = [step * kv_unroll + u + w for w in range(group)]
                kbs = [keys(j) for j in blocks]
                vtbs = [vt_ref[j] for j in blocks]
                for a in range(nsub):
                    for mi in range(2):
                        tiles.append((a, mi, vtbs, False, False))
                        scores.append([_dot_nt(kb, q_map(a, mi)) for kb in kbs])
            run_tiles(tiles, scores)
        return carry

    lax.fori_loop(0, base // kv_unroll, body, 0)

    lp = lam_ref[...]
    lam = (jnp.exp(jnp.sum(lp[0:1] * lp[1:2], axis=-1, keepdims=True))
           - jnp.exp(jnp.sum(lp[2:3] * lp[3:4], axis=-1, keepdims=True)) + lambda_init)
    for a in range(nsub):
        o = acc_ref[a, 0] / l_ref[a, 0] - lam * (acc_ref[a, 1] / l_ref[a, 1])
        o = o * lax.rsqrt(jnp.mean(o * o, axis=0, keepdims=True) + EPS) * (1.0 - lambda_init)
        o_ref[a * tk:(a + 1) * tk, :] = (o.T * g_ref[...]).astype(o_ref.dtype)


def _diff_attention(dq, dk, dvt, lam_params, subln_g, lambda_init, batch, seq):
    n, wd = dq.shape
    lam_params, lam_spec = _picked(*lam_params)
    subln_g, g_spec = _picked(*subln_g)
    tk = dvt.shape[2]
    tq = min(ATTN_QSUB * tk, seq)
    nq = seq // tq
    nsub = tq // tk
    qspec = pl.BlockSpec((tq, LANES), lambda b, h, i: (b * nq + i, h))
    kspec = pl.BlockSpec((seq, LANES), lambda b, h, i: (b, h), pipeline_mode=pl.Buffered(1))
    vtspec = pl.BlockSpec((seq // tk, DIFF_DV + ONES_ROWS, tk), lambda b, h, i: (b, h, 0),
                          pipeline_mode=pl.Buffered(1))
    return pl.pallas_call(
        functools.partial(_diff_kernel, lambda_init=lambda_init),
        grid=(batch, DIFF_HEADS, nq),
        in_specs=[lam_spec, g_spec, qspec, kspec, vtspec],
        out_specs=qspec,
        out_shape=jax.ShapeDtypeStruct((n, wd), BF16),
        scratch_shapes=[pltpu.VMEM((nsub, 2, 1, tk), F32), pltpu.VMEM((nsub, 2, 1, tk), F32),
                        pltpu.VMEM((nsub, 2, DIFF_DV, tk), F32)],
        compiler_params=_cparams(("parallel", "parallel", "arbitrary"), DIFF_VMEM_LIMIT),
        name="diff_attn",
    )(lam_params, subln_g, dq, dk, dvt)


def _gla_kernel(q_ref, k_ref, la_ref, v_ref, gg_ref, g_ref, o_ref, state_ref):
    c = GLA_CHUNK
    sub = GLA_SUB
    nsub = c // sub
    batch = q_ref.shape[0]
    heads = state_ref.shape[0] // batch
    pair_lane = lax.broadcasted_iota(jnp.int32, (c, LANES), 1)
    pair_lane3 = lax.broadcasted_iota(jnp.int32, (nsub, sub, LANES), 2)
    own = [pair_lane < GLA_DK, pair_lane >= GLA_DK]
    own3 = [pair_lane3 < GLA_DK, pair_lane3 >= GLA_DK]

    @pl.when(pl.program_id(0) == 0)
    def _():
        state_ref[...] = jnp.zeros(state_ref.shape, F32)

    r_i = lax.broadcasted_iota(jnp.int32, (c, c), 0)
    c_i = lax.broadcasted_iota(jnp.int32, (c, c), 1)
    tril = (c_i <= r_i).astype(BF16)
    levels = []
    s = c // 2
    while s >= sub:
        levels.append((s, (r_i // (2 * s) == c_i // (2 * s)) & (r_i % (2 * s) >= s) & (c_i % (2 * s) < s)))
        s //= 2
    diag_mask = (r_i // sub == c_i // sub) & (c_i <= r_i)
    lane3 = lax.broadcasted_iota(jnp.int32, (nsub, sub, c), 2)
    blk3 = lax.broadcasted_iota(jnp.int32, (nsub, sub, c), 0)
    g = g_ref[...]

    def step(si, carry):
        cols = [slice(h * LANES, (h + 1) * LANES) for h in range(heads)]
        rows = [pl.ds(pl.multiple_of((si * GLA_UNROLL + u) * c, c), c) for u in range(GLA_UNROLL)]

        pairs = [slice(p * LANES, (p + 1) * LANES) for p in range(heads // 2)]

        chains = [(b, u, p) for u in range(GLA_UNROLL) for b in range(batch) for p in range(heads // 2)]
        cums = {}
        for b, u, p in chains:
            la_hi, la_mid, la_lo = _split3(la_ref[b, rows[u], pairs[p]])
            cums[b, u, p] = _dot(tril, la_hi) + _dot(tril, la_mid) + _dot(tril, la_lo)

        states = [state_ref[i] for i in range(batch * heads)]
        mids = {}
        for b, u, p in chains:
            q = q_ref[b, rows[u], pairs[p]]
            k = k_ref[b, rows[u], pairs[p]]
            cum = cums[b, u, p]
            last = cum[c - 1:c, :]
            q_dec = (q * jnp.exp(cum)).astype(BF16)
            k_dec = k * jnp.exp(last - cum)
            state_decay = jnp.exp(last)

            level_ops = []
            for s, _ in levels:
                ref = jnp.concatenate([jnp.broadcast_to(cum[lo + s:lo + s + 1, :], (2 * s, LANES))
                                       for lo in range(0, c, 2 * s)], axis=0)
                qs = q * jnp.exp(jnp.minimum(cum - ref, 0.0))
                ks = k * jnp.exp(jnp.minimum(ref - cum, 0.0))
                level_ops.append((qs, ks.astype(BF16)))

            q3 = q.reshape(nsub, sub, LANES)
            k3 = k.reshape(nsub, sub, LANES)
            c3 = cum.reshape(nsub, sub, LANES)
            a3 = [jnp.zeros((nsub, sub, c), F32) for _ in range(2)]
            for j in range(sub):
                dec = jnp.exp(jnp.minimum(c3 - c3[:, j:j + 1, :], 0.0))
                prod = q3 * k3[:, j:j + 1, :] * dec
                for t in range(2):
                    col = jnp.sum(jnp.where(own3[t], prod, 0.0), axis=-1, keepdims=True)
                    a3[t] = jnp.where(lane3 == blk3 * sub + j, col, a3[t])

            for t in range(2):
                h = 2 * p + t
                v = v_ref[b, rows[u], cols[h]]
                state = states[b * heads + h]
                o_inter = _dot_nt(q_dec, state.astype(BF16))
                states[b * heads + h] = (state * state_decay
                                         + _dot_tn(v, jnp.where(own[t], k_dec, 0.0).astype(BF16)))
                attn = jnp.where(diag_mask, a3[t].reshape(c, c), 0.0)
                for (_, mask), (qs, ks) in zip(levels, level_ops):
                    a_level = _dot_nt(jnp.where(own[t], qs, 0.0).astype(BF16), ks)
                    attn = jnp.where(mask, a_level, attn)
                mids[b, u, h] = (o_inter, _dot(attn.astype(BF16), v))

        for i in range(batch * heads):
            state_ref[i] = states[i]
        for (b, u, h), (o_inter, o_intra) in mids.items():
            gate = gg_ref[b, rows[u], cols[h]]
            o_ref[b, rows[u], cols[h]] = (_rms(o_inter + o_intra, g)
                                          * (gate * jax.nn.sigmoid(gate))).astype(o_ref.dtype)
        return carry

    lax.fori_loop(0, q_ref.shape[1] // (c * GLA_UNROLL), step, 0)


def _gla(gq, gk, la, gv, gg, norm_g, batch, seq):
    n, wd = gv.shape
    norm_g, g_spec = _picked(*norm_g)
    t = min(GLA_TILE, seq)
    wk = gq.shape[1]
    spec = pl.BlockSpec((batch, t, wd), lambda i: (0, i, 0))
    kspec = pl.BlockSpec((batch, t, wk), lambda i: (0, i, 0))
    as_rows = lambda x: x.reshape(batch, seq, x.shape[1])
    out = pl.pallas_call(
        _gla_kernel,
        grid=(seq // t,),
        in_specs=[kspec, kspec, kspec, spec, spec, g_spec],
        out_specs=spec,
        out_shape=jax.ShapeDtypeStruct((batch, seq, wd), BF16),
        scratch_shapes=[pltpu.VMEM((batch * GLA_HEADS, GLA_DV, LANES), F32)],
        compiler_params=_cparams(("arbitrary",)),
        name="gla",
    )(as_rows(gq), as_rows(gk), as_rows(la), as_rows(gv), as_rows(gg), norm_g)
    return out.reshape(n, wd)


SUM_ROWS = 8


def _sb_kernel(q_ref, k_ref, vt_ref, o_ref, acc_ref):
    i = pl.program_id(2)
    tk = vt_ref.shape[2]
    nsub = q_ref.shape[0] // tk
    base = i * nsub
    r_s = lax.broadcasted_iota(jnp.int32, (tk + SUM_ROWS, tk), 0)
    c_s = lax.broadcasted_iota(jnp.int32, (tk + SUM_ROWS, tk), 1)
    suffix_mat = ((c_s > r_s) | (r_s >= tk)).astype(BF16)
    strict = (lax.broadcasted_iota(jnp.int32, (tk, tk), 0) < lax.broadcasted_iota(jnp.int32, (tk, tk), 1))

    def run_tiles(tiles, later):
        later = list(later)
        logits = [_dot_nt(kb, q_ref[a * tk:(a + 1) * tk, :]) for a, kb, _, _, _ in tiles]
        mids = []
        for (a, _, _, masked, _), z in zip(tiles, logits):
            log_term = jnp.log2(1.0 + jnp.exp2(_neg_abs(z)))
            log_beta = jnp.minimum(z, 0.0) - log_term
            log_keep = log_beta - z
            if masked:
                log_keep = jnp.where(strict, log_keep, 0.0)
            mids.append((log_beta, _dot(suffix_mat, log_keep.astype(BF16))))
        pending = []
        for (a, _, vtb, masked, valid), (log_beta, sums) in zip(tiles, mids):
            arg = log_beta + sums[:tk]
            if later[a] is not None:
                if valid is not None:
                    later[a] = jnp.where(valid, later[a], -jnp.inf)
                arg = arg + later[a]
            att = jnp.exp2(arg)
            if masked:
                att = jnp.where(strict, att, 0.0)
            pending.append((a, later[a] is None, _dot(vtb, att.astype(BF16))))
            later[a] = sums[tk:tk + 1] if later[a] is None else later[a] + sums[tk:tk + 1]
        for a, first, contrib in pending:
            if first:
                acc_ref[a] = contrib
            else:
                acc_ref[a] += contrib
        return tuple(later)

    def key_block(j):
        return k_ref[pl.ds(pl.multiple_of(j * tk, tk), tk), :], vt_ref[j]

    def lockstep_tiles(offset):
        tiles = []
        for a in range(nsub):
            j = base + a - offset
            kb, vtb = key_block(jnp.maximum(j, 0))
            tiles.append((a, kb, vtb, False, j >= 0))
        return tiles

    diag = []
    for a in range(nsub):
        kb, vtb = key_block(base + a)
        diag.append((a, kb, vtb, True, None))
    later = run_tiles(diag + lockstep_tiles(1), [None] * nsub)

    def alive(later):
        return (jnp.max(functools.reduce(jnp.maximum, later)) > LOG2_FLUSH).astype(jnp.int32)

    def cond(carry):
        step, live, _ = carry
        return (step < base + nsub - 2) & (live > 0)

    def body(carry):
        step, _, later = carry
        later = run_tiles(lockstep_tiles(step + 2), later)
        return step + 1, alive(later), later

    lax.while_loop(cond, body, (jnp.int32(0), alive(later), later))
    for a in range(nsub):
        o_ref[a * tk:(a + 1) * tk, :] = acc_ref[a].T.astype(o_ref.dtype)


def _stick_breaking(q, k, vt, batch, seq):
    n, d = q.shape
    tk = vt.shape[2]
    tq = min(SB_QSUB * tk, seq)
    nq = seq // tq
    qspec = pl.BlockSpec((tq, LANES), lambda b, h, i: (b * nq + i, h))
    kspec = pl.BlockSpec((seq, LANES), lambda b, h, i: (b, h))
    vtspec = pl.BlockSpec((seq // tk, LANES, tk), lambda b, h, i: (b, h, 0))
    return pl.pallas_call(
        _sb_kernel,
        grid=(batch, SB_HEADS, nq),
        in_specs=[qspec, kspec, vtspec],
        out_specs=qspec,
        out_shape=jax.ShapeDtypeStruct((n, d), BF16),
        scratch_shapes=[pltpu.VMEM((tq // tk, LANES, tk), F32)],
        compiler_params=_cparams(("parallel", "parallel", "arbitrary")),
        name="stick_breaking",
    )(q, k, vt)


def kernel(x, ffn_pre_g, ffn_post_g, ffn_w_gate, ffn_w_up, ffn_w_down, mix_pre_g, mix_post_g, hyb_w_in, hyb_w_out, diff_lambda, diff_subln_g, gla_w_a2, gla_b_a, gla_norm_g, sb_w_qkv, sb_w_out):
    batch, seq, d = x.shape
    depth = ffn_pre_g.shape[0]
    assert d // SB_HEADS == LANES and DIFF_DV == LANES and GLA_DV == LANES
    h = x.reshape(batch * seq, d)

    n_ffn = ffn_w_gate.shape[1]
    dff = ffn_w_gate.shape[-1]
    wgu_all = jnp.concatenate([ffn_w_gate, ffn_w_up], axis=-1).astype(BF16).reshape(depth * n_ffn, d, 2 * dff)
    wd_all = ffn_w_down.astype(BF16).reshape(depth * n_ffn, dff, d)

    ffn_pre = ffn_pre_g.reshape(depth * n_ffn, 1, d)
    ffn_post = ffn_post_g.reshape(depth * n_ffn, 1, d)
    mix_pre = mix_pre_g.reshape(depth, 1, d)
    mix_post = mix_post_g.reshape(depth, 1, d)
    subln = diff_subln_g.reshape(-1, 1, DIFF_DV)
    gla_g = gla_norm_g.reshape(-1, 1, GLA_DV)
    hyb_out = hyb_w_out.astype(BF16)
    sb_out = sb_w_out.astype(BF16)

    def ffn(h, layer, j, mix=None):
        idx = layer * n_ffn + j
        return _ffn(h, (ffn_pre, idx), wgu_all, wd_all, idx, (ffn_post, idx), mix)

    for layer in range(depth):
        h = ffn(h, layer, 0)
        if layer % 2 == 0:
            e = layer // 2
            lambda_init = 0.8 - 0.6 * math.exp(-0.3 * layer)
            dq, dk, dvt, gq, gk, la, gv, gg = _proj_even(h, (mix_pre, layer), hyb_w_in[e], gla_w_a2[e], gla_b_a[e])
            a_out = _diff_attention(dq, dk, dvt, (diff_lambda, e), (subln, e), lambda_init, batch, seq)
            b_out = _gla(gq, gk, la, gv, gg, (gla_g, e), batch, seq)
            mix = (a_out, 0, b_out, 0, (hyb_out, e), (mix_post, layer))
        else:
            o = layer // 2
            q, k, vt = _proj_odd(h, (mix_pre, layer), sb_w_qkv[o])
            att = _stick_breaking(q, k, vt, batch, seq)
            mix = (att, 0, att, 1, (sb_out, o), (mix_post, layer))
        h = ffn(h, layer, 1, mix)
    return h.reshape(batch, seq, d)
```
